```python
import math
import jax
import jax.numpy as jnp
from jax import lax
import numpy as np

D_MODEL = 2048
BATCH = 16
SEQ = 2048
DEPTH = 1
DEC_BATCH = 32
DEC_SEQ = 8
PAST_LEN = 16384
PAGE_SIZE = 128

HEAD_DIM = 128
N_NSA_HEADS = D_MODEL // (2 * HEAD_DIM)
N_NSA_KV = N_NSA_HEADS // 4
NSA_REP = N_NSA_HEADS // N_NSA_KV
N_FOX_HEADS = D_MODEL // (2 * HEAD_DIM)
NSA_WIDTH = N_NSA_HEADS * HEAD_DIM
FOX_WIDTH = N_FOX_HEADS * HEAD_DIM
NSA_KV_WIDTH = 2 * N_NSA_KV * HEAD_DIM
IN_WIDTH = NSA_WIDTH + 3 * NSA_KV_WIDTH + 3 * N_NSA_HEADS + 3 * FOX_WIDTH + N_FOX_HEADS
CMP_BLOCK = 32
CMP_STRIDE = 16
CMP_SLOTS = CMP_BLOCK // CMP_STRIDE
SLC_BLOCK = 64
N_SELECT = 16
WINDOW = 512
D_FF = 4 * D_MODEL
N_BUCKETS = 32
MAX_DISTANCE = 128
Q_BLOCK = 128
SLC_Q_BLOCK = 16
RMS_EPS = 1e-6
NEG_INF = -1e30
FORCE_SCORE = 1e9
ATTN_SCALE = HEAD_DIM ** -0.5

kernel_name = "hymba_nsa_fox_decode_step"


def _rmsnorm(x, g):
    xf = x.astype(jnp.float32)
    xf = xf * lax.rsqrt(jnp.mean(xf * xf, axis=-1, keepdims=True) + RMS_EPS)
    return xf.astype(x.dtype) * g


def _t5_bucket(dist):
    n = jnp.maximum(dist, 0)
    exact = N_BUCKETS // 2
    nf = jnp.maximum(n, 1).astype(jnp.float32)
    far = exact + (jnp.log(nf / exact) / math.log(MAX_DISTANCE / exact) * (N_BUCKETS - exact)).astype(jnp.int32)
    return jnp.where(n < exact, n, jnp.minimum(far, N_BUCKETS - 1))


def _masked_softmax(s, mask, axis):
    p = jax.nn.softmax(jnp.where(mask, s, NEG_INF), axis=axis)
    return jnp.where(mask, p, 0.0)


def _project(x, g_attn, w_in, b_f):
    B, T, _ = x.shape
    h = _rmsnorm(x, g_attn) @ w_in
    sizes = [NSA_WIDTH, NSA_KV_WIDTH, NSA_KV_WIDTH, NSA_KV_WIDTH, 3 * N_NSA_HEADS,
             FOX_WIDTH, 2 * FOX_WIDTH, N_FOX_HEADS]
    cuts, acc = [], 0
    for s in sizes[:-1]:
        acc += s
        cuts.append(acc)
    q, ckv, skv, wkv, gt, qf, kvf, fl = jnp.split(h, cuts, axis=-1)
    kv_shape = (B, T, 2, N_NSA_KV, HEAD_DIM)
    return (q.reshape(B, T, N_NSA_KV, NSA_REP, HEAD_DIM),
            ckv.reshape(kv_shape), skv.reshape(kv_shape), wkv.reshape(kv_shape),
            jax.nn.sigmoid(gt).reshape(B, T, 3, N_NSA_KV, NSA_REP),
            qf.reshape(B, T, N_FOX_HEADS, HEAD_DIM),
            kvf.reshape(B, T, 2, N_FOX_HEADS, HEAD_DIM),
            jax.nn.log_sigmoid((fl + b_f).astype(jnp.float32)))


def _cmp_chunk_proj(kv, w_cmp, pe_cmp):
    B, L = kv.shape[:2]
    n = L // CMP_STRIDE
    ch = kv[:, :n * CMP_STRIDE].reshape(B, n, CMP_STRIDE, 2, N_NSA_KV, HEAD_DIM)
    w = w_cmp.reshape(2, CMP_SLOTS, CMP_STRIDE, HEAD_DIM, HEAD_DIM)
    pe = pe_cmp.reshape(2, CMP_SLOTS, CMP_STRIDE, HEAD_DIM)
    proj = jnp.einsum('bnrcgd,curde->bnucge', ch, w)
    pos_term = jnp.einsum('curd,curde->uce', pe, w)
    return proj + pos_term[None, None, :, :, None, :]


def _cmp_blocks(P):
    nb = P.shape[1] - CMP_SLOTS + 1
    out = P[:, 0:nb, 0]
    for u in range(1, CMP_SLOTS):
        out = out + P[:, u:u + nb, u]
    return out


def _nsa_compressed(q, q_pos, ckv, table):
    n_blk = ckv.shape[1]
    k_pos = jnp.arange(n_blk) * CMP_STRIDE + (CMP_BLOCK - 1)
    dist = q_pos[:, None] - k_pos[None, :]
    bias = table[_t5_bucket(dist)].transpose(2, 0, 1).reshape(N_NSA_KV, NSA_REP, q_pos.shape[0], n_blk)
    s = jnp.einsum('btgrd,bngd->bgrtn', q, ckv[:, :, 0]).astype(jnp.float32) * ATTN_SCALE + bias
    p = _masked_softmax(s, dist >= 0, -1)
    o = jnp.einsum('bgrtn,bngd->btgrd', p.astype(ckv.dtype), ckv[:, :, 1])
    return o, p


def _nsa_select(p, q_pos, total_len):
    n_cmp = p.shape[-1]
    n_slc = -(-total_len // SLC_BLOCK)
    c0 = jnp.arange(n_cmp)[:, None] * CMP_STRIDE
    s0 = jnp.arange(n_slc)[None, :] * SLC_BLOCK
    cover = ((c0 < s0 + SLC_BLOCK) & (c0 + CMP_BLOCK > s0)).astype(p.dtype)
    imp = jnp.einsum('bgrtn,ns->bgts', p, cover)
    cur = (q_pos // SLC_BLOCK)[:, None]
    blk = jnp.arange(n_slc)[None, :]
    forced = (blk == 0) | (blk == cur) | (blk == cur - 1)
    score = jnp.where(forced, FORCE_SCORE, jnp.where(blk <= cur, imp, -FORCE_SCORE))
    _, idx = lax.top_k(score, min(N_SELECT, n_slc))
    return idx


def _nsa_selected(q, q_pos, idx, fetch, table):
    B, T, G, R, Dh = q.shape
    qc = math.gcd(T, SLC_Q_BLOCK)
    nc = T // qc
    k = idx.shape[-1]
    qs = q.reshape(B, nc, qc, G, R, Dh).swapaxes(0, 1)
    ids = idx.reshape(B, G, nc, qc, k).transpose(2, 0, 1, 3, 4)
    ps = q_pos.reshape(nc, qc)
    table_g = table.reshape(N_BUCKETS, G, R).transpose(1, 0, 2)
    offs = jnp.arange(SLC_BLOCK)

    def block(args):
        qb, ib, pb = args
        kv = fetch(ib)
        s = jnp.einsum('bqgrd,bgqkpd->bgrqkp', qb, kv[..., 0, :]).astype(jnp.float32) * ATTN_SCALE
        dist = pb[None, None, :, None, None] - (ib[..., None] * SLC_BLOCK + offs)
        bias = jax.vmap(lambda tg, bk: tg[bk], in_axes=(0, 1), out_axes=1)(table_g, _t5_bucket(dist))
        s = s + jnp.moveaxis(bias, -1, 2)
        p = _masked_softmax(s, (dist >= 0)[:, :, None], (-2, -1))
        return jnp.einsum('bgrqkp,bgqkpd->bqgrd', p.astype(kv.dtype), kv[..., 1, :])

    o = lax.map(block, (qs, ids, ps))
    return o.swapaxes(0, 1).reshape(B, T, G, R, Dh)


def _win_attend(q, q_pos, kv, k_pos, table):
    Tq, Tk = q_pos.shape[0], k_pos.shape[0]
    dist = q_pos[:, None] - k_pos[None, :]
    bias = table[_t5_bucket(dist)].transpose(2, 0, 1).reshape(N_NSA_KV, NSA_REP, Tq, Tk)
    s = jnp.einsum('btgrd,bsgd->bgrts', q, kv[:, :, 0]).astype(jnp.float32) * ATTN_SCALE + bias
    mask = (dist >= 0) & (dist <= WINDOW) & (k_pos[None, :] >= 0)
    p = _masked_softmax(s, mask, -1)
    return jnp.einsum('bgrts,bsgd->btgrd', p.astype(kv.dtype), kv[:, :, 1])


def _win_prompt(q, kv, table):
    B, T, G, R, Dh = q.shape
    qb = math.gcd(T, Q_BLOCK)
    nb = T // qb
    kvp = jnp.pad(kv, ((0, 0), (WINDOW, 0), (0, 0), (0, 0), (0, 0)))
    qs = q.reshape(B, nb, qb, G, R, Dh).swapaxes(0, 1)

    def block(args):
        i, qi = args
        start = i * qb
        kb = lax.dynamic_slice_in_dim(kvp, start, WINDOW + qb, axis=1)
        return _win_attend(qi, start + jnp.arange(qb), kb, start - WINDOW + jnp.arange(WINDOW + qb), table)

    o = lax.map(block, (jnp.arange(nb), qs))
    return o.swapaxes(0, 1).reshape(B, T, G, R, Dh)


def _fox_attend(q, q_pos, c_q, segments):
    B, T, H, Dh = q.shape
    qb = math.gcd(T, Q_BLOCK)
    nb = T // qb
    qs = q.reshape(B, nb, qb, H, Dh).swapaxes(0, 1)
    cs = c_q.reshape(B, nb, qb, H).swapaxes(0, 1)
    ps = q_pos.reshape(nb, qb)
    sizes = [seg[0].shape[1] for seg in segments]

    def block(args):
        qi, ci, pi = args
        logits = []
        for kv, ck, kp in segments:
            s = jnp.einsum('bqhd,bshd->bhqs', qi, kv[:, :, 0]).astype(jnp.float32) * ATTN_SCALE
            s = s + jnp.swapaxes(ci, 1, 2)[..., None] - jnp.swapaxes(ck, 1, 2)[:, :, None, :]
            logits.append(jnp.where(kp[None, :] <= pi[:, None], s, NEG_INF))
        p = jax.nn.softmax(jnp.concatenate(logits, axis=-1), axis=-1)
        out, start = None, 0
        for (kv, _, _), n in zip(segments, sizes):
            part = jnp.einsum('bhqs,bshd->bqhd', p[..., start:start + n].astype(kv.dtype), kv[:, :, 1])
            out = part if out is None else out + part
            start += n
        return out

    o = lax.map(block, (qs, cs, ps))
    return o.swapaxes(0, 1).reshape(B, T, H * Dh)


def _finish(x, gate, o_c, o_s, o_w, o_f, g_nsa_out, g_fox_out, w_o, g_mlp, w_up, w_down):
    B, T = x.shape[:2]
    o_nsa = (gate[:, :, 0, :, :, None] * o_c + gate[:, :, 1, :, :, None] * o_s
             + gate[:, :, 2, :, :, None] * o_w).reshape(B, T, NSA_WIDTH)
    mixed = jnp.concatenate([_rmsnorm(o_nsa, g_nsa_out), _rmsnorm(o_f, g_fox_out)], axis=-1)
    h = x + mixed @ w_o
    u = jax.nn.relu(_rmsnorm(h, g_mlp) @ w_up)
    return h + (u * u) @ w_down


def _prompt_layer(x, table, g_attn, w_in, b_f, w_cmp, pe_cmp, g_nsa_out, g_fox_out, w_o, g_mlp, w_up, w_down):
    B, T, _ = x.shape
    q, ckv, skv, wkv, gate, qf, kvf, logf = _project(x, g_attn, w_in, b_f)
    pos = jnp.arange(T)
    o_c, p_c = _nsa_compressed(q, pos, _cmp_blocks(_cmp_chunk_proj(ckv, w_cmp, pe_cmp)), table)
    idx = _nsa_select(p_c, pos, T)
    n_blk = -(-T // SLC_BLOCK)
    blocks = jnp.pad(skv, ((0, 0), (0, n_blk * SLC_BLOCK - T), (0, 0), (0, 0), (0, 0))).reshape(
        B, n_blk, SLC_BLOCK, 2, N_NSA_KV, HEAD_DIM)
    bi = jnp.arange(B)[:, None, None, None]
    gi = jnp.arange(N_NSA_KV)[None, :, None, None]
    o_s = _nsa_selected(q, pos, idx, lambda ib: blocks[bi, ib, :, :, gi], table)
    o_w = _win_prompt(q, wkv, table)
    c = jnp.cumsum(logf, axis=1)
    o_f = _fox_attend(qf, pos, c, ((kvf, c, pos),))
    y = _finish(x, gate, o_c, o_s, o_w, o_f, g_nsa_out, g_fox_out, w_o, g_mlp, w_up, w_down)
    return y, (ckv, skv, kvf, logf, wkv[:, T - min(WINDOW, T):])


def _sample_layer(x, cache_cmp_kv, cache_slc_kv, cache_fox_kv, cache_fox_logf, win_buf, page_table, layer, table,
                  g_attn, w_in, b_f, w_cmp, pe_cmp, g_nsa_out, g_fox_out, w_o, g_mlp, w_up, w_down):
    B, T, _ = x.shape
    q, ckv, skv, wkv, gate, qf, kvf, logf = _project(x, g_attn, w_in, b_f)
    pos = PAST_LEN + jnp.arange(T)
    cmp_past = cache_cmp_kv[layer, page_table].reshape(B, PAST_LEN, 2, N_NSA_KV, HEAD_DIM)
    chunks = jnp.concatenate([_cmp_chunk_proj(cmp_past, w_cmp, pe_cmp), _cmp_chunk_proj(ckv, w_cmp, pe_cmp)], axis=1)
    o_c, p_c = _nsa_compressed(q, pos, _cmp_blocks(chunks), table)
    idx = _nsa_select(p_c, pos, PAST_LEN + T)
    n_past_blk = PAST_LEN // SLC_BLOCK
    n_new_blk = -(-T // SLC_BLOCK)
    new_blocks = jnp.pad(skv, ((0, 0), (0, n_new_blk * SLC_BLOCK - T), (0, 0), (0, 0), (0, 0))).reshape(
        B, n_new_blk, SLC_BLOCK, 2, N_NSA_KV, HEAD_DIM)
    blocks_per_page = PAGE_SIZE // SLC_BLOCK
    bi = jnp.arange(B)[:, None, None, None]
    gi = jnp.arange(N_NSA_KV)[None, :, None, None]
    offs = jnp.arange(SLC_BLOCK)

    def fetch(ib):
        pb = jnp.minimum(ib, n_past_blk - 1)
        phys = page_table[bi, pb // blocks_per_page]
        rows = (pb % blocks_per_page)[..., None] * SLC_BLOCK + offs
        from_pool = cache_slc_kv[layer, phys[..., None], rows, :, gi[..., None]]
        from_new = new_blocks[bi, jnp.clip(ib - n_past_blk, 0, n_new_blk - 1), :, :, gi]
        return jnp.where((ib < n_past_blk)[..., None, None, None], from_pool, from_new)

    o_s = _nsa_selected(q, pos, idx, fetch, table)
    wb = win_buf.shape[1]
    win_all = jnp.concatenate([win_buf, wkv], axis=1)
    o_w = _win_attend(q, pos, win_all, PAST_LEN - wb + jnp.arange(wb + T), table)
    kv_past = cache_fox_kv[layer, page_table].reshape(B, PAST_LEN, 2, N_FOX_HEADS, HEAD_DIM)
    logf_past = cache_fox_logf[layer, page_table].reshape(B, PAST_LEN, N_FOX_HEADS)
    c_all = jnp.cumsum(jnp.concatenate([logf_past.astype(jnp.float32), logf], axis=1), axis=1)
    c_past, c_new = c_all[:, :PAST_LEN], c_all[:, PAST_LEN:]
    o_f = _fox_attend(qf, pos, c_new, ((kv_past, c_past, jnp.arange(PAST_LEN)), (kvf, c_new, pos)))
    y = _finish(x, gate, o_c, o_s, o_w, o_f, g_nsa_out, g_fox_out, w_o, g_mlp, w_up, w_down)
    return y, (ckv, skv, kvf, logf, win_all[:, -wb:])


def setup_inputs(seed: int = 0) -> dict:
    key = jax.random.key(seed)
    ks = jax.random.split(key, 24)
    f32 = jnp.float32
    n_pages = PAST_LEN // PAGE_SIZE
    n_used = DEC_BATCH * n_pages
    n_phys = n_used + (n_used + 3) // 4
    win_buf = min(WINDOW, PAST_LEN)

    def nrm(k, shape, scale):
        return scale * jax.random.normal(k, shape, f32)

    def gain(k, shape):
        return 1.0 + 0.02 * jax.random.normal(k, shape, f32)

    page_table = jax.random.permutation(ks[0], n_phys)[:n_used].reshape(DEC_BATCH, n_pages).astype(jnp.int32)
    return {
        "x_prompt": jax.random.normal(ks[1], (BATCH, SEQ, D_MODEL), f32),
        "x_sample": jax.random.normal(ks[2], (DEC_BATCH, DEC_SEQ, D_MODEL), f32),
        "cache_cmp_kv": jax.random.normal(ks[3], (DEPTH, n_phys, PAGE_SIZE, 2, N_NSA_KV, HEAD_DIM), f32),
        "cache_slc_kv": jax.random.normal(ks[4], (DEPTH, n_phys, PAGE_SIZE, 2, N_NSA_KV, HEAD_DIM), f32),
        "cache_fox_kv": jax.random.normal(ks[5], (DEPTH, n_phys, PAGE_SIZE, 2, N_FOX_HEADS, HEAD_DIM), f32),
        "cache_fox_logf": jax.nn.log_sigmoid(jax.random.normal(ks[6], (DEPTH, n_phys, PAGE_SIZE, N_FOX_HEADS), f32)),
        "state_win_kv": jax.random.normal(ks[7], (DEPTH, DEC_BATCH, win_buf, 2, N_NSA_KV, HEAD_DIM), f32),
        "page_table": page_table,
        "t5_table": nrm(ks[8], (N_BUCKETS, N_NSA_HEADS), 0.5),
        "g_attn": gain(ks[9], (DEPTH, D_MODEL)),
        "w_in": nrm(ks[10], (DEPTH, D_MODEL, IN_WIDTH), D_MODEL ** -0.5),
        "b_f": nrm(ks[11], (DEPTH, N_FOX_HEADS), 0.1),
        "w_cmp": nrm(ks[12], (DEPTH, 2, CMP_BLOCK, HEAD_DIM, HEAD_DIM), (CMP_BLOCK * HEAD_DIM) ** -0.5),
        "pe_cmp": nrm(ks[13], (DEPTH, 2, CMP_BLOCK, HEAD_DIM), 0.1),
        "g_nsa_out": gain(ks[14], (DEPTH, NSA_WIDTH)),
        "g_fox_out": gain(ks[15], (DEPTH, FOX_WIDTH)),
        "w_o": nrm(ks[16], (DEPTH, NSA_WIDTH + FOX_WIDTH, D_MODEL), (NSA_WIDTH + FOX_WIDTH) ** -0.5),
        "g_mlp": gain(ks[17], (DEPTH, D_MODEL)),
        "w_up": nrm(ks[18], (DEPTH, D_MODEL, D_FF), D_MODEL ** -0.5),
        "w_down": nrm(ks[19], (DEPTH, D_FF, D_MODEL), D_FF ** -0.5),
        "g_final": gain(ks[20], (D_MODEL,)),
    }


def reference(x_prompt, x_sample, cache_cmp_kv, cache_slc_kv, cache_fox_kv, cache_fox_logf, state_win_kv,
              page_table, t5_table, g_attn, w_in, b_f, w_cmp, pe_cmp, g_nsa_out, g_fox_out, w_o, g_mlp,
              w_up, w_down, g_final):
    h_p, h_s = x_prompt, x_sample
    st_p, st_s = [], []
    for layer in range(DEPTH):
        lw = (g_attn[layer], w_in[layer], b_f[layer], w_cmp[layer], pe_cmp[layer], g_nsa_out[layer],
              g_fox_out[layer], w_o[layer], g_mlp[layer], w_up[layer], w_down[layer])
        h_p, new_p = _prompt_layer(h_p, t5_table, *lw)
        h_s, new_s = _sample_layer(h_s, cache_cmp_kv, cache_slc_kv, cache_fox_kv, cache_fox_logf,
                                   state_win_kv[layer], page_table, layer, t5_table, *lw)
        st_p.append(new_p)
        st_s.append(new_s)
    y_prompt = _rmsnorm(h_p, g_final)
    y_sample = _rmsnorm(h_s, g_final)
    new_cmp_kv_p = jnp.stack([s[0] for s in st_p])
    new_slc_kv_p = jnp.stack([s[1] for s in st_p])
    new_fox_kv_p = jnp.stack([s[2] for s in st_p])
    new_fox_logf_p = jnp.stack([s[3] for s in st_p])
    new_win_kv_p = jnp.stack([s[4] for s in st_p])
    new_cmp_kv_s = jnp.stack([s[0] for s in st_s])
    new_slc_kv_s = jnp.stack([s[1] for s in st_s])
    new_fox_kv_s = jnp.stack([s[2] for s in st_s])
    new_fox_logf_s = jnp.stack([s[3] for s in st_s])
    new_win_kv_s = jnp.stack([s[4] for s in st_s])
    return (y_prompt, y_sample, new_cmp_kv_p, new_slc_kv_p, new_fox_kv_p, new_fox_logf_p, new_win_kv_p,
            new_cmp_kv_s, new_slc_kv_s, new_fox_kv_s, new_fox_logf_s, new_win_kv_s)
```

```python
import functools
import math

import jax
import jax.numpy as jnp
from jax import lax
from jax.experimental import pallas as pl
from jax.experimental.pallas import tpu as pltpu

F32 = jnp.float32
BF16 = jnp.bfloat16
I32 = jnp.int32

HEAD_DIM = 128
N_NSA_HEADS = 8
N_NSA_KV = 2
NSA_REP = 4
N_FOX_HEADS = 8
NSA_WIDTH = N_NSA_HEADS * HEAD_DIM
FOX_WIDTH = N_FOX_HEADS * HEAD_DIM
NSA_KV_WIDTH = 2 * N_NSA_KV * HEAD_DIM
CMP_BLOCK = 32
CMP_STRIDE = 16
CMP_SLOTS = CMP_BLOCK // CMP_STRIDE
SLC_BLOCK = 64
N_SELECT = 16
WINDOW = 512
N_BUCKETS = 32
MAX_DISTANCE = 128
PAGE_SIZE = 128
RMS_EPS = 1e-6
NEG_INF = -1e30
FORCE_SCORE = 1e9
ATTN_SCALE = HEAD_DIM ** -0.5

LANES = 128
SUBLANES = 8
VMEM_LIMIT = 56 * 1024 * 1024
GATE_LANES = 3 * N_NSA_HEADS
LOGF_LANE0 = GATE_LANES


def _params(sem):
    return pltpu.CompilerParams(dimension_semantics=sem, vmem_limit_bytes=VMEM_LIMIT)


def _iota(shape, dim):
    return lax.broadcasted_iota(I32, shape, dim)


def _log2(n):
    assert n > 0 and n & (n - 1) == 0
    return n.bit_length() - 1


def _div_pow2(x, n):
    return jnp.right_shift(x, _log2(n))


def _mod_pow2(x, n):
    return jnp.bitwise_and(x, n - 1)


def _dot_nt(a, b):
    return lax.dot_general(a, b, (((1,), (1,)), ((), ())), preferred_element_type=F32)


def _split3(x):
    hi = x.astype(BF16)
    r1 = x - hi.astype(F32)
    mid = r1.astype(BF16)
    lo = (r1 - mid.astype(F32)).astype(BF16)
    return hi, mid, lo


def _dot01_right(x, m01):
    hi, mid, lo = _split3(x)
    d = lambda a: jnp.dot(a, m01, preferred_element_type=F32)
    return d(hi) + d(mid) + d(lo)


def _dot01_left(m01, x):
    hi, mid, lo = _split3(x)
    d = lambda a: jnp.dot(m01, a, preferred_element_type=F32)
    return d(hi) + d(mid) + d(lo)


def _dot01_nt(m01, x):
    hi, mid, lo = _split3(x)
    return _dot_nt(m01, hi) + _dot_nt(m01, mid) + _dot_nt(m01, lo)


def _t5_bucket(dist):
    n = jnp.maximum(dist, 0)
    exact = N_BUCKETS // 2
    nf = jnp.maximum(n, 1).astype(F32)
    far = exact + (jnp.log(nf / exact) / math.log(MAX_DISTANCE / exact) * (N_BUCKETS - exact)).astype(I32)
    return jnp.where(n < exact, n, jnp.minimum(far, N_BUCKETS - 1))


def _t5_lookup(bucket, table_ref, head):
    acc = jnp.zeros(bucket.shape, F32)
    for k in range(N_BUCKETS):
        acc = jnp.where(bucket == k, table_ref[k, head], acc)
    return acc


def _online_update(s, mask, v, m_s, l_s, acc_s):
    if mask is not None:
        s = jnp.where(mask, s, NEG_INF)
    m_old = m_s[...]
    m_new = jnp.maximum(m_old, jnp.max(s, axis=-1, keepdims=True))
    p = jnp.exp(s - m_new)
    if mask is not None:
        p = jnp.where(mask, p, 0.0)
    alpha = jnp.exp(m_old - m_new)
    l_s[...] = alpha * l_s[...] + jnp.sum(p, axis=-1, keepdims=True)
    acc_s[...] = alpha * acc_s[...] + jnp.dot(p.astype(BF16), v, preferred_element_type=F32)
    m_s[...] = m_new


def _init_state(m_s, l_s, acc_s):
    m_s[...] = jnp.full(m_s.shape, NEG_INF, F32)
    l_s[...] = jnp.zeros(l_s.shape, F32)
    acc_s[...] = jnp.zeros(acc_s.shape, F32)


def _t5_bias_body(table_ref, dist_ref, out_ref):
    bucket = _t5_bucket(dist_ref[...])
    for h in range(N_NSA_HEADS):
        out_ref[h] = _t5_lookup(bucket, table_ref, h)


def _t5_bias(table, dist):
    rows, cols = dist.shape
    tr = min(rows, 256)
    assert rows % tr == 0
    return pl.pallas_call(
        _t5_bias_body,
        grid=(rows // tr,),
        in_specs=[pl.BlockSpec(memory_space=pltpu.SMEM),
                  pl.BlockSpec((tr, cols), lambda i: (i, 0))],
        out_specs=pl.BlockSpec((N_NSA_HEADS, tr, cols), lambda i: (0, i, 0)),
        out_shape=jax.ShapeDtypeStruct((N_NSA_HEADS, rows, cols), F32),
        compiler_params=_params(("arbitrary",)),
        name="t5_bias",
    )(table, dist)


_PROJ_WIDTHS = (NSA_WIDTH, NSA_KV_WIDTH, NSA_KV_WIDTH, NSA_KV_WIDTH, FOX_WIDTH, 2 * FOX_WIDTH)
_PROJ_COL_CHUNK = 512


def _proj_body(x_ref, g_ref, wm_ref, ws_ref, bf_ref, q_ref, ckv_ref, skv_ref, wkv_ref, qf_ref, kvf_ref, sm_ref):
    x = x_ref[...]
    xn = x * lax.rsqrt(jnp.mean(x * x, axis=-1, keepdims=True) + RMS_EPS) * g_ref[...]
    xb = xn.astype(BF16)
    off = 0
    for ref in (q_ref, ckv_ref, skv_ref, wkv_ref, qf_ref, kvf_ref):
        width = ref.shape[-1]
        for c in range(0, width, _PROJ_COL_CHUNK):
            ref[:, c:c + _PROJ_COL_CHUNK] = jnp.dot(
                xb, wm_ref[:, off + c:off + c + _PROJ_COL_CHUNK], preferred_element_type=F32)
        off += width
    s = jnp.dot(xb, ws_ref[...], preferred_element_type=F32)
    z = s + bf_ref[...]
    lane = _iota(s.shape, 1)
    logf = jnp.minimum(z, 0.0) - jnp.log1p(jnp.exp(-jnp.abs(z)))
    sig = 1.0 / (1.0 + jnp.exp(-s))
    sm_ref[...] = jnp.where(lane < GATE_LANES, sig, logf)


def _project(x2d, g_attn, wm, ws, bf128):
    n, d = x2d.shape
    tm = min(n, 256)
    assert n % tm == 0
    const = lambda shape: pl.BlockSpec(shape, lambda i: (0, 0), pipeline_mode=pl.Buffered(1))
    widths = _PROJ_WIDTHS + (LANES,)
    return pl.pallas_call(
        _proj_body,
        grid=(n // tm,),
        in_specs=[pl.BlockSpec((tm, d), lambda i: (i, 0)), const((1, d)), const(wm.shape), const(ws.shape),
                  const((1, LANES))],
        out_specs=[pl.BlockSpec((tm, w), lambda i: (i, 0)) for w in widths],
        out_shape=[jax.ShapeDtypeStruct((n, w), F32) for w in widths],
        compiler_params=_params(("arbitrary",)),
        name="in_proj",
    )(x2d, g_attn.reshape(1, d), wm, ws, bf128)


_CHUNK_ROW = CMP_STRIDE * NSA_KV_WIDTH
_CHUNK_K = CMP_STRIDE * HEAD_DIM


def _chunk_cols(c, g):
    return [r * NSA_KV_WIDTH + c * N_NSA_KV * HEAD_DIM + g * HEAD_DIM for r in range(CMP_STRIDE)]


def _chunkproj_compute(get_rows, pe_ref, w_ref, out_ref):
    for c in range(2):
        w = w_ref[c]
        pos = jnp.dot(pe_ref[c].astype(BF16), w, preferred_element_type=F32)
        for g in range(N_NSA_KV):
            y = jnp.dot(get_rows(c, g).astype(BF16), w, preferred_element_type=F32)
            for u in range(CMP_SLOTS):
                col = ((u * 2 + c) * N_NSA_KV + g) * HEAD_DIM
                out_ref[0, :, col:col + HEAD_DIM] = (
                    y[:, u * HEAD_DIM:(u + 1) * HEAD_DIM] + pos[u:u + 1, u * HEAD_DIM:(u + 1) * HEAD_DIM])


def _chunkproj_p_body(x_ref, pe_ref, w_ref, out_ref):
    def get_rows(c, g):
        return jnp.concatenate([x_ref[0, :, o:o + HEAD_DIM] for o in _chunk_cols(c, g)], axis=1)
    _chunkproj_compute(get_rows, pe_ref, w_ref, out_ref)


def _chunkproj_prompt(ckv2d, batch, pe8, wc):
    n_chunks = ckv2d.shape[0] // batch // CMP_STRIDE
    x = ckv2d.reshape(batch, n_chunks, _CHUNK_ROW)
    out_w = CMP_SLOTS * NSA_KV_WIDTH
    return pl.pallas_call(
        _chunkproj_p_body,
        grid=(batch,),
        in_specs=[pl.BlockSpec((1, n_chunks, _CHUNK_ROW), lambda b: (b, 0, 0)),
                  pl.BlockSpec(pe8.shape, lambda b: (0, 0, 0)),
                  pl.BlockSpec(wc.shape, lambda b: (0, 0, 0))],
        out_specs=pl.BlockSpec((1, n_chunks, out_w), lambda b: (b, 0, 0)),
        out_shape=jax.ShapeDtypeStruct((batch, n_chunks, out_w), F32),
        compiler_params=_params(("arbitrary",)),
        name="chunkproj_prompt",
    )(x, pe8, wc)


def _chunkproj_s_body(pt_ref, *refs, n_pg):
    page_refs = refs[:n_pg]
    pe_ref, w_ref, out_ref = refs[n_pg:]

    def get_rows(c, g):
        cols = _chunk_cols(c, g)
        return jnp.concatenate(
            [jnp.concatenate([pr[0, :, o:o + HEAD_DIM] for o in cols], axis=1) for pr in page_refs], axis=0)
    _chunkproj_compute(get_rows, pe_ref, w_ref, out_ref)


def _chunkproj_sample(cache, page_table, pe8, wc):
    n_phys = cache.shape[0]
    batch, n_pages = page_table.shape
    cpp = PAGE_SIZE // CMP_STRIDE
    n_pg = math.gcd(n_pages, 32)
    x = cache.reshape(n_phys, cpp, _CHUNK_ROW)
    out_w = CMP_SLOTS * NSA_KV_WIDTH

    def page_spec(k):
        return pl.BlockSpec((1, cpp, _CHUNK_ROW),
                            lambda b, j, pt: (pt[b * n_pages + j * n_pg + k], 0, 0))
    grid_spec = pltpu.PrefetchScalarGridSpec(
        num_scalar_prefetch=1,
        grid=(batch, n_pages // n_pg),
        in_specs=[page_spec(k) for k in range(n_pg)] + [
            pl.BlockSpec(pe8.shape, lambda b, j, pt: (0, 0, 0)),
            pl.BlockSpec(wc.shape, lambda b, j, pt: (0, 0, 0))],
        out_specs=pl.BlockSpec((1, n_pg * cpp, out_w), lambda b, j, pt: (b, j, 0)),
    )
    return pl.pallas_call(
        functools.partial(_chunkproj_s_body, n_pg=n_pg),
        grid_spec=grid_spec,
        out_shape=jax.ShapeDtypeStruct((batch, n_pages * cpp, out_w), F32),
        compiler_params=_params(("arbitrary", "arbitrary")),
        name="chunkproj_sample",
    )(page_table.reshape(-1), *([x] * n_pg), pe8, wc)


def _cmp_kv(p00, p10, p01, p11):
    n = p00.shape[1]
    k = p00[0] + pltpu.roll(p10[0], n - 1, 0)
    v = p01[0] + pltpu.roll(p11[0], n - 1, 0)
    return k.astype(BF16), v.astype(BF16)


def _p_specs(n_chunks, index):
    def spec(u, c):
        return pl.BlockSpec((1, n_chunks, HEAD_DIM),
                            lambda *a: (index(*a)[0], 0, (u * 2 + c) * N_NSA_KV + index(*a)[1]))
    return [spec(0, 0), spec(1, 0), spec(0, 1), spec(1, 1)]


def _cmp_attn_p_body(q_ref, p00, p10, p01, p11, bias_ref, oc_ref, sel_ref, *, tq, n_blk, n_slc, n_sel):
    i = pl.program_id(2)
    n_chunks = p00.shape[1]
    k, v = _cmp_kv(p00, p10, p01, p11)
    t = i * tq + _iota((tq, n_chunks), 0)
    n = _iota((tq, n_chunks), 1)
    mask = (n * CMP_STRIDE + (CMP_BLOCK - 1) <= t) & (n < n_blk)
    psum = jnp.zeros((tq, n_chunks), F32)
    for r in range(NSA_REP):
        qr = q_ref[0, :, r * HEAD_DIM:(r + 1) * HEAD_DIM].astype(BF16)
        s = _dot_nt(qr, k) * ATTN_SCALE + bias_ref[r]
        s = jnp.where(mask, s, NEG_INF)
        e = jnp.where(mask, jnp.exp(s - jnp.max(s, axis=-1, keepdims=True)), 0.0)
        l = jnp.sum(e, axis=-1, keepdims=True)
        p = e / jnp.where(l > 0.0, l, 1.0)
        oc_ref[0, :, r * HEAD_DIM:(r + 1) * HEAD_DIM] = jnp.dot(p.astype(BF16), v, preferred_element_type=F32)
        psum = psum + p
    sb = _iota((LANES, n_chunks), 0)
    nb = _iota((LANES, n_chunks), 1)
    cover = ((nb * CMP_STRIDE < sb * SLC_BLOCK + SLC_BLOCK) & (nb * CMP_STRIDE + CMP_BLOCK > sb * SLC_BLOCK))
    imp_t = _dot01_nt(cover.astype(BF16), psum)
    blk = _iota((LANES, tq), 0)
    cur = _div_pow2(i * tq + _iota((LANES, tq), 1), SLC_BLOCK)
    forced = (blk == 0) | (blk == cur) | (blk == cur - 1)
    score = jnp.where(forced, FORCE_SCORE, jnp.where(blk <= cur, imp_t, -FORCE_SCORE))
    rank = jnp.zeros((LANES, tq), F32)
    for s2 in range(n_slc):
        row = score[s2:s2 + 1, :]
        ahead = (row > score) | ((row == score) & (s2 < blk))
        rank = rank + ahead.astype(F32)
    sel_t = ((rank < n_sel) & (blk < n_slc)).astype(F32)
    sel_ref[0, 0] = sel_t.T


def _cmp_attn_prompt(q, pchunks, bias_c, batch, seq):
    tq = 256
    n_chunks = pchunks.shape[1]
    n_blk = n_chunks - CMP_SLOTS + 1
    n_slc = -(-seq // SLC_BLOCK)
    assert n_chunks == LANES and n_slc <= LANES and seq % tq == 0
    q3 = q.reshape(batch, seq, NSA_WIDTH)
    gw = NSA_REP * HEAD_DIM
    body = functools.partial(_cmp_attn_p_body, tq=tq, n_blk=n_blk, n_slc=n_slc, n_sel=min(N_SELECT, n_slc))
    return pl.pallas_call(
        body,
        grid=(batch, N_NSA_KV, seq // tq),
        in_specs=[pl.BlockSpec((1, tq, gw), lambda b, g, i: (b, i, g))]
        + _p_specs(n_chunks, lambda b, g, i: (b, g))
        + [pl.BlockSpec((NSA_REP, tq, n_chunks), lambda b, g, i: (g, i, 0))],
        out_specs=[pl.BlockSpec((1, tq, gw), lambda b, g, i: (b, i, g)),
                   pl.BlockSpec((1, 1, tq, LANES), lambda b, g, i: (b, g, i, 0))],
        out_shape=[jax.ShapeDtypeStruct((batch, seq, NSA_WIDTH), F32),
                   jax.ShapeDtypeStruct((batch, N_NSA_KV, seq, LANES), F32)],
        compiler_params=_params(("arbitrary",) * 3),
        name="cmp_attn_prompt",
    )(q3, pchunks, pchunks, pchunks, pchunks, bias_c)


_TK = 128


def _nsa_flash_body(table_ref, q_ref, k_ref, v_ref, tz_ref, *rest, mode, tq, seq):
    if mode == "slc":
        sel_ref, o_ref, m_s, l_s, acc_s, msk_s = rest
    else:
        o_ref, m_s, l_s, acc_s = rest
    g = pl.program_id(1)
    i = pl.program_id(2)
    qb = jnp.concatenate([q_ref[0, :, r * HEAD_DIM:(r + 1) * HEAD_DIM] for r in range(NSA_REP)],
                         axis=0).astype(BF16)
    _init_state(m_s, l_s, acc_s)
    if mode == "slc":
        selb = sel_ref[0, 0].astype(BF16)
        sb = _iota((LANES, _TK), 0)
        kb = _iota((LANES, _TK), 1)
        for jj in range(seq // _TK):
            expand = (sb == _div_pow2(jj * _TK + kb, SLC_BLOCK)).astype(BF16)
            msk_s[jj] = jnp.dot(selb, expand, preferred_element_type=F32)
    ti = _iota((tq, _TK), 0)
    kj = _iota((tq, _TK), 1)

    def tile(j, kind):
        start = pl.multiple_of(j * _TK, _TK)
        kt = k_ref[0, pl.ds(start, _TK), :].astype(BF16)
        vt = v_ref[0, pl.ds(start, _TK), :].astype(BF16)
        s = _dot_nt(qb, kt) * ATTN_SCALE
        if kind == "far":
            bias = jnp.concatenate(
                [jnp.full((tq, _TK), table_ref[N_BUCKETS - 1, g * NSA_REP + r], F32) for r in range(NSA_REP)], axis=0)
        elif kind == "prev":
            bias = jnp.concatenate([tz_ref[r, :, 0:_TK] for r in range(NSA_REP)], axis=0)
        else:
            bias = jnp.concatenate([tz_ref[r, :, _TK:2 * _TK] for r in range(NSA_REP)], axis=0)
        s = s + bias
        dist = (i - j) * _TK + ti - kj
        if mode == "slc":
            m1 = msk_s[j] > 0.5
            if kind == "diag":
                m1 = m1 & (dist >= 0)
        else:
            m1 = (dist >= 0) & (dist <= WINDOW)
        mask = jnp.concatenate([m1] * NSA_REP, axis=0)
        _online_update(s, mask, vt, m_s, l_s, acc_s)

    lo = 0 if mode == "slc" else jnp.maximum(i - WINDOW // _TK, 0)

    def far_body(j, carry):
        tile(j, "far")
        return carry
    lax.fori_loop(lo, jnp.maximum(i - 1, lo), far_body, 0)

    @pl.when(i >= 1)
    def _():
        tile(i - 1, "prev")
    tile(i, "diag")
    o = acc_s[...] / l_s[...]
    for r in range(NSA_REP):
        o_ref[0, :, r * HEAD_DIM:(r + 1) * HEAD_DIM] = o[r * tq:(r + 1) * tq]


def _nsa_flash_prompt(mode, table, q, kv, tz, sel, batch, seq):
    tq = _TK
    gw = NSA_REP * HEAD_DIM
    q3 = q.reshape(batch, seq, NSA_WIDTH)
    kv3 = kv.reshape(batch, seq, NSA_KV_WIDTH)
    rows = NSA_REP * tq
    in_specs = [pl.BlockSpec(memory_space=pltpu.SMEM),
                pl.BlockSpec((1, tq, gw), lambda b, g, i: (b, i, g)),
                pl.BlockSpec((1, seq, HEAD_DIM), lambda b, g, i: (b, 0, g)),
                pl.BlockSpec((1, seq, HEAD_DIM), lambda b, g, i: (b, 0, N_NSA_KV + g)),
                pl.BlockSpec((NSA_REP, _TK, 2 * _TK), lambda b, g, i: (g, 0, 0))]
    args = [table, q3, kv3, kv3, tz]
    scratch = [pltpu.VMEM((rows, 1), F32), pltpu.VMEM((rows, 1), F32), pltpu.VMEM((rows, HEAD_DIM), F32)]
    if mode == "slc":
        in_specs.append(pl.BlockSpec((1, 1, tq, LANES), lambda b, g, i: (b, g, i, 0)))
        args.append(sel)
        scratch.append(pltpu.VMEM((seq // _TK, tq, _TK), F32))
    return pl.pallas_call(
        functools.partial(_nsa_flash_body, mode=mode, tq=tq, seq=seq),
        grid=(batch, N_NSA_KV, seq // tq),
        in_specs=in_specs,
        out_specs=pl.BlockSpec((1, tq, gw), lambda b, g, i: (b, i, g)),
        out_shape=jax.ShapeDtypeStruct((batch, seq, NSA_WIDTH), F32),
        scratch_shapes=scratch,
        compiler_params=_params(("arbitrary",) * 3),
        name="nsa_flash_" + mode,
    )(*args)


def _cumsum_body(sm_ref, c_ref, ct_ref):
    seq = sm_ref.shape[1]
    lane = _iota((LANES, LANES), 1)
    tri = (_iota((LANES, LANES), 0) >= lane).astype(BF16)
    keep = (lane >= LOGF_LANE0) & (lane < LOGF_LANE0 + N_FOX_HEADS)
    carry = jnp.zeros((1, LANES), F32)
    for blk in range(seq // LANES):
        x = jnp.where(keep, sm_ref[0, blk * LANES:(blk + 1) * LANES, :], 0.0)
        cb = _dot01_left(tri, x) + carry
        c_ref[0, blk * LANES:(blk + 1) * LANES, :] = cb
        ct_ref[0, :, blk * LANES:(blk + 1) * LANES] = cb.T
        carry = cb[LANES - 1:LANES, :]


def _cumsum_prompt(sm, batch, seq):
    sm3 = sm.reshape(batch, seq, LANES)
    return pl.pallas_call(
        _cumsum_body,
        grid=(batch,),
        in_specs=[pl.BlockSpec((1, seq, LANES), lambda b: (b, 0, 0))],
        out_specs=[pl.BlockSpec((1, seq, LANES), lambda b: (b, 0, 0)),
                   pl.BlockSpec((1, LANES, seq), lambda b: (b, 0, 0))],
        out_shape=[jax.ShapeDtypeStruct((batch, seq, LANES), F32),
                   jax.ShapeDtypeStruct((batch, LANES, seq), F32)],
        compiler_params=_params(("arbitrary",)),
        name="logf_cumsum",
    )(sm3)


def _fox_flash_body(q_ref, k_ref, v_ref, ccol_ref, crow_ref, o_ref, m_s, l_s, acc_s, *, tq):
    h = pl.program_id(1)
    i = pl.program_id(2)
    qb = q_ref[0].astype(BF16)
    _init_state(m_s, l_s, acc_s)
    lane = _iota((tq, LANES), 1)
    c_q = jnp.sum(jnp.where(lane == LOGF_LANE0 + h, ccol_ref[0], 0.0), axis=-1, keepdims=True)
    ti = i * tq + _iota((tq, _TK), 0)
    kj = _iota((tq, _TK), 1)

    def body(j, carry):
        start = pl.multiple_of(j * _TK, _TK)
        kt = k_ref[0, pl.ds(start, _TK), :].astype(BF16)
        vt = v_ref[0, pl.ds(start, _TK), :].astype(BF16)
        s = _dot_nt(qb, kt) * ATTN_SCALE + c_q - crow_ref[0, 0, pl.ds(j, 1), :]
        mask = (j * _TK + kj) <= ti
        _online_update(s, mask, vt, m_s, l_s, acc_s)
        return carry
    lax.fori_loop(0, (i + 1) * (tq // _TK), body, 0)
    o_ref[0] = acc_s[...] / l_s[...]


def _fox_flash_prompt(qf, kvf, c, c_rows, batch, seq):
    tq = 256
    q3 = qf.reshape(batch, seq, FOX_WIDTH)
    kv3 = kvf.reshape(batch, seq, 2 * FOX_WIDTH)
    return pl.pallas_call(
        functools.partial(_fox_flash_body, tq=tq),
        grid=(batch, N_FOX_HEADS, seq // tq),
        in_specs=[pl.BlockSpec((1, tq, HEAD_DIM), lambda b, h, i: (b, i, h)),
                  pl.BlockSpec((1, seq, HEAD_DIM), lambda b, h, i: (b, 0, h)),
                  pl.BlockSpec((1, seq, HEAD_DIM), lambda b, h, i: (b, 0, N_FOX_HEADS + h)),
                  pl.BlockSpec((1, tq, LANES), lambda b, h, i: (b, i, 0)),
                  pl.BlockSpec((1, 1, seq // _TK, _TK), lambda b, h, i: (b, h, 0, 0))],
        out_specs=pl.BlockSpec((1, tq, HEAD_DIM), lambda b, h, i: (b, i, h)),
        out_shape=jax.ShapeDtypeStruct((batch, seq, FOX_WIDTH), F32),
        scratch_shapes=[pltpu.VMEM((tq, 1), F32), pltpu.VMEM((tq, 1), F32), pltpu.VMEM((tq, HEAD_DIM), F32)],
        compiler_params=_params(("arbitrary",) * 3),
        name="fox_flash_prompt",
    )(q3, kv3, kv3, c, c_rows)


def _rms(x, g):
    return x * lax.rsqrt(jnp.mean(x * x, axis=-1, keepdims=True) + RMS_EPS) * g


def _outproj_body(x_ref, oc_ref, os_ref, ow_ref, of_ref, sm_ref, gn_ref, gf_ref, wo_ref, h_ref):
    gates = sm_ref[...]
    parts = []
    for hh in range(N_NSA_HEADS):
        sl = slice(hh * HEAD_DIM, (hh + 1) * HEAD_DIM)
        parts.append(gates[:, hh:hh + 1] * oc_ref[:, sl]
                     + gates[:, N_NSA_HEADS + hh:N_NSA_HEADS + hh + 1] * os_ref[:, sl]
                     + gates[:, 2 * N_NSA_HEADS + hh:2 * N_NSA_HEADS + hh + 1] * ow_ref[:, sl])
    o_nsa = jnp.concatenate(parts, axis=1)
    mixed = jnp.concatenate([_rms(o_nsa, gn_ref[...]), _rms(of_ref[...], gf_ref[...])], axis=1).astype(BF16)
    h_ref[...] = x_ref[...] + jnp.dot(mixed, wo_ref[...], preferred_element_type=F32)


def _outproj(x2d, o_c, o_s, o_w, o_f, sm, g_nsa, g_fox, wo):
    n, d = x2d.shape
    tm = min(n, 256)
    row = lambda w: pl.BlockSpec((tm, w), lambda i: (i, 0))
    const = lambda shape: pl.BlockSpec(shape, lambda i: (0, 0), pipeline_mode=pl.Buffered(1))
    return pl.pallas_call(
        _outproj_body,
        grid=(n // tm,),
        in_specs=[row(d), row(NSA_WIDTH), row(NSA_WIDTH), row(NSA_WIDTH), row(FOX_WIDTH), row(LANES),
                  const((1, NSA_WIDTH)), const((1, FOX_WIDTH)), const(wo.shape)],
        out_specs=row(d),
        out_shape=jax.ShapeDtypeStruct((n, d), F32),
        compiler_params=_params(("arbitrary",)),
        name="out_proj",
    )(x2d, o_c, o_s, o_w, o_f, sm, g_nsa.reshape(1, -1), g_fox.reshape(1, -1), wo)


def _mlp_body(h_ref, gm_ref, gfin_ref, wu_ref, wd_ref, y_ref, xn_s, acc_s):
    j = pl.program_id(1)

    @pl.when(j == 0)
    def _():
        xn_s[...] = _rms(h_ref[...], gm_ref[...]).astype(BF16)
        acc_s[...] = jnp.zeros(acc_s.shape, F32)
    u = jnp.maximum(jnp.dot(xn_s[...], wu_ref[...], preferred_element_type=F32), 0.0)
    acc_s[...] += jnp.dot((u * u).astype(BF16), wd_ref[...], preferred_element_type=F32)

    @pl.when(j == pl.num_programs(1) - 1)
    def _():
        y_ref[...] = _rms(h_ref[...] + acc_s[...], gfin_ref[...])


def _mlp(h, g_mlp, g_final, wu, wd):
    n, d = h.shape
    dff = wu.shape[1]
    tm = min(n, 512)
    tf = 512
    return pl.pallas_call(
        _mlp_body,
        grid=(n // tm, dff // tf),
        in_specs=[pl.BlockSpec((tm, d), lambda i, j: (i, 0)),
                  pl.BlockSpec((1, d), lambda i, j: (0, 0)),
                  pl.BlockSpec((1, d), lambda i, j: (0, 0)),
                  pl.BlockSpec((d, tf), lambda i, j: (0, j)),
                  pl.BlockSpec((tf, d), lambda i, j: (j, 0))],
        out_specs=pl.BlockSpec((tm, d), lambda i, j: (i, 0)),
        out_shape=jax.ShapeDtypeStruct((n, d), F32),
        scratch_shapes=[pltpu.VMEM((tm, d), BF16), pltpu.VMEM((tm, d), F32)],
        compiler_params=_params(("arbitrary", "arbitrary")),
        name="mlp_final",
    )(h, g_mlp.reshape(1, d), g_final.reshape(1, d), wu, wd)


def _cmp_attn_s_body(q_ref, p00, p10, p01, p11, bias_ref, oc_ref, sel_ref, *, past, n_blk, n_slc, n_sel):
    n_chunks = p00.shape[1]
    t_new = q_ref.shape[1]
    sl = sel_ref.shape[-1]
    k, v = _cmp_kv(p00, p10, p01, p11)
    qb = jnp.concatenate([q_ref[0, :, r * HEAD_DIM:(r + 1) * HEAD_DIM] for r in range(NSA_REP)],
                         axis=0).astype(BF16)
    bias = jnp.concatenate([bias_ref[r] for r in range(NSA_REP)], axis=0)
    rows = NSA_REP * t_new
    t = past + _mod_pow2(_iota((rows, n_chunks), 0), t_new)
    n = _iota((rows, n_chunks), 1)
    mask = (n * CMP_STRIDE + (CMP_BLOCK - 1) <= t) & (n < n_blk)
    s = jnp.where(mask, _dot_nt(qb, k) * ATTN_SCALE + bias, NEG_INF)
    e = jnp.where(mask, jnp.exp(s - jnp.max(s, axis=-1, keepdims=True)), 0.0)
    l = jnp.sum(e, axis=-1, keepdims=True)
    p = e / jnp.where(l > 0.0, l, 1.0)
    o = jnp.dot(p.astype(BF16), v, preferred_element_type=F32)
    psum = jnp.zeros((t_new, n_chunks), F32)
    for r in range(NSA_REP):
        oc_ref[0, :, r * HEAD_DIM:(r + 1) * HEAD_DIM] = o[r * t_new:(r + 1) * t_new]
        psum = psum + p[r * t_new:(r + 1) * t_new]
    nb = _iota((n_chunks, sl), 0)
    sb = _iota((n_chunks, sl), 1)
    cover = ((nb * CMP_STRIDE < sb * SLC_BLOCK + SLC_BLOCK) & (nb * CMP_STRIDE + CMP_BLOCK > sb * SLC_BLOCK))
    imp = _dot01_right(psum, cover.astype(BF16))
    blk = _iota((t_new, sl), 1)
    cur = _div_pow2(past + _iota((t_new, sl), 0), SLC_BLOCK)
    forced = (blk == 0) | (blk == cur) | (blk == cur - 1)
    score = jnp.where(forced, FORCE_SCORE, jnp.where(blk <= cur, imp, -FORCE_SCORE))
    score = jnp.where(blk < n_slc, score, -jnp.inf)
    sel = jnp.zeros((t_new, sl), F32)
    for _ in range(n_sel):
        mx = jnp.max(score, axis=-1, keepdims=True)
        first = jnp.min(jnp.where(score == mx, blk, sl), axis=-1, keepdims=True)
        hit = blk == first
        sel = jnp.where(hit, 1.0, sel)
        score = jnp.where(hit, -jnp.inf, score)
    sel_ref[0, 0] = sel


def _cmp_attn_sample(q, pchunks, bias_c, batch, t_new, past):
    n_chunks = pchunks.shape[1]
    n_blk = n_chunks - CMP_SLOTS + 1
    n_slc = -(-(past + t_new) // SLC_BLOCK)
    sl = -(-n_slc // LANES) * LANES
    gw = NSA_REP * HEAD_DIM
    q3 = q.reshape(batch, t_new, NSA_WIDTH)
    body = functools.partial(_cmp_attn_s_body, past=past, n_blk=n_blk, n_slc=n_slc, n_sel=min(N_SELECT, n_slc))
    return pl.pallas_call(
        body,
        grid=(batch, N_NSA_KV),
        in_specs=[pl.BlockSpec((1, t_new, gw), lambda b, g: (b, 0, g))]
        + _p_specs(n_chunks, lambda b, g: (b, g))
        + [pl.BlockSpec((NSA_REP, t_new, n_chunks), lambda b, g: (g, 0, 0))],
        out_specs=[pl.BlockSpec((1, t_new, gw), lambda b, g: (b, 0, g)),
                   pl.BlockSpec((1, 1, t_new, sl), lambda b, g: (b, g, 0, 0))],
        out_shape=[jax.ShapeDtypeStruct((batch, t_new, NSA_WIDTH), F32),
                   jax.ShapeDtypeStruct((batch, N_NSA_KV, t_new, sl), F32)],
        compiler_params=_params(("arbitrary",) * 2),
        name="cmp_attn_sample",
    )(q3, pchunks, pchunks, pchunks, pchunks, bias_c)


_GKV = N_NSA_KV * HEAD_DIM


def _dec_tile(qbd, kv, dist, mask, table_ref, full_bias, t_new, m_s, l_s, acc_s):
    s = _dot_nt(qbd, kv[:, :_GKV].astype(BF16)) * ATTN_SCALE
    parts = []
    for hh in range(N_NSA_HEADS):
        rows = slice(hh * t_new, (hh + 1) * t_new)
        if full_bias:
            b = _t5_lookup(_t5_bucket(dist[rows]), table_ref, hh)
        else:
            b = jnp.full((t_new, s.shape[1]), table_ref[N_BUCKETS - 1, hh], F32)
        parts.append(s[rows] + b)
    s = jnp.concatenate(parts, axis=0)
    _online_update(s, mask, kv[:, _GKV:].astype(BF16), m_s, l_s, acc_s)


def _dec_finish(o_ref, t_new, l_s, acc_s):
    o = acc_s[...] / l_s[...]
    for g in range(N_NSA_KV):
        for r in range(NSA_REP):
            hh = g * NSA_REP + r
            o_ref[0, :, hh * HEAD_DIM:(hh + 1) * HEAD_DIM] = o[hh * t_new:(hh + 1) * t_new,
                                                             g * HEAD_DIM:(g + 1) * HEAD_DIM]


def _row_mask(m2, t_new):
    return jnp.concatenate([m2[g * t_new:(g + 1) * t_new] for g in range(N_NSA_KV) for _ in range(NSA_REP)], axis=0)


def _slc_s_body(pt_ref, table_ref, q_ref, *rest, n_pg, past, t_new):
    page_refs = rest[:n_pg]
    selp_ref, kvnew_ref, selnew_ref, o_ref, m_s, l_s, acc_s = rest[n_pg:]
    j = pl.program_id(1)
    rows = N_NSA_HEADS * t_new
    qbd = q_ref[0]
    t_pos = past + _mod_pow2(_iota((rows, PAGE_SIZE), 0), t_new)
    lane = _iota((rows, PAGE_SIZE), 1)

    @pl.when(j == 0)
    def _():
        _init_state(m_s, l_s, acc_s)

    def do_page(k, full_bias):
        base = (j * n_pg + k) * PAGE_SIZE
        dist = t_pos - (base + lane)
        m2 = selp_ref[0, k]
        pick = jnp.where(_iota((N_NSA_KV * t_new, PAGE_SIZE), 1) < SLC_BLOCK, m2[:, 0:1], m2[:, 1:2])
        mask = _row_mask(pick, t_new) > 0.5
        _dec_tile(qbd, page_refs[k][0], dist, mask, table_ref, full_bias, t_new, m_s, l_s, acc_s)

    for k in range(n_pg - 1):
        do_page(k, False)
    last = pl.num_programs(1) - 1

    @pl.when(j < last)
    def _():
        do_page(n_pg - 1, False)

    @pl.when(j == last)
    def _():
        do_page(n_pg - 1, True)
        dist = t_pos - (past + lane)
        mnew = _row_mask(jnp.broadcast_to(selnew_ref[0][:, 0:1], (N_NSA_KV * t_new, PAGE_SIZE)), t_new) > 0.5
        mask = mnew & (dist >= 0) & (lane < t_new)
        _dec_tile(qbd, kvnew_ref[0], dist, mask, table_ref, True, t_new, m_s, l_s, acc_s)
        _dec_finish(o_ref, t_new, l_s, acc_s)


def _slc_sample(table, page_table, qbd, cache, selp, kvnew, selnew, past, t_new):
    batch, n_pages = page_table.shape
    n_pg = math.gcd(n_pages, 8)
    n_phys = cache.shape[0]
    assert t_new <= PAGE_SIZE and past % PAGE_SIZE == 0 and PAGE_SIZE >= MAX_DISTANCE
    x = cache.reshape(n_phys, PAGE_SIZE, 2 * _GKV)
    rows = N_NSA_HEADS * t_new

    def page_spec(k):
        return pl.BlockSpec((1, PAGE_SIZE, 2 * _GKV), lambda b, j, pt: (pt[b * n_pages + j * n_pg + k], 0, 0))
    grid_spec = pltpu.PrefetchScalarGridSpec(
        num_scalar_prefetch=1,
        grid=(batch, n_pages // n_pg),
        in_specs=[pl.BlockSpec(memory_space=pltpu.SMEM),
                  pl.BlockSpec((1, rows, _GKV), lambda b, j, pt: (b, 0, 0))]
        + [page_spec(k) for k in range(n_pg)]
        + [pl.BlockSpec((1, n_pg, N_NSA_KV * t_new, 2), lambda b, j, pt: (b, j, 0, 0)),
           pl.BlockSpec((1, PAGE_SIZE, 2 * _GKV), lambda b, j, pt: (b, 0, 0)),
           pl.BlockSpec((1, N_NSA_KV * t_new, 2), lambda b, j, pt: (b, 0, 0))],
        out_specs=pl.BlockSpec((1, t_new, NSA_WIDTH), lambda b, j, pt: (b, 0, 0)),
        scratch_shapes=[pltpu.VMEM((rows, 1), F32), pltpu.VMEM((rows, 1), F32), pltpu.VMEM((rows, _GKV), F32)],
    )
    return pl.pallas_call(
        functools.partial(_slc_s_body, n_pg=n_pg, past=past, t_new=t_new),
        grid_spec=grid_spec,
        out_shape=jax.ShapeDtypeStruct((batch, t_new, NSA_WIDTH), F32),
        compiler_params=_params(("arbitrary", "arbitrary")),
        name="slc_sample",
    )(page_table.reshape(-1), table, qbd, *([x] * n_pg), selp, kvnew, selnew)


def _win_s_body(table_ref, q_ref, kv_ref, o_ref, m_s, l_s, acc_s, *, past, t_new, wb):
    rows = N_NSA_HEADS * t_new
    qbd = q_ref[0]
    t_pos = past + _mod_pow2(_iota((rows, PAGE_SIZE), 0), t_new)
    lane = _iota((rows, PAGE_SIZE), 1)
    _init_state(m_s, l_s, acc_s)
    for k in range(kv_ref.shape[1] // PAGE_SIZE):
        k_pos = past - wb + k * PAGE_SIZE + lane
        dist = t_pos - k_pos
        mask = (dist >= 0) & (dist <= WINDOW) & (k_pos >= 0) & (k * PAGE_SIZE + lane < wb + t_new)
        _dec_tile(qbd, kv_ref[0, k * PAGE_SIZE:(k + 1) * PAGE_SIZE, :], dist, mask, table_ref, True, t_new,
                  m_s, l_s, acc_s)
    _dec_finish(o_ref, t_new, l_s, acc_s)


def _win_sample(table, qbd, win_all_padded, past, t_new, wb):
    batch, n_keys, _ = win_all_padded.shape
    rows = N_NSA_HEADS * t_new
    return pl.pallas_call(
        functools.partial(_win_s_body, past=past, t_new=t_new, wb=wb),
        grid=(batch,),
        in_specs=[pl.BlockSpec(memory_space=pltpu.SMEM),
                  pl.BlockSpec((1, rows, _GKV), lambda b: (b, 0, 0)),
                  pl.BlockSpec((1, n_keys, 2 * _GKV), lambda b: (b, 0, 0))],
        out_specs=pl.BlockSpec((1, t_new, NSA_WIDTH), lambda b: (b, 0, 0)),
        out_shape=jax.ShapeDtypeStruct((batch, t_new, NSA_WIDTH), F32),
        scratch_shapes=[pltpu.VMEM((rows, 1), F32), pltpu.VMEM((rows, 1), F32), pltpu.VMEM((rows, _GKV), F32)],
        compiler_params=_params(("arbitrary",)),
        name="win_sample",
    )(table, qbd, win_all_padded)


def _fox_suffix_body(pt_ref, *refs, n_pg):
    page_refs = refs[:n_pg]
    z_ref = refs[n_pg]
    x = jnp.concatenate([pr[0] for pr in page_refs], axis=0)
    sp = _iota((PAGE_SIZE, 2 * PAGE_SIZE), 0)
    s = _iota((PAGE_SIZE, 2 * PAGE_SIZE), 1)
    tri = (((s < PAGE_SIZE) & (sp > s)) | ((s >= PAGE_SIZE) & (sp >= s - PAGE_SIZE))).astype(BF16)
    z = _dot01_right(x, tri)
    for k in range(n_pg):
        z_ref[0, k] = z[k * N_FOX_HEADS:(k + 1) * N_FOX_HEADS]


def _fox_suffix(logf_t, page_table):
    batch, n_pages = page_table.shape
    n_pg = math.gcd(n_pages, 16)

    def page_spec(k):
        return pl.BlockSpec((1, N_FOX_HEADS, PAGE_SIZE), lambda b, j, pt: (pt[b * n_pages + j * n_pg + k], 0, 0))
    grid_spec = pltpu.PrefetchScalarGridSpec(
        num_scalar_prefetch=1,
        grid=(batch, n_pages // n_pg),
        in_specs=[page_spec(k) for k in range(n_pg)],
        out_specs=pl.BlockSpec((1, n_pg, N_FOX_HEADS, 2 * PAGE_SIZE), lambda b, j, pt: (b, j, 0, 0)),
    )
    return pl.pallas_call(
        functools.partial(_fox_suffix_body, n_pg=n_pg),
        grid_spec=grid_spec,
        out_shape=jax.ShapeDtypeStruct((batch, n_pages, N_FOX_HEADS, 2 * PAGE_SIZE), F32),
        compiler_params=_params(("arbitrary", "arbitrary")),
        name="fox_suffix",
    )(page_table.reshape(-1), *([logf_t] * n_pg))


def _fox_s_body(pt_ref, q_ref, *rest, n_pg, t_new):
    page_refs = rest[:n_pg]
    z_ref, kvnew_ref, lfn_ref, o_ref, m_s, l_s, acc_s, off_s = rest[n_pg:]
    j = pl.program_id(1)
    rows = N_FOX_HEADS * t_new
    qbd = q_ref[0]
    lane = _iota((rows, PAGE_SIZE), 1)
    t_row = _mod_pow2(_iota((rows, PAGE_SIZE), 0), t_new)
    lfn = lfn_ref[0]
    c_new = jnp.sum(jnp.where(lane <= t_row, lfn, 0.0), axis=-1, keepdims=True)

    @pl.when(j == 0)
    def _():
        _init_state(m_s, l_s, acc_s)
        off_s[...] = jnp.zeros(off_s.shape, F32)
        bias = jnp.zeros((rows, PAGE_SIZE), F32)
        for u in range(t_new):
            col = jnp.sum(jnp.where((lane > u) & (lane <= t_row), lfn, 0.0), axis=-1, keepdims=True)
            bias = jnp.where(lane == u, col, bias)
        kv = kvnew_ref[0]
        s = _dot_nt(qbd, kv[:, :FOX_WIDTH].astype(BF16)) * ATTN_SCALE + bias
        mask = (lane <= t_row) & (lane < t_new)
        _online_update(s, mask, kv[:, FOX_WIDTH:].astype(BF16), m_s, l_s, acc_s)

    for k in range(n_pg):
        kv = page_refs[k][0]
        zk = z_ref[0, n_pg - 1 - k]
        d_exc = zk[:, :PAGE_SIZE] + off_s[:, 0:1]
        bias = jnp.concatenate(
            [jnp.broadcast_to(d_exc[h:h + 1, :], (t_new, PAGE_SIZE)) for h in range(N_FOX_HEADS)], axis=0)
        s = _dot_nt(qbd, kv[:, :FOX_WIDTH].astype(BF16)) * ATTN_SCALE + (bias + c_new)
        _online_update(s, None, kv[:, FOX_WIDTH:].astype(BF16), m_s, l_s, acc_s)
        off_s[...] = off_s[...] + zk[:, PAGE_SIZE:PAGE_SIZE + 1]

    @pl.when(j == pl.num_programs(1) - 1)
    def _():
        o = acc_s[...] / l_s[...]
        for h in range(N_FOX_HEADS):
            o_ref[0, :, h * HEAD_DIM:(h + 1) * HEAD_DIM] = o[h * t_new:(h + 1) * t_new,
                                                           h * HEAD_DIM:(h + 1) * HEAD_DIM]


def _fox_sample(page_table, qbd, cache, z, kvnew, lfn, t_new):
    batch, n_pages = page_table.shape
    n_pg = math.gcd(n_pages, 8)
    n_groups = n_pages // n_pg
    n_phys = cache.shape[0]
    x = cache.reshape(n_phys, PAGE_SIZE, 2 * FOX_WIDTH)
    rows = N_FOX_HEADS * t_new

    def page_spec(k):
        return pl.BlockSpec((1, PAGE_SIZE, 2 * FOX_WIDTH),
                            lambda b, j, pt: (pt[b * n_pages + n_pages - 1 - (j * n_pg + k)], 0, 0))
    grid_spec = pltpu.PrefetchScalarGridSpec(
        num_scalar_prefetch=1,
        grid=(batch, n_groups),
        in_specs=[pl.BlockSpec((1, rows, FOX_WIDTH), lambda b, j, pt: (b, 0, 0))]
        + [page_spec(k) for k in range(n_pg)]
        + [pl.BlockSpec((1, n_pg, N_FOX_HEADS, 2 * PAGE_SIZE), lambda b, j, pt: (b, n_groups - 1 - j, 0, 0)),
           pl.BlockSpec((1, PAGE_SIZE, 2 * FOX_WIDTH), lambda b, j, pt: (b, 0, 0)),
           pl.BlockSpec((1, rows, PAGE_SIZE), lambda b, j, pt: (b, 0, 0))],
        out_specs=pl.BlockSpec((1, t_new, FOX_WIDTH), lambda b, j, pt: (b, 0, 0)),
        scratch_shapes=[pltpu.VMEM((rows, 1), F32), pltpu.VMEM((rows, 1), F32), pltpu.VMEM((rows, FOX_WIDTH), F32),
                        pltpu.VMEM((N_FOX_HEADS, LANES), F32)],
    )
    return pl.pallas_call(
        functools.partial(_fox_s_body, n_pg=n_pg, t_new=t_new),
        grid_spec=grid_spec,
        out_shape=jax.ShapeDtypeStruct((batch, t_new, FOX_WIDTH), F32),
        compiler_params=_params(("arbitrary", "arbitrary")),
        name="fox_sample",
    )(page_table.reshape(-1), qbd, *([x] * n_pg), z, kvnew, lfn)


def _pack_weights(w_in, b_f, w_cmp, pe_cmp, w_o, w_up, w_down):
    c_gt = NSA_WIDTH + 3 * NSA_KV_WIDTH
    c_qf = c_gt + GATE_LANES
    c_fl = c_qf + 3 * FOX_WIDTH
    wm = jnp.concatenate([w_in[:, :c_gt], w_in[:, c_qf:c_fl]], axis=1).astype(BF16)
    ws = jnp.concatenate([w_in[:, c_gt:c_qf], w_in[:, c_fl:],
                          jnp.zeros((w_in.shape[0], LANES - GATE_LANES - N_FOX_HEADS), F32)], axis=1).astype(BF16)
    bf128 = jnp.zeros((1, LANES), F32).at[0, LOGF_LANE0:LOGF_LANE0 + N_FOX_HEADS].set(b_f)
    wc = w_cmp.reshape(2, CMP_SLOTS, CMP_STRIDE, HEAD_DIM, HEAD_DIM).transpose(0, 2, 3, 1, 4).reshape(
        2, _CHUNK_K, CMP_SLOTS * HEAD_DIM).astype(BF16)
    pe8 = jnp.zeros((2, SUBLANES, _CHUNK_K), F32).at[:, :CMP_SLOTS].set(pe_cmp.reshape(2, CMP_SLOTS, _CHUNK_K))
    return wm, ws, bf128, wc, pe8, w_o.astype(BF16), w_up.astype(BF16), w_down.astype(BF16)


def _block_diag_q(q, t_new, n_grp, per_grp):
    batch = q.shape[0]
    q5 = q.reshape(batch, t_new, n_grp, per_grp, HEAD_DIM).transpose(0, 2, 3, 1, 4)
    eye = jnp.eye(n_grp, dtype=q.dtype)
    out = q5[:, :, :, :, None, :] * eye[None, :, None, None, :, None]
    return out.reshape(batch, n_grp * per_grp * t_new, n_grp * HEAD_DIM).astype(BF16)


def _prompt_pass(x, table, lw, packed):
    g_attn, g_nsa_out, g_fox_out, g_mlp, g_final = lw
    wm, ws, bf128, wc, pe8, wo, wu, wd = packed
    batch, seq, d = x.shape
    assert seq % 256 == 0 and seq >= WINDOW
    x2d = x.reshape(batch * seq, d)
    q, ckv, skv, wkv, qf, kvf, sm = _project(x2d, g_attn, wm, ws, bf128)
    pchunks = _chunkproj_prompt(ckv, batch, pe8, wc)
    n_chunks = seq // CMP_STRIDE
    dist_c = jnp.arange(seq, dtype=I32)[:, None] - (jnp.arange(n_chunks, dtype=I32)[None, :] * CMP_STRIDE
                                                     + (CMP_BLOCK - 1))
    bias_c = _t5_bias(table, dist_c)
    o_c, sel = _cmp_attn_prompt(q, pchunks, bias_c, batch, seq)
    dist_tz = jnp.arange(_TK, dtype=I32)[:, None] - jnp.arange(2 * _TK, dtype=I32)[None, :] + _TK
    tz = _t5_bias(table, dist_tz)
    o_s = _nsa_flash_prompt("slc", table, q, skv, tz, sel, batch, seq)
    o_w = _nsa_flash_prompt("win", table, q, wkv, tz, None, batch, seq)
    c, c_t = _cumsum_prompt(sm, batch, seq)
    c_rows = c_t[:, LOGF_LANE0:LOGF_LANE0 + N_FOX_HEADS, :].reshape(batch, N_FOX_HEADS, seq // _TK, _TK)
    o_f = _fox_flash_prompt(qf, kvf, c, c_rows, batch, seq)
    h = _outproj(x2d, o_c.reshape(-1, NSA_WIDTH), o_s.reshape(-1, NSA_WIDTH), o_w.reshape(-1, NSA_WIDTH),
                 o_f.reshape(-1, FOX_WIDTH), sm, g_nsa_out, g_fox_out, wo)
    y = _mlp(h, g_mlp, g_final, wu, wd).reshape(batch, seq, d)
    kv_shape = (1, batch, seq, 2, N_NSA_KV, HEAD_DIM)
    wkv5 = wkv.reshape(kv_shape)
    logf = sm.reshape(batch, seq, LANES)[:, :, LOGF_LANE0:LOGF_LANE0 + N_FOX_HEADS]
    return (y, ckv.reshape(kv_shape), skv.reshape(kv_shape),
            kvf.reshape(1, batch, seq, 2, N_FOX_HEADS, HEAD_DIM), logf[None],
            wkv5[:, :, seq - min(WINDOW, seq):])


def _sample_pass(x, cache_cmp, cache_slc, cache_fox, cache_logf, win_buf, page_table, table, lw, packed):
    g_attn, g_nsa_out, g_fox_out, g_mlp, g_final = lw
    wm, ws, bf128, wc, pe8, wo, wu, wd = packed
    batch, t_new, d = x.shape
    n_pages = page_table.shape[1]
    past = n_pages * PAGE_SIZE
    assert t_new < CMP_STRIDE and t_new % SUBLANES == 0
    x2d = x.reshape(batch * t_new, d)
    q, ckv, skv, wkv, qf, kvf, sm = _project(x2d, g_attn, wm, ws, bf128)
    pchunks = _chunkproj_sample(cache_cmp, page_table, pe8, wc)
    n_chunks = pchunks.shape[1]
    pos = past + jnp.arange(t_new, dtype=I32)
    dist_c = pos[:, None] - (jnp.arange(n_chunks, dtype=I32)[None, :] * CMP_STRIDE + (CMP_BLOCK - 1))
    bias_c = _t5_bias(table, dist_c)
    o_c, sel = _cmp_attn_sample(q, pchunks, bias_c, batch, t_new, past)
    qbd = _block_diag_q(q.reshape(batch, t_new, NSA_WIDTH), t_new, N_NSA_KV, NSA_REP)
    n_past_blk = past // SLC_BLOCK
    bpp = PAGE_SIZE // SLC_BLOCK
    selp = sel[..., :n_past_blk].reshape(batch, N_NSA_KV, t_new, n_pages, bpp).transpose(0, 3, 1, 2, 4).reshape(
        batch, n_pages, N_NSA_KV * t_new, bpp)
    selnew = jnp.broadcast_to(sel[..., n_past_blk:n_past_blk + 1].reshape(batch, N_NSA_KV * t_new, 1),
                              (batch, N_NSA_KV * t_new, bpp))
    pad_rows = lambda a: jnp.pad(a.reshape(batch, t_new, -1), ((0, 0), (0, PAGE_SIZE - t_new), (0, 0)))
    o_s = _slc_sample(table, page_table, qbd, cache_slc, selp, pad_rows(skv), selnew, past, t_new)
    wb = win_buf.shape[1]
    win_all = jnp.concatenate([win_buf.reshape(batch, wb, NSA_KV_WIDTH), wkv.reshape(batch, t_new, NSA_KV_WIDTH)],
                              axis=1)
    n_keys = -(-(wb + t_new) // PAGE_SIZE) * PAGE_SIZE
    o_w = _win_sample(table, qbd, jnp.pad(win_all, ((0, 0), (0, n_keys - wb - t_new), (0, 0))), past, t_new, wb)
    logf_new = sm.reshape(batch, t_new, LANES)[:, :, LOGF_LANE0:LOGF_LANE0 + N_FOX_HEADS]
    logf_t = cache_logf.transpose(0, 2, 1)
    z = _fox_suffix(logf_t, page_table)
    qbd_f = _block_diag_q(qf.reshape(batch, t_new, FOX_WIDTH), t_new, N_FOX_HEADS, 1)
    lfn = jnp.broadcast_to(logf_new.transpose(0, 2, 1)[:, :, None, :], (batch, N_FOX_HEADS, t_new, t_new))
    lfn = jnp.pad(lfn.reshape(batch, N_FOX_HEADS * t_new, t_new), ((0, 0), (0, 0), (0, PAGE_SIZE - t_new)))
    o_f = _fox_sample(page_table, qbd_f, cache_fox, z, pad_rows(kvf), lfn, t_new)
    h = _outproj(x2d, o_c.reshape(-1, NSA_WIDTH), o_s.reshape(-1, NSA_WIDTH), o_w.reshape(-1, NSA_WIDTH),
                 o_f.reshape(-1, FOX_WIDTH), sm, g_nsa_out, g_fox_out, wo)
    y = _mlp(h, g_mlp, g_final, wu, wd).reshape(batch, t_new, d)
    kv_shape = (1, batch, t_new, 2, N_NSA_KV, HEAD_DIM)
    return (y, ckv.reshape(kv_shape), skv.reshape(kv_shape),
            kvf.reshape(1, batch, t_new, 2, N_FOX_HEADS, HEAD_DIM), logf_new[None],
            win_all[:, t_new:].reshape(1, batch, wb, 2, N_NSA_KV, HEAD_DIM))


def kernel(x_prompt, x_sample, cache_cmp_kv, cache_slc_kv, cache_fox_kv, cache_fox_logf, state_win_kv, page_table,
           t5_table, g_attn, w_in, b_f, w_cmp, pe_cmp, g_nsa_out, g_fox_out, w_o, g_mlp, w_up, w_down, g_final):
    assert g_attn.shape[0] == 1, "single-layer step"
    assert x_prompt.shape[-1] == NSA_WIDTH + FOX_WIDTH
    packed = _pack_weights(w_in[0], b_f[0], w_cmp[0], pe_cmp[0], w_o[0], w_up[0], w_down[0])
    lw = (g_attn[0], g_nsa_out[0], g_fox_out[0], g_mlp[0], g_final)
    yp, cmp_p, slc_p, fox_p, logf_p, win_p = _prompt_pass(x_prompt, t5_table, lw, packed)
    ys, cmp_s, slc_s, fox_s, logf_s, win_s = _sample_pass(
        x_sample, cache_cmp_kv[0], cache_slc_kv[0], cache_fox_kv[0], cache_fox_logf[0], state_win_kv[0],
        page_table, t5_table, lw, packed)
    return (yp, ys, cmp_p, slc_p, fox_p, logf_p, win_p, cmp_s, slc_s, fox_s, logf_s, win_s)
```

```python
import functools
import math

import jax
import jax.numpy as jnp
from jax import lax
from jax.experimental import pallas as pl
from jax.experimental.pallas import tpu as pltpu

F32 = jnp.float32
BF16 = jnp.bfloat16
I32 = jnp.int32

HEAD_DIM = 128
N_NSA_HEADS = 8
N_NSA_KV = 2
NSA_REP = 4
N_FOX_HEADS = 8
NSA_WIDTH = N_NSA_HEADS * HEAD_DIM
FOX_WIDTH = N_FOX_HEADS * HEAD_DIM
NSA_KV_WIDTH = 2 * N_NSA_KV * HEAD_DIM
CMP_BLOCK = 32
CMP_STRIDE = 16
CMP_SLOTS = CMP_BLOCK // CMP_STRIDE
SLC_BLOCK = 64
N_SELECT = 16
WINDOW = 512
N_BUCKETS = 32
MAX_DISTANCE = 128
PAGE_SIZE = 128
RMS_EPS = 1e-6
NEG_INF = -1e30
FORCE_SCORE = 1e9
ATTN_SCALE = HEAD_DIM ** -0.5

LANES = 128
SUBLANES = 8
VMEM_LIMIT = 56 * 1024 * 1024
GATE_LANES = 3 * N_NSA_HEADS
LOGF_LANE0 = GATE_LANES


def _params(sem):
    return pltpu.CompilerParams(dimension_semantics=sem, vmem_limit_bytes=VMEM_LIMIT)


def _iota(shape, dim):
    return lax.broadcasted_iota(I32, shape, dim)


def _log2(n):
    assert n > 0 and n & (n - 1) == 0
    return n.bit_length() - 1


def _div_pow2(x, n):
    return jnp.right_shift(x, _log2(n))


def _mod_pow2(x, n):
    return jnp.bitwise_and(x, n - 1)


def _dot_nt(a, b):
    return lax.dot_general(a, b, (((1,), (1,)), ((), ())), preferred_element_type=F32)


def _split3(x):
    hi = x.astype(BF16)
    r1 = x - hi.astype(F32)
    mid = r1.astype(BF16)
    lo = (r1 - mid.astype(F32)).astype(BF16)
    return hi, mid, lo


def _dot01_right(x, m01):
    hi, mid, lo = _split3(x)
    d = lambda a: jnp.dot(a, m01, preferred_element_type=F32)
    return d(hi) + d(mid) + d(lo)


def _dot01_left(m01, x):
    hi, mid, lo = _split3(x)
    d = lambda a: jnp.dot(m01, a, preferred_element_type=F32)
    return d(hi) + d(mid) + d(lo)


def _dot01_nt(m01, x):
    hi, mid, lo = _split3(x)
    return _dot_nt(m01, hi) + _dot_nt(m01, mid) + _dot_nt(m01, lo)


def _t5_bucket(dist):
    n = jnp.maximum(dist, 0)
    exact = N_BUCKETS // 2
    nf = jnp.maximum(n, 1).astype(F32)
    far = exact + (jnp.log(nf / exact) / math.log(MAX_DISTANCE / exact) * (N_BUCKETS - exact)).astype(I32)
    return jnp.where(n < exact, n, jnp.minimum(far, N_BUCKETS - 1))


def _t5_lookup(bucket, table_ref, head):
    acc = jnp.zeros(bucket.shape, F32)
    for k in range(N_BUCKETS):
        acc = jnp.where(bucket == k, table_ref[k, head], acc)
    return acc


def _online_update(s, mask, v, m_s, l_s, acc_s):
    if mask is not None:
        s = jnp.where(mask, s, NEG_INF)
    m_old = m_s[...]
    m_new = jnp.maximum(m_old, jnp.max(s, axis=-1, keepdims=True))
    p = jnp.exp(s - m_new)
    if mask is not None:
        p = jnp.where(mask, p, 0.0)
    alpha = jnp.exp(m_old - m_new)
    l_s[...] = alpha * l_s[...] + jnp.sum(p, axis=-1, keepdims=True)
    acc_s[...] = alpha * acc_s[...] + jnp.dot(p.astype(BF16), v, preferred_element_type=F32)
    m_s[...] = m_new


def _init_state(m_s, l_s, acc_s):
    m_s[...] = jnp.full(m_s.shape, NEG_INF, F32)
    l_s[...] = jnp.zeros(l_s.shape, F32)
    acc_s[...] = jnp.zeros(acc_s.shape, F32)


def _t5_bias_body(table_ref, dist_ref, out_ref):
    bucket = _t5_bucket(dist_ref[...])
    for h in range(N_NSA_HEADS):
        out_ref[h] = _t5_lookup(bucket, table_ref, h)


def _t5_bias(table, dist):
    rows, cols = dist.shape
    tr = min(rows, 256)
    assert rows % tr == 0
    return pl.pallas_call(
        _t5_bias_body,
        grid=(rows // tr,),
        in_specs=[pl.BlockSpec(memory_space=pltpu.SMEM),
                  pl.BlockSpec((tr, cols), lambda i: (i, 0))],
        out_specs=pl.BlockSpec((N_NSA_HEADS, tr, cols), lambda i: (0, i, 0)),
        out_shape=jax.ShapeDtypeStruct((N_NSA_HEADS, rows, cols), F32),
        compiler_params=_params(("arbitrary",)),
        name="t5_bias",
    )(table, dist)


_PROJ_WIDTHS = (NSA_WIDTH, NSA_KV_WIDTH, NSA_KV_WIDTH, NSA_KV_WIDTH, FOX_WIDTH, 2 * FOX_WIDTH)
_PROJ_COL_CHUNK = 512


def _proj_body(x_ref, g_ref, wm_ref, ws_ref, bf_ref, q_ref, ckv_ref, skv_ref, wkv_ref, qf_ref, kvf_ref, sm_ref):
    x = x_ref[...]
    xn = x * lax.rsqrt(jnp.mean(x * x, axis=-1, keepdims=True) + RMS_EPS) * g_ref[...]
    xb = xn.astype(BF16)
    off = 0
    for ref in (q_ref, ckv_ref, skv_ref, wkv_ref, qf_ref, kvf_ref):
        width = ref.shape[-1]
        for c in range(0, width, _PROJ_COL_CHUNK):
            ref[:, c:c + _PROJ_COL_CHUNK] = jnp.dot(
                xb, wm_ref[:, off + c:off + c + _PROJ_COL_CHUNK], preferred_element_type=F32)
        off += width
    s = jnp.dot(xb, ws_ref[...], preferred_element_type=F32)
    z = s + bf_ref[...]
    lane = _iota(s.shape, 1)
    logf = jnp.minimum(z, 0.0) - jnp.log1p(jnp.exp(-jnp.abs(z)))
    sig = 1.0 / (1.0 + jnp.exp(-s))
    sm_ref[...] = jnp.where(lane < GATE_LANES, sig, logf)


def _project(x2d, g_attn, wm, ws, bf128):
    n, d = x2d.shape
    tm = min(n, 256)
    assert n % tm == 0
    const = lambda shape: pl.BlockSpec(shape, lambda i: (0, 0), pipeline_mode=pl.Buffered(1))
    widths = _PROJ_WIDTHS + (LANES,)
    return pl.pallas_call(
        _proj_body,
        grid=(n // tm,),
        in_specs=[pl.BlockSpec((tm, d), lambda i: (i, 0)), const((1, d)), const(wm.shape), const(ws.shape),
                  const((1, LANES))],
        out_specs=[pl.BlockSpec((tm, w), lambda i: (i, 0)) for w in widths],
        out_shape=[jax.ShapeDtypeStruct((n, w), F32) for w in widths],
        compiler_params=_params(("arbitrary",)),
        name="in_proj",
    )(x2d, g_attn.reshape(1, d), wm, ws, bf128)


_CHUNK_ROW = CMP_STRIDE * NSA_KV_WIDTH
_CHUNK_K = CMP_STRIDE * HEAD_DIM


def _chunk_cols(c, g):
    return [r * NSA_KV_WIDTH + c * N_NSA_KV * HEAD_DIM + g * HEAD_DIM for r in range(CMP_STRIDE)]


def _chunkproj_compute(get_rows, pe_ref, w_ref, out_ref):
    for c in range(2):
        w = w_ref[c]
        pos = jnp.dot(pe_ref[c].astype(BF16), w, preferred_element_type=F32)
        for g in range(N_NSA_KV):
            y = jnp.dot(get_rows(c, g).astype(BF16), w, preferred_element_type=F32)
            for u in range(CMP_SLOTS):
                col = ((u * 2 + c) * N_NSA_KV + g) * HEAD_DIM
                out_ref[0, :, col:col + HEAD_DIM] = (
                    y[:, u * HEAD_DIM:(u + 1) * HEAD_DIM] + pos[u:u + 1, u * HEAD_DIM:(u + 1) * HEAD_DIM])


def _chunkproj_p_body(x_ref, pe_ref, w_ref, out_ref):
    def get_rows(c, g):
        return jnp.concatenate([x_ref[0, :, o:o + HEAD_DIM] for o in _chunk_cols(c, g)], axis=1)
    _chunkproj_compute(get_rows, pe_ref, w_ref, out_ref)


def _chunkproj_prompt(ckv2d, batch, pe8, wc):
    n_chunks = ckv2d.shape[0] // batch // CMP_STRIDE
    x = ckv2d.reshape(batch, n_chunks, _CHUNK_ROW)
    out_w = CMP_SLOTS * NSA_KV_WIDTH
    return pl.pallas_call(
        _chunkproj_p_body,
        grid=(batch,),
        in_specs=[pl.BlockSpec((1, n_chunks, _CHUNK_ROW), lambda b: (b, 0, 0)),
                  pl.BlockSpec(pe8.shape, lambda b: (0, 0, 0)),
                  pl.BlockSpec(wc.shape, lambda b: (0, 0, 0))],
        out_specs=pl.BlockSpec((1, n_chunks, out_w), lambda b: (b, 0, 0)),
        out_shape=jax.ShapeDtypeStruct((batch, n_chunks, out_w), F32),
        compiler_params=_params(("arbitrary",)),
        name="chunkproj_prompt",
    )(x, pe8, wc)


def _chunkproj_s_body(pt_ref, *refs, n_pg):
    page_refs = refs[:n_pg]
    pe_ref, w_ref, out_ref = refs[n_pg:]

    def get_rows(c, g):
        cols = _chunk_cols(c, g)
        return jnp.concatenate(
            [jnp.concatenate([pr[0, :, o:o + HEAD_DIM] for o in cols], axis=1) for pr in page_refs], axis=0)
    _chunkproj_compute(get_rows, pe_ref, w_ref, out_ref)


def _chunkproj_sample(cache, page_table, pe8, wc):
    n_phys = cache.shape[0]
    batch, n_pages = page_table.shape
    cpp = PAGE_SIZE // CMP_STRIDE
    n_pg = math.gcd(n_pages, 32)
    x = cache.reshape(n_phys, cpp, _CHUNK_ROW)
    out_w = CMP_SLOTS * NSA_KV_WIDTH

    def page_spec(k):
        return pl.BlockSpec((1, cpp, _CHUNK_ROW),
                            lambda b, j, pt: (pt[b * n_pages + j * n_pg + k], 0, 0))
    grid_spec = pltpu.PrefetchScalarGridSpec(
        num_scalar_prefetch=1,
        grid=(batch, n_pages // n_pg),
        in_specs=[page_spec(k) for k in range(n_pg)] + [
            pl.BlockSpec(pe8.shape, lambda b, j, pt: (0, 0, 0)),
            pl.BlockSpec(wc.shape, lambda b, j, pt: (0, 0, 0))],
        out_specs=pl.BlockSpec((1, n_pg * cpp, out_w), lambda b, j, pt: (b, j, 0)),
    )
    return pl.pallas_call(
        functools.partial(_chunkproj_s_body, n_pg=n_pg),
        grid_spec=grid_spec,
        out_shape=jax.ShapeDtypeStruct((batch, n_pages * cpp, out_w), F32),
        compiler_params=_params(("arbitrary", "arbitrary")),
        name="chunkproj_sample",
    )(page_table.reshape(-1), *([x] * n_pg), pe8, wc)


def _cmp_kv(p00, p10, p01, p11):
    n = p00.shape[1]
    k = p00[0] + pltpu.roll(p10[0], n - 1, 0)
    v = p01[0] + pltpu.roll(p11[0], n - 1, 0)
    return k.astype(BF16), v.astype(BF16)


def _p_specs(n_chunks, index):
    def spec(u, c):
        return pl.BlockSpec((1, n_chunks, HEAD_DIM),
                            lambda *a: (index(*a)[0], 0, (u * 2 + c) * N_NSA_KV + index(*a)[1]))
    return [spec(0, 0), spec(1, 0), spec(0, 1), spec(1, 1)]


def _cmp_attn_p_body(q_ref, p00, p10, p01, p11, bias_ref, oc_ref, sel_ref, *, tq, n_blk, n_slc, n_sel):
    i = pl.program_id(2)
    n_chunks = p00.shape[1]
    k, v = _cmp_kv(p00, p10, p01, p11)
    t = i * tq + _iota((tq, n_chunks), 0)
    n = _iota((tq, n_chunks), 1)
    mask = (n * CMP_STRIDE + (CMP_BLOCK - 1) <= t) & (n < n_blk)
    psum = jnp.zeros((tq, n_chunks), F32)
    for r in range(NSA_REP):
        qr = q_ref[0, :, r * HEAD_DIM:(r + 1) * HEAD_DIM].astype(BF16)
        s = _dot_nt(qr, k) * ATTN_SCALE + bias_ref[r]
        s = jnp.where(mask, s, NEG_INF)
        e = jnp.where(mask, jnp.exp(s - jnp.max(s, axis=-1, keepdims=True)), 0.0)
        l = jnp.sum(e, axis=-1, keepdims=True)
        p = e / jnp.where(l > 0.0, l, 1.0)
        oc_ref[0, :, r * HEAD_DIM:(r + 1) * HEAD_DIM] = jnp.dot(p.astype(BF16), v, preferred_element_type=F32)
        psum = psum + p
    sb = _iota((LANES, n_chunks), 0)
    nb = _iota((LANES, n_chunks), 1)
    cover = ((nb * CMP_STRIDE < sb * SLC_BLOCK + SLC_BLOCK) & (nb * CMP_STRIDE + CMP_BLOCK > sb * SLC_BLOCK))
    imp_t = _dot01_nt(cover.astype(BF16), psum)
    blk = _iota((LANES, tq), 0)
    cur = _div_pow2(i * tq + _iota((LANES, tq), 1), SLC_BLOCK)
    forced = (blk == 0) | (blk == cur) | (blk == cur - 1)
    score = jnp.where(forced, FORCE_SCORE, jnp.where(blk <= cur, imp_t, -FORCE_SCORE))
    rank = jnp.zeros((LANES, tq), F32)
    for s2 in range(n_slc):
        row = score[s2:s2 + 1, :]
        ahead = (row > score) | ((row == score) & (s2 < blk))
        rank = rank + ahead.astype(F32)
    sel_t = ((rank < n_sel) & (blk < n_slc)).astype(F32)
    sel_ref[0, 0] = sel_t.T


def _cmp_attn_prompt(q, pchunks, bias_c, batch, seq):
    tq = 256
    n_chunks = pchunks.shape[1]
    n_blk = n_chunks - CMP_SLOTS + 1
    n_slc = -(-seq // SLC_BLOCK)
    assert n_chunks == LANES and n_slc <= LANES and seq % tq == 0
    q3 = q.reshape(batch, seq, NSA_WIDTH)
    gw = NSA_REP * HEAD_DIM
    body = functools.partial(_cmp_attn_p_body, tq=tq, n_blk=n_blk, n_slc=n_slc, n_sel=min(N_SELECT, n_slc))
    return pl.pallas_call(
        body,
        grid=(batch, N_NSA_KV, seq // tq),
        in_specs=[pl.BlockSpec((1, tq, gw), lambda b, g, i: (b, i, g))]
        + _p_specs(n_chunks, lambda b, g, i: (b, g))
        + [pl.BlockSpec((NSA_REP, tq, n_chunks), lambda b, g, i: (g, i, 0))],
        out_specs=[pl.BlockSpec((1, tq, gw), lambda b, g, i: (b, i, g)),
                   pl.BlockSpec((1, 1, tq, LANES), lambda b, g, i: (b, g, i, 0))],
        out_shape=[jax.ShapeDtypeStruct((batch, seq, NSA_WIDTH), F32),
                   jax.ShapeDtypeStruct((batch, N_NSA_KV, seq, LANES), F32)],
        compiler_params=_params(("arbitrary",) * 3),
        name="cmp_attn_prompt",
    )(q3, pchunks, pchunks, pchunks, pchunks, bias_c)


_TK = 256


def _flash_step(xs, consts, vt, m_s, l_s, acc_s, tq):
    ps, alphas = [], []
    for r, x in enumerate(xs):
        rows = slice(r * tq, (r + 1) * tq)
        m_old = m_s[rows]
        m_new = jnp.maximum(m_old, jnp.max(x, axis=-1, keepdims=True) + consts[r])
        p = jnp.exp(x - (m_new - consts[r]))
        alpha = jnp.exp(m_old - m_new)
        l_s[rows] = alpha * l_s[rows] + jnp.sum(p, axis=-1, keepdims=True)
        m_s[rows] = m_new
        ps.append(p.astype(BF16))
        alphas.append(alpha)
    pv = jnp.dot(jnp.concatenate(ps, axis=0), vt, preferred_element_type=F32)
    acc_s[...] = jnp.concatenate(alphas, axis=0) * acc_s[...] + pv


def _nsa_flash_body(table_ref, q_ref, k_ref, v_ref, tz_ref, *rest, mode, tq, seq):
    if mode == "slc":
        sel_ref, o_ref, m_s, l_s, acc_s, msk_s = rest
    else:
        o_ref, m_s, l_s, acc_s = rest
    g = pl.program_id(1)
    i = pl.program_id(2)
    qb = jnp.concatenate([q_ref[0, :, r * HEAD_DIM:(r + 1) * HEAD_DIM] * ATTN_SCALE for r in range(NSA_REP)],
                         axis=0).astype(BF16)
    _init_state(m_s, l_s, acc_s)
    if mode == "slc":
        selb = sel_ref[0, 0].astype(BF16)
        sb = _iota((LANES, _TK), 0)
        kb = _iota((LANES, _TK), 1)
        for jj in range(seq // _TK):
            expand = (sb == _div_pow2(jj * _TK + kb, SLC_BLOCK)).astype(BF16)
            hit = jnp.dot(selb, expand, preferred_element_type=F32)
            msk_s[jj] = (hit - 1.0) * (-NEG_INF)
    ti = _iota((tq, _TK), 0)
    kj = _iota((tq, _TK), 1)
    causal = jnp.where(ti >= kj, 0.0, NEG_INF)
    far_consts = [table_ref[N_BUCKETS - 1, g * NSA_REP + r] for r in range(NSA_REP)]

    def block(j, kind):
        start = pl.multiple_of(j * _TK, _TK)
        kt = k_ref[0, pl.ds(start, _TK), :].astype(BF16)
        vt = v_ref[0, pl.ds(start, _TK), :].astype(BF16)
        x = _dot_nt(qb, kt)
        add = None
        if mode == "slc":
            add = msk_s[j]
        elif kind == "far":
            add = jnp.where((i - j) * _TK + ti - kj <= WINDOW, 0.0, NEG_INF)
        if kind == "diag":
            add = causal if add is None else add + causal
        xs = []
        for r in range(NSA_REP):
            xr = x[r * tq:(r + 1) * tq]
            if kind == "prev":
                xr = xr + tz_ref[r, :, 0:_TK]
            elif kind == "diag":
                xr = xr + tz_ref[r, :, _TK:2 * _TK]
            if add is not None:
                xr = xr + add
            xs.append(xr)
        _flash_step(xs, far_consts if kind == "far" else [0.0] * NSA_REP, vt, m_s, l_s, acc_s, tq)

    lo = 0 if mode == "slc" else jnp.maximum(i - WINDOW // _TK, 0)

    def far_body(j, carry):
        block(j, "far")
        return carry
    lax.fori_loop(lo, jnp.maximum(i - 1, lo), far_body, 0)

    @pl.when(i >= 1)
    def _():
        block(i - 1, "prev")
    block(i, "diag")
    o = acc_s[...] / l_s[...]
    for r in range(NSA_REP):
        o_ref[0, :, r * HEAD_DIM:(r + 1) * HEAD_DIM] = o[r * tq:(r + 1) * tq]


def _nsa_flash_prompt(mode, table, q, kv, tz, sel, batch, seq):
    tq = _TK
    assert _TK >= MAX_DISTANCE and WINDOW % _TK == 0 and seq % _TK == 0
    gw = NSA_REP * HEAD_DIM
    q3 = q.reshape(batch, seq, NSA_WIDTH)
    kv3 = kv.reshape(batch, seq, NSA_KV_WIDTH)
    rows = NSA_REP * tq
    in_specs = [pl.BlockSpec(memory_space=pltpu.SMEM),
                pl.BlockSpec((1, tq, gw), lambda b, g, i: (b, i, g)),
                pl.BlockSpec((1, seq, HEAD_DIM), lambda b, g, i: (b, 0, g)),
                pl.BlockSpec((1, seq, HEAD_DIM), lambda b, g, i: (b, 0, N_NSA_KV + g)),
                pl.BlockSpec((NSA_REP, _TK, 2 * _TK), lambda b, g, i: (g, 0, 0))]
    args = [table, q3, kv3, kv3, tz]
    scratch = [pltpu.VMEM((rows, 1), F32), pltpu.VMEM((rows, 1), F32), pltpu.VMEM((rows, HEAD_DIM), F32)]
    if mode == "slc":
        in_specs.append(pl.BlockSpec((1, 1, tq, LANES), lambda b, g, i: (b, g, i, 0)))
        args.append(sel)
        scratch.append(pltpu.VMEM((seq // _TK, tq, _TK), F32))
    return pl.pallas_call(
        functools.partial(_nsa_flash_body, mode=mode, tq=tq, seq=seq),
        grid=(batch, N_NSA_KV, seq // tq),
        in_specs=in_specs,
        out_specs=pl.BlockSpec((1, tq, gw), lambda b, g, i: (b, i, g)),
        out_shape=jax.ShapeDtypeStruct((batch, seq, NSA_WIDTH), F32),
        scratch_shapes=scratch,
        compiler_params=_params(("arbitrary",) * 3),
        name="nsa_flash_" + mode,
    )(*args)


def _cumsum_body(sm_ref, c_ref, ct_ref):
    seq = sm_ref.shape[1]
    lane = _iota((LANES, LANES), 1)
    tri = (_iota((LANES, LANES), 0) >= lane).astype(BF16)
    keep = (lane >= LOGF_LANE0) & (lane < LOGF_LANE0 + N_FOX_HEADS)
    carry = jnp.zeros((1, LANES), F32)
    for blk in range(seq // LANES):
        x = jnp.where(keep, sm_ref[0, blk * LANES:(blk + 1) * LANES, :], 0.0)
        cb = _dot01_left(tri, x) + carry
        c_ref[0, blk * LANES:(blk + 1) * LANES, :] = cb
        ct_ref[0, :, blk * LANES:(blk + 1) * LANES] = cb.T
        carry = cb[LANES - 1:LANES, :]


def _cumsum_prompt(sm, batch, seq):
    sm3 = sm.reshape(batch, seq, LANES)
    return pl.pallas_call(
        _cumsum_body,
        grid=(batch,),
        in_specs=[pl.BlockSpec((1, seq, LANES), lambda b: (b, 0, 0))],
        out_specs=[pl.BlockSpec((1, seq, LANES), lambda b: (b, 0, 0)),
                   pl.BlockSpec((1, LANES, seq), lambda b: (b, 0, 0))],
        out_shape=[jax.ShapeDtypeStruct((batch, seq, LANES), F32),
                   jax.ShapeDtypeStruct((batch, LANES, seq), F32)],
        compiler_params=_params(("arbitrary",)),
        name="logf_cumsum",
    )(sm3)


_FOX_T = 512


def _fox_flash_body(q_ref, k_ref, v_ref, ccol_ref, crow_ref, o_ref, m_s, l_s, acc_s, *, tq):
    h = pl.program_id(1)
    i = pl.program_id(2)
    qb = (q_ref[0] * ATTN_SCALE).astype(BF16)
    _init_state(m_s, l_s, acc_s)
    lane = _iota((tq, LANES), 1)
    c_q = jnp.sum(jnp.where(lane == LOGF_LANE0 + h, ccol_ref[0], 0.0), axis=-1, keepdims=True)
    causal = jnp.where(_iota((tq, tq), 0) >= _iota((tq, tq), 1), 0.0, NEG_INF)

    def block(j, diag):
        start = pl.multiple_of(j * tq, tq)
        kt = k_ref[0, pl.ds(start, tq), :].astype(BF16)
        vt = v_ref[0, pl.ds(start, tq), :].astype(BF16)
        x = _dot_nt(qb, kt) - crow_ref[0, 0, pl.ds(j, 1), :]
        if diag:
            x = x + causal
        m_old = m_s[...]
        m_new = jnp.maximum(m_old, jnp.max(x, axis=-1, keepdims=True) + c_q)
        p = jnp.exp(x - (m_new - c_q))
        alpha = jnp.exp(m_old - m_new)
        l_s[...] = alpha * l_s[...] + jnp.sum(p, axis=-1, keepdims=True)
        acc_s[...] = alpha * acc_s[...] + jnp.dot(p.astype(BF16), vt, preferred_element_type=F32)
        m_s[...] = m_new

    def far_body(j, carry):
        block(j, False)
        return carry
    lax.fori_loop(0, i, far_body, 0)
    block(i, True)
    o_ref[0] = acc_s[...] / l_s[...]


def _fox_flash_prompt(qf, kvf, c, c_rows, batch, seq):
    tq = _FOX_T
    assert seq % tq == 0
    q3 = qf.reshape(batch, seq, FOX_WIDTH)
    kv3 = kvf.reshape(batch, seq, 2 * FOX_WIDTH)
    return pl.pallas_call(
        functools.partial(_fox_flash_body, tq=tq),
        grid=(batch, N_FOX_HEADS, seq // tq),
        in_specs=[pl.BlockSpec((1, tq, HEAD_DIM), lambda b, h, i: (b, i, h)),
                  pl.BlockSpec((1, seq, HEAD_DIM), lambda b, h, i: (b, 0, h)),
                  pl.BlockSpec((1, seq, HEAD_DIM), lambda b, h, i: (b, 0, N_FOX_HEADS + h)),
                  pl.BlockSpec((1, tq, LANES), lambda b, h, i: (b, i, 0)),
                  pl.BlockSpec((1, 1, seq // tq, tq), lambda b, h, i: (b, h, 0, 0))],
        out_specs=pl.BlockSpec((1, tq, HEAD_DIM), lambda b, h, i: (b, i, h)),
        out_shape=jax.ShapeDtypeStruct((batch, seq, FOX_WIDTH), F32),
        scratch_shapes=[pltpu.VMEM((tq, 1), F32), pltpu.VMEM((tq, 1), F32), pltpu.VMEM((tq, HEAD_DIM), F32)],
        compiler_params=_params(("arbitrary",) * 3),
        name="fox_flash_prompt",
    )(q3, kv3, kv3, c, c_rows)


def _rms(x, g):
    return x * lax.rsqrt(jnp.mean(x * x, axis=-1, keepdims=True) + RMS_EPS) * g


def _outproj_body(x_ref, oc_ref, os_ref, ow_ref, of_ref, sm_ref, gn_ref, gf_ref, wo_ref, h_ref):
    gates = sm_ref[...]
    parts = []
    for hh in range(N_NSA_HEADS):
        sl = slice(hh * HEAD_DIM, (hh + 1) * HEAD_DIM)
        parts.append(gates[:, hh:hh + 1] * oc_ref[:, sl]
                     + gates[:, N_NSA_HEADS + hh:N_NSA_HEADS + hh + 1] * os_ref[:, sl]
                     + gates[:, 2 * N_NSA_HEADS + hh:2 * N_NSA_HEADS + hh + 1] * ow_ref[:, sl])
    o_nsa = jnp.concatenate(parts, axis=1)
    mixed = jnp.concatenate([_rms(o_nsa, gn_ref[...]), _rms(of_ref[...], gf_ref[...])], axis=1).astype(BF16)
    h_ref[...] = x_ref[...] + jnp.dot(mixed, wo_ref[...], preferred_element_type=F32)


def _outproj(x2d, o_c, o_s, o_w, o_f, sm, g_nsa, g_fox, wo):
    n, d = x2d.shape
    tm = min(n, 256)
    row = lambda w: pl.BlockSpec((tm, w), lambda i: (i, 0))
    const = lambda shape: pl.BlockSpec(shape, lambda i: (0, 0), pipeline_mode=pl.Buffered(1))
    return pl.pallas_call(
        _outproj_body,
        grid=(n // tm,),
        in_specs=[row(d), row(NSA_WIDTH), row(NSA_WIDTH), row(NSA_WIDTH), row(FOX_WIDTH), row(LANES),
                  const((1, NSA_WIDTH)), const((1, FOX_WIDTH)), const(wo.shape)],
        out_specs=row(d),
        out_shape=jax.ShapeDtypeStruct((n, d), F32),
        compiler_params=_params(("arbitrary",)),
        name="out_proj",
    )(x2d, o_c, o_s, o_w, o_f, sm, g_nsa.reshape(1, -1), g_fox.reshape(1, -1), wo)


def _mlp_body(h_ref, gm_ref, gfin_ref, wu_ref, wd_ref, y_ref, xn_s, acc_s):
    j = pl.program_id(1)

    @pl.when(j == 0)
    def _():
        xn_s[...] = _rms(h_ref[...], gm_ref[...]).astype(BF16)
        acc_s[...] = jnp.zeros(acc_s.shape, F32)
    u = jnp.maximum(jnp.dot(xn_s[...], wu_ref[...], preferred_element_type=F32), 0.0)
    acc_s[...] += jnp.dot((u * u).astype(BF16), wd_ref[...], preferred_element_type=F32)

    @pl.when(j == pl.num_programs(1) - 1)
    def _():
        y_ref[...] = _rms(h_ref[...] + acc_s[...], gfin_ref[...])


def _mlp(h, g_mlp, g_final, wu, wd):
    n, d = h.shape
    dff = wu.shape[1]
    tm = min(n, 512)
    tf = 512
    return pl.pallas_call(
        _mlp_body,
        grid=(n // tm, dff // tf),
        in_specs=[pl.BlockSpec((tm, d), lambda i, j: (i, 0)),
                  pl.BlockSpec((1, d), lambda i, j: (0, 0)),
                  pl.BlockSpec((1, d), lambda i, j: (0, 0)),
                  pl.BlockSpec((d, tf), lambda i, j: (0, j)),
                  pl.BlockSpec((tf, d), lambda i, j: (j, 0))],
        out_specs=pl.BlockSpec((tm, d), lambda i, j: (i, 0)),
        out_shape=jax.ShapeDtypeStruct((n, d), F32),
        scratch_shapes=[pltpu.VMEM((tm, d), BF16), pltpu.VMEM((tm, d), F32)],
        compiler_params=_params(("arbitrary", "arbitrary")),
        name="mlp_final",
    )(h, g_mlp.reshape(1, d), g_final.reshape(1, d), wu, wd)


def _cmp_attn_s_body(q_ref, p00, p10, p01, p11, bias_ref, oc_ref, sel_ref, *, past, n_blk, n_slc, n_sel):
    n_chunks = p00.shape[1]
    t_new = q_ref.shape[1]
    sl = sel_ref.shape[-1]
    k, v = _cmp_kv(p00, p10, p01, p11)
    qb = jnp.concatenate([q_ref[0, :, r * HEAD_DIM:(r + 1) * HEAD_DIM] for r in range(NSA_REP)],
                         axis=0).astype(BF16)
    bias = jnp.concatenate([bias_ref[r] for r in range(NSA_REP)], axis=0)
    rows = NSA_REP * t_new
    t = past + _mod_pow2(_iota((rows, n_chunks), 0), t_new)
    n = _iota((rows, n_chunks), 1)
    mask = (n * CMP_STRIDE + (CMP_BLOCK - 1) <= t) & (n < n_blk)
    s = jnp.where(mask, _dot_nt(qb, k) * ATTN_SCALE + bias, NEG_INF)
    e = jnp.where(mask, jnp.exp(s - jnp.max(s, axis=-1, keepdims=True)), 0.0)
    l = jnp.sum(e, axis=-1, keepdims=True)
    p = e / jnp.where(l > 0.0, l, 1.0)
    o = jnp.dot(p.astype(BF16), v, preferred_element_type=F32)
    psum = jnp.zeros((t_new, n_chunks), F32)
    for r in range(NSA_REP):
        oc_ref[0, :, r * HEAD_DIM:(r + 1) * HEAD_DIM] = o[r * t_new:(r + 1) * t_new]
        psum = psum + p[r * t_new:(r + 1) * t_new]
    nb = _iota((n_chunks, sl), 0)
    sb = _iota((n_chunks, sl), 1)
    cover = ((nb * CMP_STRIDE < sb * SLC_BLOCK + SLC_BLOCK) & (nb * CMP_STRIDE + CMP_BLOCK > sb * SLC_BLOCK))
    imp = _dot01_right(psum, cover.astype(BF16))
    blk = _iota((t_new, sl), 1)
    cur = _div_pow2(past + _iota((t_new, sl), 0), SLC_BLOCK)
    forced = (blk == 0) | (blk == cur) | (blk == cur - 1)
    score = jnp.where(forced, FORCE_SCORE, jnp.where(blk <= cur, imp, -FORCE_SCORE))
    score = jnp.where(blk < n_slc, score, -jnp.inf)
    sel = jnp.zeros((t_new, sl), F32)
    for _ in range(n_sel):
        mx = jnp.max(score, axis=-1, keepdims=True)
        first = jnp.min(jnp.where(score == mx, blk, sl), axis=-1, keepdims=True)
        hit = blk == first
        sel = jnp.where(hit, 1.0, sel)
        score = jnp.where(hit, -jnp.inf, score)
    sel_ref[0, 0] = sel


def _cmp_attn_sample(q, pchunks, bias_c, batch, t_new, past):
    n_chunks = pchunks.shape[1]
    n_blk = n_chunks - CMP_SLOTS + 1
    n_slc = -(-(past + t_new) // SLC_BLOCK)
    sl = -(-n_slc // LANES) * LANES
    gw = NSA_REP * HEAD_DIM
    q3 = q.reshape(batch, t_new, NSA_WIDTH)
    body = functools.partial(_cmp_attn_s_body, past=past, n_blk=n_blk, n_slc=n_slc, n_sel=min(N_SELECT, n_slc))
    return pl.pallas_call(
        body,
        grid=(batch, N_NSA_KV),
        in_specs=[pl.BlockSpec((1, t_new, gw), lambda b, g: (b, 0, g))]
        + _p_specs(n_chunks, lambda b, g: (b, g))
        + [pl.BlockSpec((NSA_REP, t_new, n_chunks), lambda b, g: (g, 0, 0))],
        out_specs=[pl.BlockSpec((1, t_new, gw), lambda b, g: (b, 0, g)),
                   pl.BlockSpec((1, 1, t_new, sl), lambda b, g: (b, g, 0, 0))],
        out_shape=[jax.ShapeDtypeStruct((batch, t_new, NSA_WIDTH), F32),
                   jax.ShapeDtypeStruct((batch, N_NSA_KV, t_new, sl), F32)],
        compiler_params=_params(("arbitrary",) * 2),
        name="cmp_attn_sample",
    )(q3, pchunks, pchunks, pchunks, pchunks, bias_c)


_GKV = N_NSA_KV * HEAD_DIM


def _dec_tile(qbd, kv, dist, mask, table_ref, full_bias, t_new, m_s, l_s, acc_s):
    s = _dot_nt(qbd, kv[:, :_GKV].astype(BF16)) * ATTN_SCALE
    parts = []
    for hh in range(N_NSA_HEADS):
        rows = slice(hh * t_new, (hh + 1) * t_new)
        if full_bias:
            b = _t5_lookup(_t5_bucket(dist[rows]), table_ref, hh)
        else:
            b = jnp.full((t_new, s.shape[1]), table_ref[N_BUCKETS - 1, hh], F32)
        parts.append(s[rows] + b)
    s = jnp.concatenate(parts, axis=0)
    _online_update(s, mask, kv[:, _GKV:].astype(BF16), m_s, l_s, acc_s)


def _dec_finish(o_ref, t_new, l_s, acc_s):
    o = acc_s[...] / l_s[...]
    for g in range(N_NSA_KV):
        for r in range(NSA_REP):
            hh = g * NSA_REP + r
            o_ref[0, :, hh * HEAD_DIM:(hh + 1) * HEAD_DIM] = o[hh * t_new:(hh + 1) * t_new,
                                                             g * HEAD_DIM:(g + 1) * HEAD_DIM]


def _row_mask(m2, t_new):
    return jnp.concatenate([m2[g * t_new:(g + 1) * t_new] for g in range(N_NSA_KV) for _ in range(NSA_REP)], axis=0)


def _slc_s_body(pt_ref, table_ref, q_ref, *rest, n_pg, past, t_new):
    page_refs = rest[:n_pg]
    selp_ref, knew_ref, vnew_ref, selnew_ref, o_ref, m_s, l_s, acc_s = rest[n_pg:]
    j = pl.program_id(1)
    last = pl.num_programs(1) - 1
    grp_rows = NSA_REP * t_new
    q = (q_ref[0] * ATTN_SCALE).astype(BF16)
    t_pos = past + _iota((t_new, PAGE_SIZE), 0)
    lane = _iota((t_new, PAGE_SIZE), 1)

    @pl.when(j == 0)
    def _():
        _init_state(m_s, l_s, acc_s)

    def update(x, mask, vs):
        x = jnp.where(mask, x, NEG_INF)
        m_old = m_s[...]
        m_new = jnp.maximum(m_old, jnp.max(x, axis=-1, keepdims=True))
        p = jnp.where(mask, jnp.exp(x - m_new), 0.0)
        alpha = jnp.exp(m_old - m_new)
        l_s[...] = alpha * l_s[...] + jnp.sum(p, axis=-1, keepdims=True)
        pb = p.astype(BF16)
        pv = jnp.concatenate([jnp.dot(pb[g * grp_rows:(g + 1) * grp_rows], vs[g], preferred_element_type=F32)
                              for g in range(N_NSA_KV)], axis=0)
        acc_s[...] = alpha * acc_s[...] + pv
        m_s[...] = m_new

    def head_bias(hh, dist):
        return _t5_lookup(_t5_bucket(dist), table_ref, hh)

    dist_last = t_pos - ((j * n_pg + n_pg - 1) * PAGE_SIZE + lane)
    xs, masks, vs = [], [], []
    for g in range(N_NSA_KV):
        plane = lambda c: jnp.concatenate(
            [pr[0, pl.ds(c * N_NSA_KV + g, PAGE_SIZE, stride=2 * N_NSA_KV), :] for pr in page_refs],
            axis=0).astype(BF16)
        x = _dot_nt(q[g * grp_rows:(g + 1) * grp_rows], plane(0))
        vs.append(plane(1))
        for r in range(NSA_REP):
            hh = g * NSA_REP + r
            far = jnp.full((t_new, (n_pg - 1) * PAGE_SIZE), table_ref[N_BUCKETS - 1, hh], F32)
            bias = jnp.concatenate([far, head_bias(hh, dist_last)], axis=1)
            xs.append(x[r * t_new:(r + 1) * t_new] + bias)
        picked = []
        for k in range(n_pg):
            m2 = selp_ref[0, k][g * t_new:(g + 1) * t_new]
            picked.append(jnp.where(lane < SLC_BLOCK, m2[:, 0:1], m2[:, 1:2]))
        masks.extend([jnp.concatenate(picked, axis=1)] * NSA_REP)
    update(jnp.concatenate(xs, axis=0), jnp.concatenate(masks, axis=0) > 0.5, vs)

    @pl.when(j == last)
    def _():
        dist = t_pos - (past + lane)
        xs, masks, vs = [], [], []
        for g in range(N_NSA_KV):
            x = _dot_nt(q[g * grp_rows:(g + 1) * grp_rows], knew_ref[0, g].astype(BF16))
            vs.append(vnew_ref[0, g].astype(BF16))
            for r in range(NSA_REP):
                xs.append(x[r * t_new:(r + 1) * t_new] + head_bias(g * NSA_REP + r, dist))
            sel_g = jnp.broadcast_to(selnew_ref[0][g * t_new:(g + 1) * t_new, 0:1], (t_new, PAGE_SIZE)) > 0.5
            masks.extend([sel_g & (dist >= 0) & (lane < t_new)] * NSA_REP)
        update(jnp.concatenate(xs, axis=0), jnp.concatenate(masks, axis=0), vs)
        o = acc_s[...] / l_s[...]
        for hh in range(N_NSA_HEADS):
            o_ref[0, :, hh * HEAD_DIM:(hh + 1) * HEAD_DIM] = o[hh * t_new:(hh + 1) * t_new]


def _slc_sample(table, page_table, q_rows, cache, selp, knew, vnew, selnew, past, t_new):
    batch, n_pages = page_table.shape
    n_pg = math.gcd(n_pages, 8)
    n_phys = cache.shape[0]
    assert t_new <= PAGE_SIZE and past % PAGE_SIZE == 0 and PAGE_SIZE >= MAX_DISTANCE
    page_rows = PAGE_SIZE * 2 * N_NSA_KV
    x = cache.reshape(n_phys, page_rows, HEAD_DIM)
    rows = N_NSA_HEADS * t_new

    def page_spec(k):
        return pl.BlockSpec((1, page_rows, HEAD_DIM), lambda b, j, pt: (pt[b * n_pages + j * n_pg + k], 0, 0))
    new_spec = pl.BlockSpec((1, N_NSA_KV, PAGE_SIZE, HEAD_DIM), lambda b, j, pt: (b, 0, 0, 0))
    grid_spec = pltpu.PrefetchScalarGridSpec(
        num_scalar_prefetch=1,
        grid=(batch, n_pages // n_pg),
        in_specs=[pl.BlockSpec(memory_space=pltpu.SMEM),
                  pl.BlockSpec((1, rows, HEAD_DIM), lambda b, j, pt: (b, 0, 0))]
        + [page_spec(k) for k in range(n_pg)]
        + [pl.BlockSpec((1, n_pg, N_NSA_KV * t_new, 2), lambda b, j, pt: (b, j, 0, 0)),
           new_spec, new_spec,
           pl.BlockSpec((1, N_NSA_KV * t_new, 2), lambda b, j, pt: (b, 0, 0))],
        out_specs=pl.BlockSpec((1, t_new, NSA_WIDTH), lambda b, j, pt: (b, 0, 0)),
        scratch_shapes=[pltpu.VMEM((rows, 1), F32), pltpu.VMEM((rows, 1), F32), pltpu.VMEM((rows, HEAD_DIM), F32)],
    )
    return pl.pallas_call(
        functools.partial(_slc_s_body, n_pg=n_pg, past=past, t_new=t_new),
        grid_spec=grid_spec,
        out_shape=jax.ShapeDtypeStruct((batch, t_new, NSA_WIDTH), F32),
        compiler_params=_params(("arbitrary", "arbitrary")),
        name="slc_sample",
    )(page_table.reshape(-1), table, q_rows, *([x] * n_pg), selp, knew, vnew, selnew)


def _win_s_body(table_ref, q_ref, kv_ref, o_ref, m_s, l_s, acc_s, *, past, t_new, wb):
    rows = N_NSA_HEADS * t_new
    qbd = q_ref[0]
    t_pos = past + _mod_pow2(_iota((rows, PAGE_SIZE), 0), t_new)
    lane = _iota((rows, PAGE_SIZE), 1)
    _init_state(m_s, l_s, acc_s)
    for k in range(kv_ref.shape[1] // PAGE_SIZE):
        k_pos = past - wb + k * PAGE_SIZE + lane
        dist = t_pos - k_pos
        mask = (dist >= 0) & (dist <= WINDOW) & (k_pos >= 0) & (k * PAGE_SIZE + lane < wb + t_new)
        _dec_tile(qbd, kv_ref[0, k * PAGE_SIZE:(k + 1) * PAGE_SIZE, :], dist, mask, table_ref, True, t_new,
                  m_s, l_s, acc_s)
    _dec_finish(o_ref, t_new, l_s, acc_s)


def _win_sample(table, qbd, win_all_padded, past, t_new, wb):
    batch, n_keys, _ = win_all_padded.shape
    rows = N_NSA_HEADS * t_new
    return pl.pallas_call(
        functools.partial(_win_s_body, past=past, t_new=t_new, wb=wb),
        grid=(batch,),
        in_specs=[pl.BlockSpec(memory_space=pltpu.SMEM),
                  pl.BlockSpec((1, rows, _GKV), lambda b: (b, 0, 0)),
                  pl.BlockSpec((1, n_keys, 2 * _GKV), lambda b: (b, 0, 0))],
        out_specs=pl.BlockSpec((1, t_new, NSA_WIDTH), lambda b: (b, 0, 0)),
        out_shape=jax.ShapeDtypeStruct((batch, t_new, NSA_WIDTH), F32),
        scratch_shapes=[pltpu.VMEM((rows, 1), F32), pltpu.VMEM((rows, 1), F32), pltpu.VMEM((rows, _GKV), F32)],
        compiler_params=_params(("arbitrary",)),
        name="win_sample",
    )(table, qbd, win_all_padded)


_FOX_PAGE_LANES = PAGE_SIZE * N_FOX_HEADS


def _fox_s_body(pt_ref, q_ref, *rest, n_pg, t_new):
    kv_refs = rest[:n_pg]
    lf_refs = rest[n_pg:2 * n_pg]
    tri_ref, knew_ref, vnew_ref, lfn_ref, o_ref, m_s, l_s, acc_s, off_s = rest[2 * n_pg:]
    j = pl.program_id(1)
    rows = N_FOX_HEADS * t_new
    pw = _FOX_PAGE_LANES
    qb = (q_ref[0] * ATTN_SCALE).astype(BF16)
    lane = _iota((rows, PAGE_SIZE), 1)
    t_row = _mod_pow2(_iota((rows, PAGE_SIZE), 0), t_new)
    h_row = _div_pow2(_iota((rows, PAGE_SIZE), 0), t_new)
    lfn = lfn_ref[0]
    c_new = jnp.sum(jnp.where(lane <= t_row, lfn, 0.0), axis=-1, keepdims=True)

    @pl.when(j == 0)
    def _():
        _init_state(m_s, l_s, acc_s)
        off_s[...] = jnp.zeros(off_s.shape, F32)
        u_l = _div_pow2(lane, N_FOX_HEADS)
        bias = jnp.zeros((rows, PAGE_SIZE), F32)
        for u in range(t_new):
            col = jnp.sum(jnp.where((lane > u) & (lane <= t_row), lfn, 0.0), axis=-1, keepdims=True)
            bias = jnp.where(u_l == u, col, bias)
        x = _dot_nt(qb, knew_ref[0].astype(BF16)) + bias
        mask = (u_l <= t_row) & (u_l < t_new) & (_mod_pow2(lane, N_FOX_HEADS) == h_row)
        _online_update(x, mask, vnew_ref[0].astype(BF16), m_s, l_s, acc_s)

    lf = jnp.concatenate([r[0] for r in lf_refs], axis=0)
    later = _dot01_right(lf, tri_ref[...])
    tot = jnp.sum(lf, axis=-1, keepdims=True)
    head_lane = jnp.where(
        _mod_pow2(_iota((rows, pw), 1), N_FOX_HEADS) == _div_pow2(_iota((rows, pw), 0), t_new), 0.0, NEG_INF)
    spread = lambda a: jnp.concatenate(
        [jnp.broadcast_to(a[h:h + 1], (t_new, a.shape[1])) for h in range(N_FOX_HEADS)], axis=0)
    off = off_s[:, 0:1]
    xs, vs = [], []
    for k in range(n_pg):
        kk = kv_refs[k][0, :, 0].reshape(pw, HEAD_DIM).astype(BF16)
        vs.append(kv_refs[k][0, :, 1].reshape(pw, HEAD_DIM).astype(BF16))
        hs = slice(k * N_FOX_HEADS, (k + 1) * N_FOX_HEADS)
        xs.append(_dot_nt(qb, kk) + spread(later[hs]) + (spread(off) + c_new) + head_lane)
        off = off + tot[hs]
    off_s[...] = jnp.broadcast_to(off, off_s.shape)
    x = jnp.concatenate(xs, axis=1)
    m_old = m_s[...]
    m_new = jnp.maximum(m_old, jnp.max(x, axis=-1, keepdims=True))
    p = jnp.exp(x - m_new)
    alpha = jnp.exp(m_old - m_new)
    l_s[...] = alpha * l_s[...] + jnp.sum(p, axis=-1, keepdims=True)
    acc_s[...] = alpha * acc_s[...] + jnp.dot(p.astype(BF16), jnp.concatenate(vs, axis=0),
                                              preferred_element_type=F32)
    m_s[...] = m_new

    @pl.when(j == pl.num_programs(1) - 1)
    def _():
        o = acc_s[...] / l_s[...]
        for h in range(N_FOX_HEADS):
            o_ref[0, :, h * HEAD_DIM:(h + 1) * HEAD_DIM] = o[h * t_new:(h + 1) * t_new]


def _fox_sample(page_table, q_rows, cache, logf_t, tri, knew, vnew, lfn, t_new):
    batch, n_pages = page_table.shape
    n_pg = math.gcd(n_pages, 8)
    rows = N_FOX_HEADS * t_new
    page = lambda k: (lambda b, j, pt: pt[b * n_pages + n_pages - 1 - (j * n_pg + k)])

    def kv_spec(k):
        return pl.BlockSpec((1, PAGE_SIZE, 2, N_FOX_HEADS, HEAD_DIM), lambda b, j, pt: (page(k)(b, j, pt), 0, 0, 0, 0))

    def lf_spec(k):
        return pl.BlockSpec((1, N_FOX_HEADS, PAGE_SIZE), lambda b, j, pt: (page(k)(b, j, pt), 0, 0))
    new_spec = pl.BlockSpec((1, PAGE_SIZE, HEAD_DIM), lambda b, j, pt: (b, 0, 0))
    grid_spec = pltpu.PrefetchScalarGridSpec(
        num_scalar_prefetch=1,
        grid=(batch, n_pages // n_pg),
        in_specs=[pl.BlockSpec((1, rows, HEAD_DIM), lambda b, j, pt: (b, 0, 0))]
        + [kv_spec(k) for k in range(n_pg)] + [lf_spec(k) for k in range(n_pg)]
        + [pl.BlockSpec(tri.shape, lambda b, j, pt: (0, 0)), new_spec, new_spec,
           pl.BlockSpec((1, rows, PAGE_SIZE), lambda b, j, pt: (b, 0, 0))],
        out_specs=pl.BlockSpec((1, t_new, FOX_WIDTH), lambda b, j, pt: (b, 0, 0)),
        scratch_shapes=[pltpu.VMEM((rows, 1), F32), pltpu.VMEM((rows, 1), F32), pltpu.VMEM((rows, HEAD_DIM), F32),
                        pltpu.VMEM((N_FOX_HEADS, LANES), F32)],
    )
    return pl.pallas_call(
        functools.partial(_fox_s_body, n_pg=n_pg, t_new=t_new),
        grid_spec=grid_spec,
        out_shape=jax.ShapeDtypeStruct((batch, t_new, FOX_WIDTH), F32),
        compiler_params=_params(("arbitrary", "arbitrary")),
        name="fox_sample",
    )(page_table.reshape(-1), q_rows, *([cache] * n_pg), *([logf_t] * n_pg), tri, knew, vnew, lfn)


def _pack_weights(w_in, b_f, w_cmp, pe_cmp, w_o, w_up, w_down):
    c_gt = NSA_WIDTH + 3 * NSA_KV_WIDTH
    c_qf = c_gt + GATE_LANES
    c_fl = c_qf + 3 * FOX_WIDTH
    wm = jnp.concatenate([w_in[:, :c_gt], w_in[:, c_qf:c_fl]], axis=1).astype(BF16)
    ws = jnp.concatenate([w_in[:, c_gt:c_qf], w_in[:, c_fl:],
                          jnp.zeros((w_in.shape[0], LANES - GATE_LANES - N_FOX_HEADS), F32)], axis=1).astype(BF16)
    bf128 = jnp.zeros((1, LANES), F32).at[0, LOGF_LANE0:LOGF_LANE0 + N_FOX_HEADS].set(b_f)
    wc = w_cmp.reshape(2, CMP_SLOTS, CMP_STRIDE, HEAD_DIM, HEAD_DIM).transpose(0, 2, 3, 1, 4).reshape(
        2, _CHUNK_K, CMP_SLOTS * HEAD_DIM).astype(BF16)
    pe8 = jnp.zeros((2, SUBLANES, _CHUNK_K), F32).at[:, :CMP_SLOTS].set(pe_cmp.reshape(2, CMP_SLOTS, _CHUNK_K))
    return wm, ws, bf128, wc, pe8, w_o.astype(BF16), w_up.astype(BF16), w_down.astype(BF16)


def _block_diag_q(q, t_new, n_grp, per_grp):
    batch = q.shape[0]
    q5 = q.reshape(batch, t_new, n_grp, per_grp, HEAD_DIM).transpose(0, 2, 3, 1, 4)
    eye = jnp.eye(n_grp, dtype=q.dtype)
    out = q5[:, :, :, :, None, :] * eye[None, :, None, None, :, None]
    return out.reshape(batch, n_grp * per_grp * t_new, n_grp * HEAD_DIM).astype(BF16)


def _prompt_pass(x, table, lw, packed):
    g_attn, g_nsa_out, g_fox_out, g_mlp, g_final = lw
    wm, ws, bf128, wc, pe8, wo, wu, wd = packed
    batch, seq, d = x.shape
    assert seq % 256 == 0 and seq >= WINDOW
    x2d = x.reshape(batch * seq, d)
    q, ckv, skv, wkv, qf, kvf, sm = _project(x2d, g_attn, wm, ws, bf128)
    pchunks = _chunkproj_prompt(ckv, batch, pe8, wc)
    n_chunks = seq // CMP_STRIDE
    dist_c = jnp.arange(seq, dtype=I32)[:, None] - (jnp.arange(n_chunks, dtype=I32)[None, :] * CMP_STRIDE
                                                     + (CMP_BLOCK - 1))
    bias_c = _t5_bias(table, dist_c)
    o_c, sel = _cmp_attn_prompt(q, pchunks, bias_c, batch, seq)
    dist_tz = jnp.arange(_TK, dtype=I32)[:, None] - jnp.arange(2 * _TK, dtype=I32)[None, :] + _TK
    tz = _t5_bias(table, dist_tz)
    o_s = _nsa_flash_prompt("slc", table, q, skv, tz, sel, batch, seq)
    o_w = _nsa_flash_prompt("win", table, q, wkv, tz, None, batch, seq)
    c, c_t = _cumsum_prompt(sm, batch, seq)
    c_rows = c_t[:, LOGF_LANE0:LOGF_LANE0 + N_FOX_HEADS, :].reshape(batch, N_FOX_HEADS, seq // _FOX_T, _FOX_T)
    o_f = _fox_flash_prompt(qf, kvf, c, c_rows, batch, seq)
    h = _outproj(x2d, o_c.reshape(-1, NSA_WIDTH), o_s.reshape(-1, NSA_WIDTH), o_w.reshape(-1, NSA_WIDTH),
                 o_f.reshape(-1, FOX_WIDTH), sm, g_nsa_out, g_fox_out, wo)
    y = _mlp(h, g_mlp, g_final, wu, wd).reshape(batch, seq, d)
    kv_shape = (1, batch, seq, 2, N_NSA_KV, HEAD_DIM)
    wkv5 = wkv.reshape(kv_shape)
    logf = sm.reshape(batch, seq, LANES)[:, :, LOGF_LANE0:LOGF_LANE0 + N_FOX_HEADS]
    return (y, ckv.reshape(kv_shape), skv.reshape(kv_shape),
            kvf.reshape(1, batch, seq, 2, N_FOX_HEADS, HEAD_DIM), logf[None],
            wkv5[:, :, seq - min(WINDOW, seq):])


def _sample_pass(x, cache_cmp, cache_slc, cache_fox, cache_logf, win_buf, page_table, table, lw, packed):
    g_attn, g_nsa_out, g_fox_out, g_mlp, g_final = lw
    wm, ws, bf128, wc, pe8, wo, wu, wd = packed
    batch, t_new, d = x.shape
    n_pages = page_table.shape[1]
    past = n_pages * PAGE_SIZE
    assert t_new < CMP_STRIDE and t_new % SUBLANES == 0 and t_new * N_FOX_HEADS <= PAGE_SIZE
    x2d = x.reshape(batch * t_new, d)
    q, ckv, skv, wkv, qf, kvf, sm = _project(x2d, g_attn, wm, ws, bf128)
    pchunks = _chunkproj_sample(cache_cmp, page_table, pe8, wc)
    n_chunks = pchunks.shape[1]
    pos = past + jnp.arange(t_new, dtype=I32)
    dist_c = pos[:, None] - (jnp.arange(n_chunks, dtype=I32)[None, :] * CMP_STRIDE + (CMP_BLOCK - 1))
    bias_c = _t5_bias(table, dist_c)
    o_c, sel = _cmp_attn_sample(q, pchunks, bias_c, batch, t_new, past)
    q5 = q.reshape(batch, t_new, N_NSA_KV, NSA_REP, HEAD_DIM)
    q_rows = q5.transpose(0, 2, 3, 1, 4).reshape(batch, N_NSA_HEADS * t_new, HEAD_DIM)
    n_past_blk = past // SLC_BLOCK
    bpp = PAGE_SIZE // SLC_BLOCK
    selp = sel[..., :n_past_blk].reshape(batch, N_NSA_KV, t_new, n_pages, bpp).transpose(0, 3, 1, 2, 4).reshape(
        batch, n_pages, N_NSA_KV * t_new, bpp)
    selnew = jnp.broadcast_to(sel[..., n_past_blk:n_past_blk + 1].reshape(batch, N_NSA_KV * t_new, 1),
                              (batch, N_NSA_KV * t_new, bpp))
    skv5 = skv.reshape(batch, t_new, 2, N_NSA_KV, HEAD_DIM).transpose(2, 0, 3, 1, 4)
    skv5 = jnp.pad(skv5, ((0, 0), (0, 0), (0, 0), (0, PAGE_SIZE - t_new), (0, 0)))
    o_s = _slc_sample(table, page_table, q_rows, cache_slc, selp, skv5[0], skv5[1], selnew, past, t_new)
    qbd = _block_diag_q(q.reshape(batch, t_new, NSA_WIDTH), t_new, N_NSA_KV, NSA_REP)
    wb = win_buf.shape[1]
    win_all = jnp.concatenate([win_buf.reshape(batch, wb, NSA_KV_WIDTH), wkv.reshape(batch, t_new, NSA_KV_WIDTH)],
                              axis=1)
    n_keys = -(-(wb + t_new) // PAGE_SIZE) * PAGE_SIZE
    o_w = _win_sample(table, qbd, jnp.pad(win_all, ((0, 0), (0, n_keys - wb - t_new), (0, 0))), past, t_new, wb)
    logf_new = sm.reshape(batch, t_new, LANES)[:, :, LOGF_LANE0:LOGF_LANE0 + N_FOX_HEADS]
    logf_t = cache_logf.transpose(0, 2, 1)
    qf_rows = qf.reshape(batch, t_new, N_FOX_HEADS, HEAD_DIM).transpose(0, 2, 1, 3).reshape(
        batch, N_FOX_HEADS * t_new, HEAD_DIM)
    kvf5 = kvf.reshape(batch, t_new, 2, N_FOX_HEADS * HEAD_DIM).transpose(2, 0, 1, 3).reshape(
        2, batch, t_new * N_FOX_HEADS, HEAD_DIM)
    kvf5 = jnp.pad(kvf5, ((0, 0), (0, 0), (0, PAGE_SIZE - t_new * N_FOX_HEADS), (0, 0)))
    lfn = jnp.broadcast_to(logf_new.transpose(0, 2, 1)[:, :, None, :], (batch, N_FOX_HEADS, t_new, t_new))
    lfn = jnp.pad(lfn.reshape(batch, N_FOX_HEADS * t_new, t_new), ((0, 0), (0, 0), (0, PAGE_SIZE - t_new)))
    tri = (jnp.arange(PAGE_SIZE)[:, None] > jnp.arange(_FOX_PAGE_LANES)[None, :] // N_FOX_HEADS).astype(BF16)
    o_f = _fox_sample(page_table, qf_rows, cache_fox, logf_t, tri, kvf5[0], kvf5[1], lfn, t_new)
    h = _outproj(x2d, o_c.reshape(-1, NSA_WIDTH), o_s.reshape(-1, NSA_WIDTH), o_w.reshape(-1, NSA_WIDTH),
                 o_f.reshape(-1, FOX_WIDTH), sm, g_nsa_out, g_fox_out, wo)
    y = _mlp(h, g_mlp, g_final, wu, wd).reshape(batch, t_new, d)
    kv_shape = (1, batch, t_new, 2, N_NSA_KV, HEAD_DIM)
    return (y, ckv.reshape(kv_shape), skv.reshape(kv_shape),
            kvf.reshape(1, batch, t_new, 2, N_FOX_HEADS, HEAD_DIM), logf_new[None],
            win_all[:, t_new:].reshape(1, batch, wb, 2, N_NSA_KV, HEAD_DIM))


def kernel(x_prompt, x_sample, cache_cmp_kv, cache_slc_kv, cache_fox_kv, cache_fox_logf, state_win_kv, page_table,
           t5_table, g_attn, w_in, b_f, w_cmp, pe_cmp, g_nsa_out, g_fox_out, w_o, g_mlp, w_up, w_down, g_final):
    assert g_attn.shape[0] == 1, "single-layer step"
    assert x_prompt.shape[-1] == NSA_WIDTH + FOX_WIDTH
    packed = _pack_weights(w_in[0], b_f[0], w_cmp[0], pe_cmp[0], w_o[0], w_up[0], w_down[0])
    lw = (g_attn[0], g_nsa_out[0], g_fox_out[0], g_mlp[0], g_final)
    yp, cmp_p, slc_p, fox_p, logf_p, win_p = _prompt_pass(x_prompt, t5_table, lw, packed)
    ys, cmp_s, slc_s, fox_s, logf_s, win_s = _sample_pass(
        x_sample, cache_cmp_kv[0], cache_slc_kv[0], cache_fox_kv[0], cache_fox_logf[0], state_win_kv[0],
        page_table, t5_table, lw, packed)
    return (yp, ys, cmp_p, slc_p, fox_p, logf_p, win_p, cmp_s, slc_s, fox_s, logf_s, win_s)
```

```python
import functools
import math

import jax
import jax.numpy as jnp
from jax import lax
from jax.experimental import pallas as pl
from jax.experimental.pallas import tpu as pltpu

F32 = jnp.float32
BF16 = jnp.bfloat16
I32 = jnp.int32

HEAD_DIM = 128
N_NSA_HEADS = 8
N_NSA_KV = 2
NSA_REP = 4
N_FOX_HEADS = 8
NSA_WIDTH = N_NSA_HEADS * HEAD_DIM
FOX_WIDTH = N_FOX_HEADS * HEAD_DIM
NSA_KV_WIDTH = 2 * N_NSA_KV * HEAD_DIM
CMP_BLOCK = 32
CMP_STRIDE = 16
CMP_SLOTS = CMP_BLOCK // CMP_STRIDE
SLC_BLOCK = 64
N_SELECT = 16
WINDOW = 512
N_BUCKETS = 32
MAX_DISTANCE = 128
PAGE_SIZE = 128
RMS_EPS = 1e-6
NEG_INF = -1e30
FORCE_SCORE = 1e9
ATTN_SCALE = HEAD_DIM ** -0.5

LANES = 128
SUBLANES = 8
VMEM_LIMIT = 56 * 1024 * 1024
GATE_LANES = 3 * N_NSA_HEADS
LOGF_LANE0 = GATE_LANES


def _params(sem):
    return pltpu.CompilerParams(dimension_semantics=sem, vmem_limit_bytes=VMEM_LIMIT)


def _iota(shape, dim):
    return lax.broadcasted_iota(I32, shape, dim)


def _log2(n):
    assert n > 0 and n & (n - 1) == 0
    return n.bit_length() - 1


def _div_pow2(x, n):
    return jnp.right_shift(x, _log2(n))


def _mod_pow2(x, n):
    return jnp.bitwise_and(x, n - 1)


def _dot_nt(a, b):
    return lax.dot_general(a, b, (((1,), (1,)), ((), ())), preferred_element_type=F32)


def _split3(x):
    hi = x.astype(BF16)
    r1 = x - hi.astype(F32)
    mid = r1.astype(BF16)
    lo = (r1 - mid.astype(F32)).astype(BF16)
    return hi, mid, lo


def _dot01_right(x, m01):
    hi, mid, lo = _split3(x)
    d = lambda a: jnp.dot(a, m01, preferred_element_type=F32)
    return d(hi) + d(mid) + d(lo)


def _dot01_left(m01, x):
    hi, mid, lo = _split3(x)
    d = lambda a: jnp.dot(m01, a, preferred_element_type=F32)
    return d(hi) + d(mid) + d(lo)


def _dot01_nt(m01, x):
    hi, mid, lo = _split3(x)
    return _dot_nt(m01, hi) + _dot_nt(m01, mid) + _dot_nt(m01, lo)


def _t5_bucket(dist):
    n = jnp.maximum(dist, 0)
    exact = N_BUCKETS // 2
    nf = jnp.maximum(n, 1).astype(F32)
    far = exact + (jnp.log(nf / exact) / math.log(MAX_DISTANCE / exact) * (N_BUCKETS - exact)).astype(I32)
    return jnp.where(n < exact, n, jnp.minimum(far, N_BUCKETS - 1))


def _t5_lookup(bucket, table_ref, head):
    acc = jnp.zeros(bucket.shape, F32)
    for k in range(N_BUCKETS):
        acc = jnp.where(bucket == k, table_ref[k, head], acc)
    return acc


def _online_update(s, mask, v, m_s, l_s, acc_s):
    if mask is not None:
        s = jnp.where(mask, s, NEG_INF)
    m_old = m_s[...]
    m_new = jnp.maximum(m_old, jnp.max(s, axis=-1, keepdims=True))
    p = jnp.exp(s - m_new)
    if mask is not None:
        p = jnp.where(mask, p, 0.0)
    alpha = jnp.exp(m_old - m_new)
    l_s[...] = alpha * l_s[...] + jnp.sum(p, axis=-1, keepdims=True)
    acc_s[...] = alpha * acc_s[...] + jnp.dot(p.astype(BF16), v, preferred_element_type=F32)
    m_s[...] = m_new


def _init_state(m_s, l_s, acc_s):
    m_s[...] = jnp.full(m_s.shape, NEG_INF, F32)
    l_s[...] = jnp.zeros(l_s.shape, F32)
    acc_s[...] = jnp.zeros(acc_s.shape, F32)


def _t5_bias_body(table_ref, dist_ref, out_ref):
    bucket = _t5_bucket(dist_ref[...])
    for h in range(N_NSA_HEADS):
        out_ref[h] = _t5_lookup(bucket, table_ref, h)


def _t5_bias(table, dist):
    rows, cols = dist.shape
    tr = min(rows, 256)
    assert rows % tr == 0
    return pl.pallas_call(
        _t5_bias_body,
        grid=(rows // tr,),
        in_specs=[pl.BlockSpec(memory_space=pltpu.SMEM),
                  pl.BlockSpec((tr, cols), lambda i: (i, 0))],
        out_specs=pl.BlockSpec((N_NSA_HEADS, tr, cols), lambda i: (0, i, 0)),
        out_shape=jax.ShapeDtypeStruct((N_NSA_HEADS, rows, cols), F32),
        compiler_params=_params(("arbitrary",)),
        name="t5_bias",
    )(table, dist)


_PROJ_WIDTHS = (NSA_WIDTH, NSA_KV_WIDTH, NSA_KV_WIDTH, NSA_KV_WIDTH, FOX_WIDTH, 2 * FOX_WIDTH)
_PROJ_COL_CHUNK = 512


def _proj_body(x_ref, g_ref, wm_ref, ws_ref, bf_ref, q_ref, ckv_ref, skv_ref, wkv_ref, qf_ref, kvf_ref, sm_ref):
    x = x_ref[...]
    xn = x * lax.rsqrt(jnp.mean(x * x, axis=-1, keepdims=True) + RMS_EPS) * g_ref[...]
    xb = xn.astype(BF16)
    off = 0
    for ref in (q_ref, ckv_ref, skv_ref, wkv_ref, qf_ref, kvf_ref):
        width = ref.shape[-1]
        for c in range(0, width, _PROJ_COL_CHUNK):
            ref[:, c:c + _PROJ_COL_CHUNK] = jnp.dot(
                xb, wm_ref[:, off + c:off + c + _PROJ_COL_CHUNK], preferred_element_type=F32)
        off += width
    s = jnp.dot(xb, ws_ref[...], preferred_element_type=F32)
    z = s + bf_ref[...]
    lane = _iota(s.shape, 1)
    logf = jnp.minimum(z, 0.0) - jnp.log1p(jnp.exp(-jnp.abs(z)))
    sig = 1.0 / (1.0 + jnp.exp(-s))
    sm_ref[...] = jnp.where(lane < GATE_LANES, sig, logf)


def _project(x2d, g_attn, wm, ws, bf128):
    n, d = x2d.shape
    tm = min(n, 256)
    assert n % tm == 0
    const = lambda shape: pl.BlockSpec(shape, lambda i: (0, 0), pipeline_mode=pl.Buffered(1))
    widths = _PROJ_WIDTHS + (LANES,)
    return pl.pallas_call(
        _proj_body,
        grid=(n // tm,),
        in_specs=[pl.BlockSpec((tm, d), lambda i: (i, 0)), const((1, d)), const(wm.shape), const(ws.shape),
                  const((1, LANES))],
        out_specs=[pl.BlockSpec((tm, w), lambda i: (i, 0)) for w in widths],
        out_shape=[jax.ShapeDtypeStruct((n, w), F32) for w in widths],
        compiler_params=_params(("arbitrary",)),
        name="in_proj",
    )(x2d, g_attn.reshape(1, d), wm, ws, bf128)


_CHUNK_ROW = CMP_STRIDE * NSA_KV_WIDTH
_CHUNK_K = CMP_STRIDE * HEAD_DIM


def _chunk_cols(c, g):
    return [r * NSA_KV_WIDTH + c * N_NSA_KV * HEAD_DIM + g * HEAD_DIM for r in range(CMP_STRIDE)]


def _chunkproj_compute(get_rows, pe_ref, w_ref, out_ref):
    for c in range(2):
        w = w_ref[c]
        pos = jnp.dot(pe_ref[c].astype(BF16), w, preferred_element_type=F32)
        for g in range(N_NSA_KV):
            y = jnp.dot(get_rows(c, g).astype(BF16), w, preferred_element_type=F32)
            for u in range(CMP_SLOTS):
                col = ((u * 2 + c) * N_NSA_KV + g) * HEAD_DIM
                out_ref[0, :, col:col + HEAD_DIM] = (
                    y[:, u * HEAD_DIM:(u + 1) * HEAD_DIM] + pos[u:u + 1, u * HEAD_DIM:(u + 1) * HEAD_DIM])


def _chunkproj_p_body(x_ref, pe_ref, w_ref, out_ref):
    def get_rows(c, g):
        return jnp.concatenate([x_ref[0, :, o:o + HEAD_DIM] for o in _chunk_cols(c, g)], axis=1)
    _chunkproj_compute(get_rows, pe_ref, w_ref, out_ref)


def _chunkproj_prompt(ckv2d, batch, pe8, wc):
    n_chunks = ckv2d.shape[0] // batch // CMP_STRIDE
    x = ckv2d.reshape(batch, n_chunks, _CHUNK_ROW)
    out_w = CMP_SLOTS * NSA_KV_WIDTH
    return pl.pallas_call(
        _chunkproj_p_body,
        grid=(batch,),
        in_specs=[pl.BlockSpec((1, n_chunks, _CHUNK_ROW), lambda b: (b, 0, 0)),
                  pl.BlockSpec(pe8.shape, lambda b: (0, 0, 0)),
                  pl.BlockSpec(wc.shape, lambda b: (0, 0, 0))],
        out_specs=pl.BlockSpec((1, n_chunks, out_w), lambda b: (b, 0, 0)),
        out_shape=jax.ShapeDtypeStruct((batch, n_chunks, out_w), F32),
        compiler_params=_params(("arbitrary",)),
        name="chunkproj_prompt",
    )(x, pe8, wc)


def _chunkproj_s_body(pt_ref, *refs, n_pg):
    page_refs = refs[:n_pg]
    pe_ref, w_ref, out_ref = refs[n_pg:]
    cpp = PAGE_SIZE // CMP_STRIDE
    planes = 2 * N_NSA_KV

    def get_rows(c, g):
        per_r = [jnp.concatenate([pr[0, pl.ds(r * planes + c * N_NSA_KV + g, cpp, stride=CMP_STRIDE * planes), :]
                                  for pr in page_refs], axis=0) for r in range(CMP_STRIDE)]
        return jnp.concatenate(per_r, axis=1)
    _chunkproj_compute(get_rows, pe_ref, w_ref, out_ref)


def _chunkproj_sample(cache, page_table, pe8, wc):
    n_phys = cache.shape[0]
    batch, n_pages = page_table.shape
    cpp = PAGE_SIZE // CMP_STRIDE
    n_pg = math.gcd(n_pages, 32)
    page_rows = PAGE_SIZE * 2 * N_NSA_KV
    x = cache.reshape(n_phys, page_rows, HEAD_DIM)
    out_w = CMP_SLOTS * NSA_KV_WIDTH

    def page_spec(k):
        return pl.BlockSpec((1, page_rows, HEAD_DIM),
                            lambda b, j, pt: (pt[b * n_pages + j * n_pg + k], 0, 0))
    grid_spec = pltpu.PrefetchScalarGridSpec(
        num_scalar_prefetch=1,
        grid=(batch, n_pages // n_pg),
        in_specs=[page_spec(k) for k in range(n_pg)] + [
            pl.BlockSpec(pe8.shape, lambda b, j, pt: (0, 0, 0)),
            pl.BlockSpec(wc.shape, lambda b, j, pt: (0, 0, 0))],
        out_specs=pl.BlockSpec((1, n_pg * cpp, out_w), lambda b, j, pt: (b, j, 0)),
    )
    return pl.pallas_call(
        functools.partial(_chunkproj_s_body, n_pg=n_pg),
        grid_spec=grid_spec,
        out_shape=jax.ShapeDtypeStruct((batch, n_pages * cpp, out_w), F32),
        compiler_params=_params(("arbitrary", "arbitrary")),
        name="chunkproj_sample",
    )(page_table.reshape(-1), *([x] * n_pg), pe8, wc)


def _cmp_kv(p00, p10, p01, p11):
    n = p00.shape[1]
    k = p00[0] + pltpu.roll(p10[0], n - 1, 0)
    v = p01[0] + pltpu.roll(p11[0], n - 1, 0)
    return k.astype(BF16), v.astype(BF16)


def _p_specs(n_chunks, index):
    def spec(u, c):
        return pl.BlockSpec((1, n_chunks, HEAD_DIM),
                            lambda *a: (index(*a)[0], 0, (u * 2 + c) * N_NSA_KV + index(*a)[1]))
    return [spec(0, 0), spec(1, 0), spec(0, 1), spec(1, 1)]


def _cmp_attn_p_body(q_ref, p00, p10, p01, p11, bias_ref, oc_ref, sel_ref, *, tq, n_blk, n_slc, n_sel):
    i = pl.program_id(2)
    n_chunks = p00.shape[1]
    k, v = _cmp_kv(p00, p10, p01, p11)
    t = i * tq + _iota((tq, n_chunks), 0)
    n = _iota((tq, n_chunks), 1)
    mask = (n * CMP_STRIDE + (CMP_BLOCK - 1) <= t) & (n < n_blk)
    psum = jnp.zeros((tq, n_chunks), F32)
    for r in range(NSA_REP):
        qr = q_ref[0, :, r * HEAD_DIM:(r + 1) * HEAD_DIM].astype(BF16)
        s = _dot_nt(qr, k) * ATTN_SCALE + bias_ref[r]
        s = jnp.where(mask, s, NEG_INF)
        e = jnp.where(mask, jnp.exp(s - jnp.max(s, axis=-1, keepdims=True)), 0.0)
        l = jnp.sum(e, axis=-1, keepdims=True)
        p = e / jnp.where(l > 0.0, l, 1.0)
        oc_ref[0, :, r * HEAD_DIM:(r + 1) * HEAD_DIM] = jnp.dot(p.astype(BF16), v, preferred_element_type=F32)
        psum = psum + p
    sb = _iota((LANES, n_chunks), 0)
    nb = _iota((LANES, n_chunks), 1)
    cover = ((nb * CMP_STRIDE < sb * SLC_BLOCK + SLC_BLOCK) & (nb * CMP_STRIDE + CMP_BLOCK > sb * SLC_BLOCK))
    imp_t = _dot01_nt(cover.astype(BF16), psum)
    blk = _iota((LANES, tq), 0)
    cur = _div_pow2(i * tq + _iota((LANES, tq), 1), SLC_BLOCK)
    forced = (blk == 0) | (blk == cur) | (blk == cur - 1)
    score = jnp.where(forced, FORCE_SCORE, jnp.where(blk <= cur, imp_t, -FORCE_SCORE))
    rank = jnp.zeros((LANES, tq), F32)
    for s2 in range(n_slc):
        row = score[s2:s2 + 1, :]
        ahead = (row > score) | ((row == score) & (s2 < blk))
        rank = rank + ahead.astype(F32)
    sel_t = ((rank < n_sel) & (blk < n_slc)).astype(F32)
    sel_ref[0, 0] = sel_t


def _cmp_attn_prompt(q, pchunks, bias_c, batch, seq):
    tq = 256
    n_chunks = pchunks.shape[1]
    n_blk = n_chunks - CMP_SLOTS + 1
    n_slc = -(-seq // SLC_BLOCK)
    assert n_chunks == LANES and n_slc <= LANES and seq % tq == 0
    q3 = q.reshape(batch, seq, NSA_WIDTH)
    gw = NSA_REP * HEAD_DIM
    body = functools.partial(_cmp_attn_p_body, tq=tq, n_blk=n_blk, n_slc=n_slc, n_sel=min(N_SELECT, n_slc))
    return pl.pallas_call(
        body,
        grid=(batch, N_NSA_KV, seq // tq),
        in_specs=[pl.BlockSpec((1, tq, gw), lambda b, g, i: (b, i, g))]
        + _p_specs(n_chunks, lambda b, g, i: (b, g))
        + [pl.BlockSpec((NSA_REP, tq, n_chunks), lambda b, g, i: (g, i, 0))],
        out_specs=[pl.BlockSpec((1, tq, gw), lambda b, g, i: (b, i, g)),
                   pl.BlockSpec((1, 1, LANES, tq), lambda b, g, i: (b, g, 0, i))],
        out_shape=[jax.ShapeDtypeStruct((batch, seq, NSA_WIDTH), F32),
                   jax.ShapeDtypeStruct((batch, N_NSA_KV, LANES, seq), F32)],
        compiler_params=_params(("arbitrary",) * 3),
        name="cmp_attn_prompt",
    )(q3, pchunks, pchunks, pchunks, pchunks, bias_c)


_TK = 256


def _flash_step_t(xs, consts, vt, m_s, l_s, acc_s, tq):
    ps, alphas = [], []
    for r, x in enumerate(xs):
        cols = slice(r * tq, (r + 1) * tq)
        m_old = m_s[:, cols]
        m_new = jnp.maximum(m_old, jnp.max(x, axis=0, keepdims=True) + consts[r])
        p = jnp.exp(x - (m_new - consts[r]))
        alpha = jnp.exp(m_old - m_new)
        l_s[:, cols] = alpha * l_s[:, cols] + jnp.sum(p, axis=0, keepdims=True)
        m_s[:, cols] = m_new
        ps.append(p.astype(BF16))
        alphas.append(alpha)
    pv = jnp.dot(vt, jnp.concatenate(ps, axis=1), preferred_element_type=F32)
    acc_s[...] = jnp.concatenate(alphas, axis=1) * acc_s[...] + pv


def _nsa_flash_body(table_ref, q_ref, k_ref, v_ref, tz_ref, *rest, mode, tq, seq):
    if mode == "slc":
        sel_ref, o_ref, m_s, l_s, acc_s, msk_s = rest
    else:
        o_ref, m_s, l_s, acc_s = rest
    g = pl.program_id(1)
    i = pl.program_id(2)
    qb = jnp.concatenate([q_ref[0, :, r * HEAD_DIM:(r + 1) * HEAD_DIM] * ATTN_SCALE for r in range(NSA_REP)],
                         axis=0).astype(BF16)
    _init_state(m_s, l_s, acc_s)
    if mode == "slc":
        selb = sel_ref[0, 0].astype(BF16)
        kb = _iota((_TK, LANES), 0)
        sb = _iota((_TK, LANES), 1)
        for jj in range(seq // _TK):
            expand = (sb == _div_pow2(jj * _TK + kb, SLC_BLOCK)).astype(BF16)
            hit = jnp.dot(expand, selb, preferred_element_type=F32)
            msk_s[jj] = (hit - 1.0) * (-NEG_INF)
    kj = _iota((_TK, tq), 0)
    ti = _iota((_TK, tq), 1)
    causal = jnp.where(ti >= kj, 0.0, NEG_INF)
    far_consts = [table_ref[N_BUCKETS - 1, g * NSA_REP + r] for r in range(NSA_REP)]

    def logits(j, kind):
        start = pl.multiple_of(j * _TK, _TK)
        kt = k_ref[0, pl.ds(start, _TK), :].astype(BF16)
        vt = v_ref[0, pl.ds(start, _TK), :].T.astype(BF16)
        x = _dot_nt(kt, qb)
        add = None
        if mode == "slc":
            add = msk_s[j]
        elif kind == "far":
            add = jnp.where((i - j) * _TK + ti - kj <= WINDOW, 0.0, NEG_INF)
        if kind == "diag":
            add = causal if add is None else add + causal
        xs = []
        for r in range(NSA_REP):
            xr = x[:, r * tq:(r + 1) * tq]
            if kind == "prev":
                xr = xr + tz_ref[r, :, tq:2 * tq]
            elif kind == "diag":
                xr = xr + tz_ref[r, :, 0:tq]
            if add is not None:
                xr = xr + add
            xs.append(xr)
        return xs, (far_consts if kind == "far" else [0.0] * NSA_REP), vt

    def run(blocks):
        for xs, consts, vt in [logits(j, kind) for j, kind in blocks]:
            _flash_step_t(xs, consts, vt, m_s, l_s, acc_s, tq)

    lo = 0 if mode == "slc" else jnp.maximum(i - WINDOW // _TK, 0)
    n_far = jnp.maximum(i - 1 - lo, 0)

    def far_pair(p, carry):
        run([(lo + 2 * p, "far"), (lo + 2 * p + 1, "far")])
        return carry
    lax.fori_loop(0, jnp.right_shift(n_far, 1), far_pair, 0)

    @pl.when(jnp.bitwise_and(n_far, 1) == 1)
    def _():
        run([(lo + n_far - 1, "far")])

    @pl.when(i >= 1)
    def _():
        run([(i - 1, "prev"), (i, "diag")])

    @pl.when(i == 0)
    def _():
        run([(i, "diag")])
    o_t = acc_s[...] / l_s[...]
    for r in range(NSA_REP):
        o_ref[0, :, r * HEAD_DIM:(r + 1) * HEAD_DIM] = o_t[:, r * tq:(r + 1) * tq].T


def _nsa_flash_prompt(mode, table, q, kv, tz, sel, batch, seq):
    tq = _TK
    assert _TK >= MAX_DISTANCE and WINDOW % _TK == 0 and seq % _TK == 0
    gw = NSA_REP * HEAD_DIM
    q3 = q.reshape(batch, seq, NSA_WIDTH)
    kv3 = kv.reshape(batch, seq, NSA_KV_WIDTH)
    cols = NSA_REP * tq
    in_specs = [pl.BlockSpec(memory_space=pltpu.SMEM),
                pl.BlockSpec((1, tq, gw), lambda b, g, i: (b, i, g)),
                pl.BlockSpec((1, seq, HEAD_DIM), lambda b, g, i: (b, 0, g)),
                pl.BlockSpec((1, seq, HEAD_DIM), lambda b, g, i: (b, 0, N_NSA_KV + g)),
                pl.BlockSpec((NSA_REP, _TK, 2 * tq), lambda b, g, i: (g, 0, 0))]
    args = [table, q3, kv3, kv3, tz]
    scratch = [pltpu.VMEM((1, cols), F32), pltpu.VMEM((1, cols), F32), pltpu.VMEM((HEAD_DIM, cols), F32)]
    if mode == "slc":
        in_specs.append(pl.BlockSpec((1, 1, LANES, tq), lambda b, g, i: (b, g, 0, i)))
        args.append(sel)
        scratch.append(pltpu.VMEM((seq // _TK, _TK, tq), F32))
    return pl.pallas_call(
        functools.partial(_nsa_flash_body, mode=mode, tq=tq, seq=seq),
        grid=(batch, N_NSA_KV, seq // tq),
        in_specs=in_specs,
        out_specs=pl.BlockSpec((1, tq, gw), lambda b, g, i: (b, i, g)),
        out_shape=jax.ShapeDtypeStruct((batch, seq, NSA_WIDTH), F32),
        scratch_shapes=scratch,
        compiler_params=_params(("arbitrary",) * 3),
        name="nsa_flash_" + mode,
    )(*args)


def _cumsum_body(sm_ref, c_ref, ct_ref):
    seq = sm_ref.shape[1]
    lane = _iota((LANES, LANES), 1)
    tri = (_iota((LANES, LANES), 0) >= lane).astype(BF16)
    keep = (lane >= LOGF_LANE0) & (lane < LOGF_LANE0 + N_FOX_HEADS)
    carry = jnp.zeros((1, LANES), F32)
    for blk in range(seq // LANES):
        x = jnp.where(keep, sm_ref[0, blk * LANES:(blk + 1) * LANES, :], 0.0)
        cb = _dot01_left(tri, x) + carry
        c_ref[0, blk * LANES:(blk + 1) * LANES, :] = cb
        ct_ref[0, :, blk * LANES:(blk + 1) * LANES] = cb.T
        carry = cb[LANES - 1:LANES, :]


def _cumsum_prompt(sm, batch, seq):
    sm3 = sm.reshape(batch, seq, LANES)
    return pl.pallas_call(
        _cumsum_body,
        grid=(batch,),
        in_specs=[pl.BlockSpec((1, seq, LANES), lambda b: (b, 0, 0))],
        out_specs=[pl.BlockSpec((1, seq, LANES), lambda b: (b, 0, 0)),
                   pl.BlockSpec((1, LANES, seq), lambda b: (b, 0, 0))],
        out_shape=[jax.ShapeDtypeStruct((batch, seq, LANES), F32),
                   jax.ShapeDtypeStruct((batch, LANES, seq), F32)],
        compiler_params=_params(("arbitrary",)),
        name="logf_cumsum",
    )(sm3)


_FOX_T = 512


def _fox_flash_body(q_ref, k_ref, v_ref, c_ref, crow_ref, o_ref, m_s, l_s, acc_s, *, tq):
    h = pl.program_id(1)
    i = pl.program_id(2)
    qb = (q_ref[0] * ATTN_SCALE).astype(BF16)
    _init_state(m_s, l_s, acc_s)
    c_q = crow_ref[0, 0, pl.ds(i, 1), :]
    head_lane = _iota((tq, LANES), 1) == LOGF_LANE0 + h
    causal = jnp.where(_iota((tq, tq), 1) >= _iota((tq, tq), 0), 0.0, NEG_INF)

    def logits(j, diag):
        start = pl.multiple_of(j * tq, tq)
        kt = k_ref[0, pl.ds(start, tq), :].astype(BF16)
        vt = v_ref[0, pl.ds(start, tq), :].T.astype(BF16)
        c_k = jnp.sum(jnp.where(head_lane, c_ref[0, pl.ds(start, tq), :], 0.0), axis=-1, keepdims=True)
        x = _dot_nt(kt, qb) - c_k
        if diag:
            x = x + causal
        return x, vt

    def step(x, vt):
        m_old = m_s[...]
        m_new = jnp.maximum(m_old, jnp.max(x, axis=0, keepdims=True) + c_q)
        p = jnp.exp(x - (m_new - c_q))
        alpha = jnp.exp(m_old - m_new)
        l_s[...] = alpha * l_s[...] + jnp.sum(p, axis=0, keepdims=True)
        acc_s[...] = alpha * acc_s[...] + jnp.dot(vt, p.astype(BF16), preferred_element_type=F32)
        m_s[...] = m_new

    def run(blocks):
        for x, vt in [logits(j, diag) for j, diag in blocks]:
            step(x, vt)

    def far_pair(p, carry):
        run([(2 * p, False), (2 * p + 1, False)])
        return carry
    lax.fori_loop(0, jnp.right_shift(i, 1), far_pair, 0)

    @pl.when(jnp.bitwise_and(i, 1) == 1)
    def _():
        run([(i - 1, False), (i, True)])

    @pl.when(jnp.bitwise_and(i, 1) == 0)
    def _():
        run([(i, True)])
    o_ref[0] = (acc_s[...] / l_s[...]).T


def _fox_flash_prompt(qf, kvf, c, c_rows, batch, seq):
    tq = _FOX_T
    assert seq % tq == 0
    q3 = qf.reshape(batch, seq, FOX_WIDTH)
    kv3 = kvf.reshape(batch, seq, 2 * FOX_WIDTH)
    return pl.pallas_call(
        functools.partial(_fox_flash_body, tq=tq),
        grid=(batch, N_FOX_HEADS, seq // tq),
        in_specs=[pl.BlockSpec((1, tq, HEAD_DIM), lambda b, h, i: (b, i, h)),
                  pl.BlockSpec((1, seq, HEAD_DIM), lambda b, h, i: (b, 0, h)),
                  pl.BlockSpec((1, seq, HEAD_DIM), lambda b, h, i: (b, 0, N_FOX_HEADS + h)),
                  pl.BlockSpec((1, seq, LANES), lambda b, h, i: (b, 0, 0)),
                  pl.BlockSpec((1, 1, seq // tq, tq), lambda b, h, i: (b, h, 0, 0))],
        out_specs=pl.BlockSpec((1, tq, HEAD_DIM), lambda b, h, i: (b, i, h)),
        out_shape=jax.ShapeDtypeStruct((batch, seq, FOX_WIDTH), F32),
        scratch_shapes=[pltpu.VMEM((1, tq), F32), pltpu.VMEM((1, tq), F32), pltpu.VMEM((HEAD_DIM, tq), F32)],
        compiler_params=_params(("arbitrary",) * 3),
        name="fox_flash_prompt",
    )(q3, kv3, kv3, c, c_rows)


def _rms(x, g):
    return x * lax.rsqrt(jnp.mean(x * x, axis=-1, keepdims=True) + RMS_EPS) * g


def _outproj_body(x_ref, oc_ref, os_ref, ow_ref, of_ref, sm_ref, gn_ref, gf_ref, wo_ref, h_ref):
    gates = sm_ref[...]
    parts = []
    for hh in range(N_NSA_HEADS):
        sl = slice(hh * HEAD_DIM, (hh + 1) * HEAD_DIM)
        parts.append(gates[:, hh:hh + 1] * oc_ref[:, sl]
                     + gates[:, N_NSA_HEADS + hh:N_NSA_HEADS + hh + 1] * os_ref[:, sl]
                     + gates[:, 2 * N_NSA_HEADS + hh:2 * N_NSA_HEADS + hh + 1] * ow_ref[:, sl])
    o_nsa = jnp.concatenate(parts, axis=1)
    mixed = jnp.concatenate([_rms(o_nsa, gn_ref[...]), _rms(of_ref[...], gf_ref[...])], axis=1).astype(BF16)
    h_ref[...] = x_ref[...] + jnp.dot(mixed, wo_ref[...], preferred_element_type=F32)


def _outproj(x2d, o_c, o_s, o_w, o_f, sm, g_nsa, g_fox, wo):
    n, d = x2d.shape
    tm = min(n, 256)
    row = lambda w: pl.BlockSpec((tm, w), lambda i: (i, 0))
    const = lambda shape: pl.BlockSpec(shape, lambda i: (0, 0), pipeline_mode=pl.Buffered(1))
    return pl.pallas_call(
        _outproj_body,
        grid=(n // tm,),
        in_specs=[row(d), row(NSA_WIDTH), row(NSA_WIDTH), row(NSA_WIDTH), row(FOX_WIDTH), row(LANES),
                  const((1, NSA_WIDTH)), const((1, FOX_WIDTH)), const(wo.shape)],
        out_specs=row(d),
        out_shape=jax.ShapeDtypeStruct((n, d), F32),
        compiler_params=_params(("arbitrary",)),
        name="out_proj",
    )(x2d, o_c, o_s, o_w, o_f, sm, g_nsa.reshape(1, -1), g_fox.reshape(1, -1), wo)


def _mlp_body(h_ref, gm_ref, gfin_ref, wu_ref, wd_ref, y_ref, xn_s, acc_s):
    j = pl.program_id(1)

    @pl.when(j == 0)
    def _():
        xn_s[...] = _rms(h_ref[...], gm_ref[...]).astype(BF16)
        acc_s[...] = jnp.zeros(acc_s.shape, F32)
    u = jnp.maximum(jnp.dot(xn_s[...], wu_ref[...], preferred_element_type=F32), 0.0)
    acc_s[...] += jnp.dot((u * u).astype(BF16), wd_ref[...], preferred_element_type=F32)

    @pl.when(j == pl.num_programs(1) - 1)
    def _():
        y_ref[...] = _rms(h_ref[...] + acc_s[...], gfin_ref[...])


def _mlp(h, g_mlp, g_final, wu, wd):
    n, d = h.shape
    dff = wu.shape[1]
    tm = min(n, 512)
    tf = 1024
    return pl.pallas_call(
        _mlp_body,
        grid=(n // tm, dff // tf),
        in_specs=[pl.BlockSpec((tm, d), lambda i, j: (i, 0)),
                  pl.BlockSpec((1, d), lambda i, j: (0, 0)),
                  pl.BlockSpec((1, d), lambda i, j: (0, 0)),
                  pl.BlockSpec((d, tf), lambda i, j: (0, j)),
                  pl.BlockSpec((tf, d), lambda i, j: (j, 0))],
        out_specs=pl.BlockSpec((tm, d), lambda i, j: (i, 0)),
        out_shape=jax.ShapeDtypeStruct((n, d), F32),
        scratch_shapes=[pltpu.VMEM((tm, d), BF16), pltpu.VMEM((tm, d), F32)],
        compiler_params=_params(("arbitrary", "arbitrary")),
        name="mlp_final",
    )(h, g_mlp.reshape(1, d), g_final.reshape(1, d), wu, wd)


def _cmp_attn_s_body(q_ref, p00, p10, p01, p11, bias_ref, oc_ref, sel_ref, *, past, n_blk, n_slc, n_sel):
    n_chunks = p00.shape[1]
    t_new = q_ref.shape[1]
    sl = sel_ref.shape[-1]
    k, v = _cmp_kv(p00, p10, p01, p11)
    qb = jnp.concatenate([q_ref[0, :, r * HEAD_DIM:(r + 1) * HEAD_DIM] for r in range(NSA_REP)],
                         axis=0).astype(BF16)
    bias = jnp.concatenate([bias_ref[r] for r in range(NSA_REP)], axis=0)
    rows = NSA_REP * t_new
    t = past + _mod_pow2(_iota((rows, n_chunks), 0), t_new)
    n = _iota((rows, n_chunks), 1)
    mask = (n * CMP_STRIDE + (CMP_BLOCK - 1) <= t) & (n < n_blk)
    s = jnp.where(mask, _dot_nt(qb, k) * ATTN_SCALE + bias, NEG_INF)
    e = jnp.where(mask, jnp.exp(s - jnp.max(s, axis=-1, keepdims=True)), 0.0)
    l = jnp.sum(e, axis=-1, keepdims=True)
    p = e / jnp.where(l > 0.0, l, 1.0)
    o = jnp.dot(p.astype(BF16), v, preferred_element_type=F32)
    psum = jnp.zeros((t_new, n_chunks), F32)
    for r in range(NSA_REP):
        oc_ref[0, :, r * HEAD_DIM:(r + 1) * HEAD_DIM] = o[r * t_new:(r + 1) * t_new]
        psum = psum + p[r * t_new:(r + 1) * t_new]
    nb = _iota((n_chunks, sl), 0)
    sb = _iota((n_chunks, sl), 1)
    cover = ((nb * CMP_STRIDE < sb * SLC_BLOCK + SLC_BLOCK) & (nb * CMP_STRIDE + CMP_BLOCK > sb * SLC_BLOCK))
    imp = _dot01_right(psum, cover.astype(BF16))
    blk = _iota((t_new, sl), 1)
    cur = _div_pow2(past + _iota((t_new, sl), 0), SLC_BLOCK)
    forced = (blk == 0) | (blk == cur) | (blk == cur - 1)
    score = jnp.where(forced, FORCE_SCORE, jnp.where(blk <= cur, imp, -FORCE_SCORE))
    score = jnp.where(blk < n_slc, score, -jnp.inf)
    sel = jnp.zeros((t_new, sl), F32)
    for _ in range(n_sel):
        mx = jnp.max(score, axis=-1, keepdims=True)
        first = jnp.min(jnp.where(score == mx, blk, sl), axis=-1, keepdims=True)
        hit = blk == first
        sel = jnp.where(hit, 1.0, sel)
        score = jnp.where(hit, -jnp.inf, score)
    sel_ref[0, 0] = sel


def _cmp_attn_sample(q, pchunks, bias_c, batch, t_new, past):
    n_chunks = pchunks.shape[1]
    n_blk = n_chunks - CMP_SLOTS + 1
    n_slc = -(-(past + t_new) // SLC_BLOCK)
    sl = -(-n_slc // LANES) * LANES
    gw = NSA_REP * HEAD_DIM
    q3 = q.reshape(batch, t_new, NSA_WIDTH)
    body = functools.partial(_cmp_attn_s_body, past=past, n_blk=n_blk, n_slc=n_slc, n_sel=min(N_SELECT, n_slc))
    return pl.pallas_call(
        body,
        grid=(batch, N_NSA_KV),
        in_specs=[pl.BlockSpec((1, t_new, gw), lambda b, g: (b, 0, g))]
        + _p_specs(n_chunks, lambda b, g: (b, g))
        + [pl.BlockSpec((NSA_REP, t_new, n_chunks), lambda b, g: (g, 0, 0))],
        out_specs=[pl.BlockSpec((1, t_new, gw), lambda b, g: (b, 0, g)),
                   pl.BlockSpec((1, 1, t_new, sl), lambda b, g: (b, g, 0, 0))],
        out_shape=[jax.ShapeDtypeStruct((batch, t_new, NSA_WIDTH), F32),
                   jax.ShapeDtypeStruct((batch, N_NSA_KV, t_new, sl), F32)],
        compiler_params=_params(("arbitrary",) * 2),
        name="cmp_attn_sample",
    )(q3, pchunks, pchunks, pchunks, pchunks, bias_c)


_GKV = N_NSA_KV * HEAD_DIM


def _dec_tile(qbd, kv, dist, mask, table_ref, full_bias, t_new, m_s, l_s, acc_s):
    s = _dot_nt(qbd, kv[:, :_GKV].astype(BF16)) * ATTN_SCALE
    parts = []
    for hh in range(N_NSA_HEADS):
        rows = slice(hh * t_new, (hh + 1) * t_new)
        if full_bias:
            b = _t5_lookup(_t5_bucket(dist[rows]), table_ref, hh)
        else:
            b = jnp.full((t_new, s.shape[1]), table_ref[N_BUCKETS - 1, hh], F32)
        parts.append(s[rows] + b)
    s = jnp.concatenate(parts, axis=0)
    _online_update(s, mask, kv[:, _GKV:].astype(BF16), m_s, l_s, acc_s)


def _dec_finish(o_ref, t_new, l_s, acc_s):
    o = acc_s[...] / l_s[...]
    for g in range(N_NSA_KV):
        for r in range(NSA_REP):
            hh = g * NSA_REP + r
            o_ref[0, :, hh * HEAD_DIM:(hh + 1) * HEAD_DIM] = o[hh * t_new:(hh + 1) * t_new,
                                                             g * HEAD_DIM:(g + 1) * HEAD_DIM]


def _row_mask(m2, t_new):
    return jnp.concatenate([m2[g * t_new:(g + 1) * t_new] for g in range(N_NSA_KV) for _ in range(NSA_REP)], axis=0)


def _slc_s_body(pt_ref, table_ref, q_ref, *rest, n_pg, past, t_new):
    page_refs = rest[:n_pg]
    selp_ref, knew_ref, vnew_ref, selnew_ref, o_ref, m_s, l_s, acc_s = rest[n_pg:]
    j = pl.program_id(1)
    last = pl.num_programs(1) - 1
    grp_rows = NSA_REP * t_new
    q = (q_ref[0] * ATTN_SCALE).astype(BF16)
    t_pos = past + _iota((t_new, PAGE_SIZE), 0)
    lane = _iota((t_new, PAGE_SIZE), 1)

    @pl.when(j == 0)
    def _():
        _init_state(m_s, l_s, acc_s)

    def update(x, mask, vs):
        x = jnp.where(mask, x, NEG_INF)
        m_old = m_s[...]
        m_new = jnp.maximum(m_old, jnp.max(x, axis=-1, keepdims=True))
        p = jnp.where(mask, jnp.exp(x - m_new), 0.0)
        alpha = jnp.exp(m_old - m_new)
        l_s[...] = alpha * l_s[...] + jnp.sum(p, axis=-1, keepdims=True)
        pb = p.astype(BF16)
        pv = jnp.concatenate([jnp.dot(pb[g * grp_rows:(g + 1) * grp_rows], vs[g], preferred_element_type=F32)
                              for g in range(N_NSA_KV)], axis=0)
        acc_s[...] = alpha * acc_s[...] + pv
        m_s[...] = m_new

    def head_bias(hh, dist):
        return _t5_lookup(_t5_bucket(dist), table_ref, hh)

    dist_last = t_pos - ((j * n_pg + n_pg - 1) * PAGE_SIZE + lane)
    xs, masks, vs = [], [], []
    for g in range(N_NSA_KV):
        plane = lambda c: jnp.concatenate(
            [pr[0, pl.ds(c * N_NSA_KV + g, PAGE_SIZE, stride=2 * N_NSA_KV), :] for pr in page_refs],
            axis=0).astype(BF16)
        x = _dot_nt(q[g * grp_rows:(g + 1) * grp_rows], plane(0))
        vs.append(plane(1))
        for r in range(NSA_REP):
            hh = g * NSA_REP + r
            far = jnp.full((t_new, (n_pg - 1) * PAGE_SIZE), table_ref[N_BUCKETS - 1, hh], F32)
            bias = jnp.concatenate([far, head_bias(hh, dist_last)], axis=1)
            xs.append(x[r * t_new:(r + 1) * t_new] + bias)
        picked = []
        for k in range(n_pg):
            m2 = selp_ref[0, k][g * t_new:(g + 1) * t_new]
            picked.append(jnp.where(lane < SLC_BLOCK, m2[:, 0:1], m2[:, 1:2]))
        masks.extend([jnp.concatenate(picked, axis=1)] * NSA_REP)
    update(jnp.concatenate(xs, axis=0), jnp.concatenate(masks, axis=0) > 0.5, vs)

    @pl.when(j == last)
    def _():
        dist = t_pos - (past + lane)
        xs, masks, vs = [], [], []
        for g in range(N_NSA_KV):
            x = _dot_nt(q[g * grp_rows:(g + 1) * grp_rows], knew_ref[0, g].astype(BF16))
            vs.append(vnew_ref[0, g].astype(BF16))
            for r in range(NSA_REP):
                xs.append(x[r * t_new:(r + 1) * t_new] + head_bias(g * NSA_REP + r, dist))
            sel_g = jnp.broadcast_to(selnew_ref[0][g * t_new:(g + 1) * t_new, 0:1], (t_new, PAGE_SIZE)) > 0.5
            masks.extend([sel_g & (dist >= 0) & (lane < t_new)] * NSA_REP)
        update(jnp.concatenate(xs, axis=0), jnp.concatenate(masks, axis=0), vs)
        o = acc_s[...] / l_s[...]
        for hh in range(N_NSA_HEADS):
            o_ref[0, :, hh * HEAD_DIM:(hh + 1) * HEAD_DIM] = o[hh * t_new:(hh + 1) * t_new]


def _slc_sample(table, page_table, q_rows, cache, selp, knew, vnew, selnew, past, t_new):
    batch, n_pages = page_table.shape
    n_pg = math.gcd(n_pages, 8)
    n_phys = cache.shape[0]
    assert t_new <= PAGE_SIZE and past % PAGE_SIZE == 0 and PAGE_SIZE >= MAX_DISTANCE
    page_rows = PAGE_SIZE * 2 * N_NSA_KV
    x = cache.reshape(n_phys, page_rows, HEAD_DIM)
    rows = N_NSA_HEADS * t_new

    def page_spec(k):
        return pl.BlockSpec((1, page_rows, HEAD_DIM), lambda b, j, pt: (pt[b * n_pages + j * n_pg + k], 0, 0))
    new_spec = pl.BlockSpec((1, N_NSA_KV, PAGE_SIZE, HEAD_DIM), lambda b, j, pt: (b, 0, 0, 0))
    grid_spec = pltpu.PrefetchScalarGridSpec(
        num_scalar_prefetch=1,
        grid=(batch, n_pages // n_pg),
        in_specs=[pl.BlockSpec(memory_space=pltpu.SMEM),
                  pl.BlockSpec((1, rows, HEAD_DIM), lambda b, j, pt: (b, 0, 0))]
        + [page_spec(k) for k in range(n_pg)]
        + [pl.BlockSpec((1, n_pg, N_NSA_KV * t_new, 2), lambda b, j, pt: (b, j, 0, 0)),
           new_spec, new_spec,
           pl.BlockSpec((1, N_NSA_KV * t_new, 2), lambda b, j, pt: (b, 0, 0))],
        out_specs=pl.BlockSpec((1, t_new, NSA_WIDTH), lambda b, j, pt: (b, 0, 0)),
        scratch_shapes=[pltpu.VMEM((rows, 1), F32), pltpu.VMEM((rows, 1), F32), pltpu.VMEM((rows, HEAD_DIM), F32)],
    )
    return pl.pallas_call(
        functools.partial(_slc_s_body, n_pg=n_pg, past=past, t_new=t_new),
        grid_spec=grid_spec,
        out_shape=jax.ShapeDtypeStruct((batch, t_new, NSA_WIDTH), F32),
        compiler_params=_params(("arbitrary", "arbitrary")),
        name="slc_sample",
    )(page_table.reshape(-1), table, q_rows, *([x] * n_pg), selp, knew, vnew, selnew)


def _win_s_body(table_ref, q_ref, kv_ref, o_ref, m_s, l_s, acc_s, *, past, t_new, wb):
    rows = N_NSA_HEADS * t_new
    qbd = q_ref[0]
    t_pos = past + _mod_pow2(_iota((rows, PAGE_SIZE), 0), t_new)
    lane = _iota((rows, PAGE_SIZE), 1)
    _init_state(m_s, l_s, acc_s)
    for k in range(kv_ref.shape[1] // PAGE_SIZE):
        k_pos = past - wb + k * PAGE_SIZE + lane
        dist = t_pos - k_pos
        mask = (dist >= 0) & (dist <= WINDOW) & (k_pos >= 0) & (k * PAGE_SIZE + lane < wb + t_new)
        _dec_tile(qbd, kv_ref[0, k * PAGE_SIZE:(k + 1) * PAGE_SIZE, :], dist, mask, table_ref, True, t_new,
                  m_s, l_s, acc_s)
    _dec_finish(o_ref, t_new, l_s, acc_s)


def _win_sample(table, qbd, win_all_padded, past, t_new, wb):
    batch, n_keys, _ = win_all_padded.shape
    rows = N_NSA_HEADS * t_new
    return pl.pallas_call(
        functools.partial(_win_s_body, past=past, t_new=t_new, wb=wb),
        grid=(batch,),
        in_specs=[pl.BlockSpec(memory_space=pltpu.SMEM),
                  pl.BlockSpec((1, rows, _GKV), lambda b: (b, 0, 0)),
                  pl.BlockSpec((1, n_keys, 2 * _GKV), lambda b: (b, 0, 0))],
        out_specs=pl.BlockSpec((1, t_new, NSA_WIDTH), lambda b: (b, 0, 0)),
        out_shape=jax.ShapeDtypeStruct((batch, t_new, NSA_WIDTH), F32),
        scratch_shapes=[pltpu.VMEM((rows, 1), F32), pltpu.VMEM((rows, 1), F32), pltpu.VMEM((rows, _GKV), F32)],
        compiler_params=_params(("arbitrary",)),
        name="win_sample",
    )(table, qbd, win_all_padded)


_FOX_PAGE_LANES = PAGE_SIZE * N_FOX_HEADS


def _fox_s_body(pt_ref, q_ref, *rest, n_pg, t_new):
    kv_refs = rest[:n_pg]
    lf_refs = rest[n_pg:2 * n_pg]
    tri_ref, knew_ref, vnew_ref, lfn_ref, o_ref, m_s, l_s, acc_s, off_s = rest[2 * n_pg:]
    j = pl.program_id(1)
    rows = N_FOX_HEADS * t_new
    pw = _FOX_PAGE_LANES
    qb = (q_ref[0] * ATTN_SCALE).astype(BF16)
    lane = _iota((rows, PAGE_SIZE), 1)
    t_row = _mod_pow2(_iota((rows, PAGE_SIZE), 0), t_new)
    h_row = _div_pow2(_iota((rows, PAGE_SIZE), 0), t_new)
    lfn = lfn_ref[0]
    c_new = jnp.sum(jnp.where(lane <= t_row, lfn, 0.0), axis=-1, keepdims=True)

    @pl.when(j == 0)
    def _():
        _init_state(m_s, l_s, acc_s)
        off_s[...] = jnp.zeros(off_s.shape, F32)
        u_l = _div_pow2(lane, N_FOX_HEADS)
        bias = jnp.zeros((rows, PAGE_SIZE), F32)
        for u in range(t_new):
            col = jnp.sum(jnp.where((lane > u) & (lane <= t_row), lfn, 0.0), axis=-1, keepdims=True)
            bias = jnp.where(u_l == u, col, bias)
        x = _dot_nt(qb, knew_ref[0].astype(BF16)) + bias
        mask = (u_l <= t_row) & (u_l < t_new) & (_mod_pow2(lane, N_FOX_HEADS) == h_row)
        _online_update(x, mask, vnew_ref[0].astype(BF16), m_s, l_s, acc_s)

    lf = jnp.concatenate([r[0] for r in lf_refs], axis=0)
    later = _dot01_right(lf, tri_ref[...])
    tot = jnp.sum(lf, axis=-1, keepdims=True)
    head_lane = jnp.where(
        _mod_pow2(_iota((rows, pw), 1), N_FOX_HEADS) == _div_pow2(_iota((rows, pw), 0), t_new), 0.0, NEG_INF)
    spread = lambda a: jnp.concatenate(
        [jnp.broadcast_to(a[h:h + 1], (t_new, a.shape[1])) for h in range(N_FOX_HEADS)], axis=0)
    off = off_s[:, 0:1]
    xs, vs = [], []
    for k in range(n_pg):
        kk = kv_refs[k][0, :, 0].reshape(pw, HEAD_DIM).astype(BF16)
        vs.append(kv_refs[k][0, :, 1].reshape(pw, HEAD_DIM).astype(BF16))
        hs = slice(k * N_FOX_HEADS, (k + 1) * N_FOX_HEADS)
        xs.append(_dot_nt(qb, kk) + spread(later[hs]) + (spread(off) + c_new) + head_lane)
        off = off + tot[hs]
    off_s[...] = jnp.broadcast_to(off, off_s.shape)
    x = jnp.concatenate(xs, axis=1)
    m_old = m_s[...]
    m_new = jnp.maximum(m_old, jnp.max(x, axis=-1, keepdims=True))
    p = jnp.exp(x - m_new)
    alpha = jnp.exp(m_old - m_new)
    l_s[...] = alpha * l_s[...] + jnp.sum(p, axis=-1, keepdims=True)
    acc_s[...] = alpha * acc_s[...] + jnp.dot(p.astype(BF16), jnp.concatenate(vs, axis=0),
                                              preferred_element_type=F32)
    m_s[...] = m_new

    @pl.when(j == pl.num_programs(1) - 1)
    def _():
        o = acc_s[...] / l_s[...]
        for h in range(N_FOX_HEADS):
            o_ref[0, :, h * HEAD_DIM:(h + 1) * HEAD_DIM] = o[h * t_new:(h + 1) * t_new]


def _fox_sample(page_table, q_rows, cache, logf_t, tri, knew, vnew, lfn, t_new):
    batch, n_pages = page_table.shape
    n_pg = math.gcd(n_pages, 8)
    rows = N_FOX_HEADS * t_new
    page = lambda k: (lambda b, j, pt: pt[b * n_pages + n_pages - 1 - (j * n_pg + k)])

    def kv_spec(k):
        return pl.BlockSpec((1, PAGE_SIZE, 2, N_FOX_HEADS, HEAD_DIM), lambda b, j, pt: (page(k)(b, j, pt), 0, 0, 0, 0))

    def lf_spec(k):
        return pl.BlockSpec((1, N_FOX_HEADS, PAGE_SIZE), lambda b, j, pt: (page(k)(b, j, pt), 0, 0))
    new_spec = pl.BlockSpec((1, PAGE_SIZE, HEAD_DIM), lambda b, j, pt: (b, 0, 0))
    grid_spec = pltpu.PrefetchScalarGridSpec(
        num_scalar_prefetch=1,
        grid=(batch, n_pages // n_pg),
        in_specs=[pl.BlockSpec((1, rows, HEAD_DIM), lambda b, j, pt: (b, 0, 0))]
        + [kv_spec(k) for k in range(n_pg)] + [lf_spec(k) for k in range(n_pg)]
        + [pl.BlockSpec(tri.shape, lambda b, j, pt: (0, 0)), new_spec, new_spec,
           pl.BlockSpec((1, rows, PAGE_SIZE), lambda b, j, pt: (b, 0, 0))],
        out_specs=pl.BlockSpec((1, t_new, FOX_WIDTH), lambda b, j, pt: (b, 0, 0)),
        scratch_shapes=[pltpu.VMEM((rows, 1), F32), pltpu.VMEM((rows, 1), F32), pltpu.VMEM((rows, HEAD_DIM), F32),
                        pltpu.VMEM((N_FOX_HEADS, LANES), F32)],
    )
    return pl.pallas_call(
        functools.partial(_fox_s_body, n_pg=n_pg, t_new=t_new),
        grid_spec=grid_spec,
        out_shape=jax.ShapeDtypeStruct((batch, t_new, FOX_WIDTH), F32),
        compiler_params=_params(("arbitrary", "arbitrary")),
        name="fox_sample",
    )(page_table.reshape(-1), q_rows, *([cache] * n_pg), *([logf_t] * n_pg), tri, knew, vnew, lfn)


def _pack_weights(w_in, b_f, w_cmp, pe_cmp, w_o, w_up, w_down):
    c_gt = NSA_WIDTH + 3 * NSA_KV_WIDTH
    c_qf = c_gt + GATE_LANES
    c_fl = c_qf + 3 * FOX_WIDTH
    wm = jnp.concatenate([w_in[:, :c_gt], w_in[:, c_qf:c_fl]], axis=1).astype(BF16)
    ws = jnp.concatenate([w_in[:, c_gt:c_qf], w_in[:, c_fl:],
                          jnp.zeros((w_in.shape[0], LANES - GATE_LANES - N_FOX_HEADS), F32)], axis=1).astype(BF16)
    bf128 = jnp.zeros((1, LANES), F32).at[0, LOGF_LANE0:LOGF_LANE0 + N_FOX_HEADS].set(b_f)
    wc = w_cmp.reshape(2, CMP_SLOTS, CMP_STRIDE, HEAD_DIM, HEAD_DIM).transpose(0, 2, 3, 1, 4).reshape(
        2, _CHUNK_K, CMP_SLOTS * HEAD_DIM).astype(BF16)
    pe8 = jnp.zeros((2, SUBLANES, _CHUNK_K), F32).at[:, :CMP_SLOTS].set(pe_cmp.reshape(2, CMP_SLOTS, _CHUNK_K))
    return wm, ws, bf128, wc, pe8, w_o.astype(BF16), w_up.astype(BF16), w_down.astype(BF16)


def _block_diag_q(q, t_new, n_grp, per_grp):
    batch = q.shape[0]
    q5 = q.reshape(batch, t_new, n_grp, per_grp, HEAD_DIM).transpose(0, 2, 3, 1, 4)
    eye = jnp.eye(n_grp, dtype=q.dtype)
    out = q5[:, :, :, :, None, :] * eye[None, :, None, None, :, None]
    return out.reshape(batch, n_grp * per_grp * t_new, n_grp * HEAD_DIM).astype(BF16)


def _prompt_pass(x, table, lw, packed):
    g_attn, g_nsa_out, g_fox_out, g_mlp, g_final = lw
    wm, ws, bf128, wc, pe8, wo, wu, wd = packed
    batch, seq, d = x.shape
    assert seq % 256 == 0 and seq >= WINDOW
    x2d = x.reshape(batch * seq, d)
    q, ckv, skv, wkv, qf, kvf, sm = _project(x2d, g_attn, wm, ws, bf128)
    pchunks = _chunkproj_prompt(ckv, batch, pe8, wc)
    n_chunks = seq // CMP_STRIDE
    dist_c = jnp.arange(seq, dtype=I32)[:, None] - (jnp.arange(n_chunks, dtype=I32)[None, :] * CMP_STRIDE
                                                     + (CMP_BLOCK - 1))
    bias_c = _t5_bias(table, dist_c)
    o_c, sel = _cmp_attn_prompt(q, pchunks, bias_c, batch, seq)
    dist_tz = jnp.arange(2 * _TK, dtype=I32)[None, :] - jnp.arange(_TK, dtype=I32)[:, None]
    tz = _t5_bias(table, dist_tz)
    o_s = _nsa_flash_prompt("slc", table, q, skv, tz, sel, batch, seq)
    o_w = _nsa_flash_prompt("win", table, q, wkv, tz, None, batch, seq)
    c, c_t = _cumsum_prompt(sm, batch, seq)
    c_rows = c_t[:, LOGF_LANE0:LOGF_LANE0 + N_FOX_HEADS, :].reshape(batch, N_FOX_HEADS, seq // _FOX_T, _FOX_T)
    o_f = _fox_flash_prompt(qf, kvf, c, c_rows, batch, seq)
    h = _outproj(x2d, o_c.reshape(-1, NSA_WIDTH), o_s.reshape(-1, NSA_WIDTH), o_w.reshape(-1, NSA_WIDTH),
                 o_f.reshape(-1, FOX_WIDTH), sm, g_nsa_out, g_fox_out, wo)
    y = _mlp(h, g_mlp, g_final, wu, wd).reshape(batch, seq, d)
    kv_shape = (1, batch, seq, 2, N_NSA_KV, HEAD_DIM)
    wkv5 = wkv.reshape(kv_shape)
    logf = sm.reshape(batch, seq, LANES)[:, :, LOGF_LANE0:LOGF_LANE0 + N_FOX_HEADS]
    return (y, ckv.reshape(kv_shape), skv.reshape(kv_shape),
            kvf.reshape(1, batch, seq, 2, N_FOX_HEADS, HEAD_DIM), logf[None],
            wkv5[:, :, seq - min(WINDOW, seq):])


def _sample_pass(x, cache_cmp, cache_slc, cache_fox, cache_logf, win_buf, page_table, table, lw, packed):
    g_attn, g_nsa_out, g_fox_out, g_mlp, g_final = lw
    wm, ws, bf128, wc, pe8, wo, wu, wd = packed
    batch, t_new, d = x.shape
    n_pages = page_table.shape[1]
    past = n_pages * PAGE_SIZE
    assert t_new < CMP_STRIDE and t_new % SUBLANES == 0 and t_new * N_FOX_HEADS <= PAGE_SIZE
    x2d = x.reshape(batch * t_new, d)
    q, ckv, skv, wkv, qf, kvf, sm = _project(x2d, g_attn, wm, ws, bf128)
    pchunks = _chunkproj_sample(cache_cmp, page_table, pe8, wc)
    n_chunks = pchunks.shape[1]
    pos = past + jnp.arange(t_new, dtype=I32)
    dist_c = pos[:, None] - (jnp.arange(n_chunks, dtype=I32)[None, :] * CMP_STRIDE + (CMP_BLOCK - 1))
    bias_c = _t5_bias(table, dist_c)
    o_c, sel = _cmp_attn_sample(q, pchunks, bias_c, batch, t_new, past)
    q5 = q.reshape(batch, t_new, N_NSA_KV, NSA_REP, HEAD_DIM)
    q_rows = q5.transpose(0, 2, 3, 1, 4).reshape(batch, N_NSA_HEADS * t_new, HEAD_DIM)
    n_past_blk = past // SLC_BLOCK
    bpp = PAGE_SIZE // SLC_BLOCK
    selp = sel[..., :n_past_blk].reshape(batch, N_NSA_KV, t_new, n_pages, bpp).transpose(0, 3, 1, 2, 4).reshape(
        batch, n_pages, N_NSA_KV * t_new, bpp)
    selnew = jnp.broadcast_to(sel[..., n_past_blk:n_past_blk + 1].reshape(batch, N_NSA_KV * t_new, 1),
                              (batch, N_NSA_KV * t_new, bpp))
    skv5 = skv.reshape(batch, t_new, 2, N_NSA_KV, HEAD_DIM).transpose(2, 0, 3, 1, 4)
    skv5 = jnp.pad(skv5, ((0, 0), (0, 0), (0, 0), (0, PAGE_SIZE - t_new), (0, 0)))
    o_s = _slc_sample(table, page_table, q_rows, cache_slc, selp, skv5[0], skv5[1], selnew, past, t_new)
    qbd = _block_diag_q(q.reshape(batch, t_new, NSA_WIDTH), t_new, N_NSA_KV, NSA_REP)
    wb = win_buf.shape[1]
    win_all = jnp.concatenate([win_buf.reshape(batch, wb, NSA_KV_WIDTH), wkv.reshape(batch, t_new, NSA_KV_WIDTH)],
                              axis=1)
    n_keys = -(-(wb + t_new) // PAGE_SIZE) * PAGE_SIZE
    o_w = _win_sample(table, qbd, jnp.pad(win_all, ((0, 0), (0, n_keys - wb - t_new), (0, 0))), past, t_new, wb)
    logf_new = sm.reshape(batch, t_new, LANES)[:, :, LOGF_LANE0:LOGF_LANE0 + N_FOX_HEADS]
    logf_t = cache_logf.transpose(0, 2, 1)
    qf_rows = qf.reshape(batch, t_new, N_FOX_HEADS, HEAD_DIM).transpose(0, 2, 1, 3).reshape(
        batch, N_FOX_HEADS * t_new, HEAD_DIM)
    kvf5 = kvf.reshape(batch, t_new, 2, N_FOX_HEADS * HEAD_DIM).transpose(2, 0, 1, 3).reshape(
        2, batch, t_new * N_FOX_HEADS, HEAD_DIM)
    kvf5 = jnp.pad(kvf5, ((0, 0), (0, 0), (0, PAGE_SIZE - t_new * N_FOX_HEADS), (0, 0)))
    lfn = jnp.broadcast_to(logf_new.transpose(0, 2, 1)[:, :, None, :], (batch, N_FOX_HEADS, t_new, t_new))
    lfn = jnp.pad(lfn.reshape(batch, N_FOX_HEADS * t_new, t_new), ((0, 0), (0, 0), (0, PAGE_SIZE - t_new)))
    tri = (jnp.arange(PAGE_SIZE)[:, None] > jnp.arange(_FOX_PAGE_LANES)[None, :] // N_FOX_HEADS).astype(BF16)
    o_f = _fox_sample(page_table, qf_rows, cache_fox, logf_t, tri, kvf5[0], kvf5[1], lfn, t_new)
    h = _outproj(x2d, o_c.reshape(-1, NSA_WIDTH), o_s.reshape(-1, NSA_WIDTH), o_w.reshape(-1, NSA_WIDTH),
                 o_f.reshape(-1, FOX_WIDTH), sm, g_nsa_out, g_fox_out, wo)
    y = _mlp(h, g_mlp, g_final, wu, wd).reshape(batch, t_new, d)
    kv_shape = (1, batch, t_new, 2, N_NSA_KV, HEAD_DIM)
    return (y, ckv.reshape(kv_shape), skv.reshape(kv_shape),
            kvf.reshape(1, batch, t_new, 2, N_FOX_HEADS, HEAD_DIM), logf_new[None],
            win_all[:, t_new:].reshape(1, batch, wb, 2, N_NSA_KV, HEAD_DIM))


def kernel(x_prompt, x_sample, cache_cmp_kv, cache_slc_kv, cache_fox_kv, cache_fox_logf, state_win_kv, page_table,
           t5_table, g_attn, w_in, b_f, w_cmp, pe_cmp, g_nsa_out, g_fox_out, w_o, g_mlp, w_up, w_down, g_final):
    assert g_attn.shape[0] == 1, "single-layer step"
    assert x_prompt.shape[-1] == NSA_WIDTH + FOX_WIDTH
    packed = _pack_weights(w_in[0], b_f[0], w_cmp[0], pe_cmp[0], w_o[0], w_up[0], w_down[0])
    lw = (g_attn[0], g_nsa_out[0], g_fox_out[0], g_mlp[0], g_final)
    yp, cmp_p, slc_p, fox_p, logf_p, win_p = _prompt_pass(x_prompt, t5_table, lw, packed)
    ys, cmp_s, slc_s, fox_s, logf_s, win_s = _sample_pass(
        x_sample, cache_cmp_kv[0], cache_slc_kv[0], cache_fox_kv[0], cache_fox_logf[0], state_win_kv[0],
        page_table, t5_table, lw, packed)
    return (yp, ys, cmp_p, slc_p, fox_p, logf_p, win_p, cmp_s, slc_s, fox_s, logf_s, win_s)
```

```python
import functools
import math

import jax
import jax.numpy as jnp
from jax import lax
from jax.experimental import pallas as pl
from jax.experimental.pallas import tpu as pltpu

F32 = jnp.float32
BF16 = jnp.bfloat16
I32 = jnp.int32

HEAD_DIM = 128
N_NSA_HEADS = 8
N_NSA_KV = 2
NSA_REP = 4
N_FOX_HEADS = 8
NSA_WIDTH = N_NSA_HEADS * HEAD_DIM
FOX_WIDTH = N_FOX_HEADS * HEAD_DIM
NSA_KV_WIDTH = 2 * N_NSA_KV * HEAD_DIM
CMP_BLOCK = 32
CMP_STRIDE = 16
CMP_SLOTS = CMP_BLOCK // CMP_STRIDE
SLC_BLOCK = 64
N_SELECT = 16
WINDOW = 512
N_BUCKETS = 32
MAX_DISTANCE = 128
PAGE_SIZE = 128
RMS_EPS = 1e-6
NEG_INF = -1e30
FORCE_SCORE = 1e9
ATTN_SCALE = HEAD_DIM ** -0.5
LOG2E = 1.0 / math.log(2.0)

LANES = 128
SUBLANES = 8
VMEM_LIMIT = 56 * 1024 * 1024
GATE_LANES = 3 * N_NSA_HEADS
LOGF_LANE0 = GATE_LANES


def _params(sem):
    return pltpu.CompilerParams(dimension_semantics=sem, vmem_limit_bytes=VMEM_LIMIT)


def _iota(shape, dim):
    return lax.broadcasted_iota(I32, shape, dim)


def _log2(n):
    assert n > 0 and n & (n - 1) == 0
    return n.bit_length() - 1


def _div_pow2(x, n):
    return jnp.right_shift(x, _log2(n))


def _mod_pow2(x, n):
    return jnp.bitwise_and(x, n - 1)


def _dot_nt(a, b):
    return lax.dot_general(a, b, (((1,), (1,)), ((), ())), preferred_element_type=F32)


def _split3(x):
    hi = x.astype(BF16)
    r1 = x - hi.astype(F32)
    mid = r1.astype(BF16)
    lo = (r1 - mid.astype(F32)).astype(BF16)
    return hi, mid, lo


def _dot01_right(x, m01):
    hi, mid, lo = _split3(x)
    d = lambda a: jnp.dot(a, m01, preferred_element_type=F32)
    return d(hi) + d(mid) + d(lo)


def _dot01_left(m01, x):
    hi, mid, lo = _split3(x)
    d = lambda a: jnp.dot(m01, a, preferred_element_type=F32)
    return d(hi) + d(mid) + d(lo)


def _dot01_nt(m01, x):
    hi, mid, lo = _split3(x)
    return _dot_nt(m01, hi) + _dot_nt(m01, mid) + _dot_nt(m01, lo)


def _t5_bucket(dist):
    n = jnp.maximum(dist, 0)
    exact = N_BUCKETS // 2
    nf = jnp.maximum(n, 1).astype(F32)
    far = exact + (jnp.log(nf / exact) / math.log(MAX_DISTANCE / exact) * (N_BUCKETS - exact)).astype(I32)
    return jnp.where(n < exact, n, jnp.minimum(far, N_BUCKETS - 1))


def _t5_lookup(bucket, table_ref, head):
    acc = jnp.zeros(bucket.shape, F32)
    for k in range(N_BUCKETS):
        acc = jnp.where(bucket == k, table_ref[k, head], acc)
    return acc


def _online_update(s, mask, v, m_s, l_s, acc_s):
    if mask is not None:
        s = jnp.where(mask, s, NEG_INF)
    m_old = m_s[...]
    m_new = jnp.maximum(m_old, jnp.max(s, axis=-1, keepdims=True))
    p = jnp.exp(s - m_new)
    if mask is not None:
        p = jnp.where(mask, p, 0.0)
    alpha = jnp.exp(m_old - m_new)
    l_s[...] = alpha * l_s[...] + jnp.sum(p, axis=-1, keepdims=True)
    acc_s[...] = alpha * acc_s[...] + jnp.dot(p.astype(BF16), v, preferred_element_type=F32)
    m_s[...] = m_new


def _init_state(m_s, l_s, acc_s):
    m_s[...] = jnp.full(m_s.shape, NEG_INF, F32)
    l_s[...] = jnp.zeros(l_s.shape, F32)
    acc_s[...] = jnp.zeros(acc_s.shape, F32)


def _t5_bias_body(table_ref, dist_ref, out_ref):
    bucket = _t5_bucket(dist_ref[...])
    for h in range(N_NSA_HEADS):
        out_ref[h] = _t5_lookup(bucket, table_ref, h)


def _t5_bias(table, dist):
    rows, cols = dist.shape
    tr = min(rows, 256)
    assert rows % tr == 0
    return pl.pallas_call(
        _t5_bias_body,
        grid=(rows // tr,),
        in_specs=[pl.BlockSpec(memory_space=pltpu.SMEM),
                  pl.BlockSpec((tr, cols), lambda i: (i, 0))],
        out_specs=pl.BlockSpec((N_NSA_HEADS, tr, cols), lambda i: (0, i, 0)),
        out_shape=jax.ShapeDtypeStruct((N_NSA_HEADS, rows, cols), F32),
        compiler_params=_params(("arbitrary",)),
        name="t5_bias",
    )(table, dist)


_PROJ_WIDTHS = (NSA_WIDTH, NSA_KV_WIDTH, NSA_KV_WIDTH, NSA_KV_WIDTH, FOX_WIDTH, 2 * FOX_WIDTH)
_PROJ_COL_CHUNK = 512


def _proj_body(x_ref, g_ref, wm_ref, ws_ref, bf_ref, q_ref, ckv_ref, skv_ref, wkv_ref, qf_ref, kvf_ref, sm_ref):
    x = x_ref[...]
    xn = x * lax.rsqrt(jnp.mean(x * x, axis=-1, keepdims=True) + RMS_EPS) * g_ref[...]
    xb = xn.astype(BF16)
    tm = x.shape[0]
    planes = NSA_KV_WIDTH // HEAD_DIM
    off = 0
    for ref, width in zip((q_ref, ckv_ref, skv_ref, wkv_ref, qf_ref, kvf_ref), _PROJ_WIDTHS):
        for c in range(0, width, _PROJ_COL_CHUNK):
            y = jnp.dot(xb, wm_ref[:, off + c:off + c + _PROJ_COL_CHUNK], preferred_element_type=F32)
            if ref.shape[-1] == width:
                ref[:, c:c + _PROJ_COL_CHUNK] = y
            else:
                for cg in range(planes):
                    ref[pl.ds(cg, tm, stride=planes), :] = y[:, cg * HEAD_DIM:(cg + 1) * HEAD_DIM]
        off += width
    s = jnp.dot(xb, ws_ref[...], preferred_element_type=F32)
    z = s + bf_ref[...]
    lane = _iota(s.shape, 1)
    logf = jnp.minimum(z, 0.0) - jnp.log1p(jnp.exp(-jnp.abs(z)))
    sig = 1.0 / (1.0 + jnp.exp(-s))
    sm_ref[...] = jnp.where(lane < GATE_LANES, sig, logf)


def _project(x2d, g_attn, wm, ws, bf128):
    n, d = x2d.shape
    tm = min(n, 256)
    assert n % tm == 0
    const = lambda shape: pl.BlockSpec(shape, lambda i: (0, 0), pipeline_mode=pl.Buffered(1))
    assert _PROJ_COL_CHUNK == NSA_KV_WIDTH
    planes = NSA_KV_WIDTH // HEAD_DIM
    shapes = [(n, NSA_WIDTH)] + [(n * planes, HEAD_DIM)] * 3 + [(n, FOX_WIDTH), (n, 2 * FOX_WIDTH), (n, LANES)]
    return pl.pallas_call(
        _proj_body,
        grid=(n // tm,),
        in_specs=[pl.BlockSpec((tm, d), lambda i: (i, 0)), const((1, d)), const(wm.shape), const(ws.shape),
                  const((1, LANES))],
        out_specs=[pl.BlockSpec((r // (n // tm), w), lambda i: (i, 0)) for r, w in shapes],
        out_shape=[jax.ShapeDtypeStruct(sh, F32) for sh in shapes],
        compiler_params=_params(("arbitrary",)),
        name="in_proj",
    )(x2d, g_attn.reshape(1, d), wm, ws, bf128)


_CHUNK_K = CMP_STRIDE * HEAD_DIM


def _chunkproj_compute(get_rows, pe_ref, w_ref, out_ref):
    for c in range(2):
        w = w_ref[c]
        pos = jnp.dot(pe_ref[c].astype(BF16), w, preferred_element_type=F32)
        for g in range(N_NSA_KV):
            y = jnp.dot(get_rows(c, g).astype(BF16), w, preferred_element_type=F32)
            for u in range(CMP_SLOTS):
                col = ((u * 2 + c) * N_NSA_KV + g) * HEAD_DIM
                out_ref[0, :, col:col + HEAD_DIM] = (
                    y[:, u * HEAD_DIM:(u + 1) * HEAD_DIM] + pos[u:u + 1, u * HEAD_DIM:(u + 1) * HEAD_DIM])


def _chunkproj_s_body(pt_ref, *refs, n_pg):
    page_refs = refs[:n_pg]
    pe_ref, w_ref, out_ref = refs[n_pg:]
    cpp = PAGE_SIZE // CMP_STRIDE
    planes = 2 * N_NSA_KV

    def get_rows(c, g):
        per_r = [jnp.concatenate([pr[0, pl.ds(r * planes + c * N_NSA_KV + g, cpp, stride=CMP_STRIDE * planes), :]
                                  for pr in page_refs], axis=0) for r in range(CMP_STRIDE)]
        return jnp.concatenate(per_r, axis=1)
    _chunkproj_compute(get_rows, pe_ref, w_ref, out_ref)


def _chunkproj_paged(rows, page_table, pe8, wc):
    batch, n_pages = page_table.shape
    cpp = PAGE_SIZE // CMP_STRIDE
    n_pg = math.gcd(n_pages, 32)
    page_rows = PAGE_SIZE * 2 * N_NSA_KV
    x = rows.reshape(-1, page_rows, HEAD_DIM)
    out_w = CMP_SLOTS * NSA_KV_WIDTH

    def page_spec(k):
        return pl.BlockSpec((1, page_rows, HEAD_DIM),
                            lambda b, j, pt: (pt[b * n_pages + j * n_pg + k], 0, 0))
    grid_spec = pltpu.PrefetchScalarGridSpec(
        num_scalar_prefetch=1,
        grid=(batch, n_pages // n_pg),
        in_specs=[page_spec(k) for k in range(n_pg)] + [
            pl.BlockSpec(pe8.shape, lambda b, j, pt: (0, 0, 0)),
            pl.BlockSpec(wc.shape, lambda b, j, pt: (0, 0, 0))],
        out_specs=pl.BlockSpec((1, n_pg * cpp, out_w), lambda b, j, pt: (b, j, 0)),
    )
    return pl.pallas_call(
        functools.partial(_chunkproj_s_body, n_pg=n_pg),
        grid_spec=grid_spec,
        out_shape=jax.ShapeDtypeStruct((batch, n_pages * cpp, out_w), F32),
        compiler_params=_params(("arbitrary", "arbitrary")),
        name="chunkproj_paged",
    )(page_table.reshape(-1), *([x] * n_pg), pe8, wc)


def _cmp_kv(p00, p10, p01, p11):
    n = p00.shape[1]
    k = p00[0] + pltpu.roll(p10[0], n - 1, 0)
    v = p01[0] + pltpu.roll(p11[0], n - 1, 0)
    return k.astype(BF16), v.astype(BF16)


def _p_specs(n_chunks, index):
    def spec(u, c):
        return pl.BlockSpec((1, n_chunks, HEAD_DIM),
                            lambda *a: (index(*a)[0], 0, (u * 2 + c) * N_NSA_KV + index(*a)[1]))
    return [spec(0, 0), spec(1, 0), spec(0, 1), spec(1, 1)]


def _cmp_attn_p_body(q_ref, p00, p10, p01, p11, bias_ref, oc_ref, sel_ref, *, tq, n_blk, n_slc, n_sel):
    i = pl.program_id(2)
    n_chunks = p00.shape[1]
    k, v = _cmp_kv(p00, p10, p01, p11)
    t = i * tq + _iota((tq, n_chunks), 0)
    n = _iota((tq, n_chunks), 1)
    mask = (n * CMP_STRIDE + (CMP_BLOCK - 1) <= t) & (n < n_blk)
    psum = jnp.zeros((tq, n_chunks), F32)
    for r in range(NSA_REP):
        qr = q_ref[0, :, r * HEAD_DIM:(r + 1) * HEAD_DIM].astype(BF16)
        s = _dot_nt(qr, k) * ATTN_SCALE + bias_ref[r]
        s = jnp.where(mask, s, NEG_INF)
        e = jnp.where(mask, jnp.exp(s - jnp.max(s, axis=-1, keepdims=True)), 0.0)
        l = jnp.sum(e, axis=-1, keepdims=True)
        p = e / jnp.where(l > 0.0, l, 1.0)
        oc_ref[0, :, r * HEAD_DIM:(r + 1) * HEAD_DIM] = jnp.dot(p.astype(BF16), v, preferred_element_type=F32)
        psum = psum + p
    sb = _iota((LANES, n_chunks), 0)
    nb = _iota((LANES, n_chunks), 1)
    cover = ((nb * CMP_STRIDE < sb * SLC_BLOCK + SLC_BLOCK) & (nb * CMP_STRIDE + CMP_BLOCK > sb * SLC_BLOCK))
    imp_t = _dot01_nt(cover.astype(BF16), psum)
    blk = _iota((LANES, tq), 0)
    cur = _div_pow2(i * tq + _iota((LANES, tq), 1), SLC_BLOCK)
    forced = (blk == 0) | (blk == cur) | (blk == cur - 1)
    score = jnp.where(forced, FORCE_SCORE, jnp.where(blk <= cur, imp_t, -FORCE_SCORE))
    rank = jnp.zeros((LANES, tq), F32)
    for s2 in range(n_slc):
        row = score[s2:s2 + 1, :]
        ahead = (row > score) | ((row == score) & (s2 < blk))
        rank = rank + ahead.astype(F32)
    sel_t = ((rank < n_sel) & (blk < n_slc)).astype(F32)
    sel_ref[0, 0] = sel_t


def _cmp_attn_prompt(q, pchunks, bias_c, batch, seq):
    tq = 256
    n_chunks = pchunks.shape[1]
    n_blk = n_chunks - CMP_SLOTS + 1
    n_slc = -(-seq // SLC_BLOCK)
    assert n_chunks == LANES and n_slc <= LANES and seq % tq == 0
    q3 = q.reshape(batch, seq, NSA_WIDTH)
    gw = NSA_REP * HEAD_DIM
    body = functools.partial(_cmp_attn_p_body, tq=tq, n_blk=n_blk, n_slc=n_slc, n_sel=min(N_SELECT, n_slc))
    return pl.pallas_call(
        body,
        grid=(batch, N_NSA_KV, seq // tq),
        in_specs=[pl.BlockSpec((1, tq, gw), lambda b, g, i: (b, i, g))]
        + _p_specs(n_chunks, lambda b, g, i: (b, g))
        + [pl.BlockSpec((NSA_REP, tq, n_chunks), lambda b, g, i: (g, i, 0))],
        out_specs=[pl.BlockSpec((1, tq, gw), lambda b, g, i: (b, i, g)),
                   pl.BlockSpec((1, 1, LANES, tq), lambda b, g, i: (b, g, 0, i))],
        out_shape=[jax.ShapeDtypeStruct((batch, seq, NSA_WIDTH), F32),
                   jax.ShapeDtypeStruct((batch, N_NSA_KV, LANES, seq), F32)],
        compiler_params=_params(("arbitrary",) * 3),
        name="cmp_attn_prompt",
    )(q3, pchunks, pchunks, pchunks, pchunks, bias_c)


_TK = 256


def _flash_step_t(xs, consts, vt, m_s, l_s, acc_s, tq):
    ps, alphas = [], []
    for r, x in enumerate(xs):
        cols = slice(r * tq, (r + 1) * tq)
        m_old = m_s[:, cols]
        m_new = jnp.maximum(m_old, jnp.max(x, axis=0, keepdims=True) + consts[r])
        p = jnp.exp2(x - (m_new - consts[r]))
        alpha = jnp.exp2(m_old - m_new)
        l_s[:, cols] = alpha * l_s[:, cols] + jnp.sum(p, axis=0, keepdims=True)
        m_s[:, cols] = m_new
        ps.append(p.astype(BF16))
        alphas.append(alpha)
    pv = jnp.dot(vt, jnp.concatenate(ps, axis=1), preferred_element_type=F32)
    acc_s[...] = jnp.concatenate(alphas, axis=1) * acc_s[...] + pv


def _nsa_flash_body(table_ref, q_ref, kv_ref, tz_ref, *rest, mode, tq, seq):
    if mode == "slc":
        sel_ref, o_ref, m_s, l_s, acc_s, msk_s = rest
    else:
        o_ref, m_s, l_s, acc_s = rest
    g = pl.program_id(1)
    i = pl.program_id(2)
    qb = jnp.concatenate([q_ref[0, :, r * HEAD_DIM:(r + 1) * HEAD_DIM] * (ATTN_SCALE * LOG2E)
                          for r in range(NSA_REP)], axis=0).astype(BF16)
    _init_state(m_s, l_s, acc_s)
    planes = NSA_KV_WIDTH // HEAD_DIM
    if mode == "slc":
        selb = sel_ref[0, 0].astype(BF16)
        kb = _iota((_TK, LANES), 0)
        sb = _iota((_TK, LANES), 1)
        for jj in range(seq // _TK):
            expand = (sb == _div_pow2(jj * _TK + kb, SLC_BLOCK)).astype(BF16)
            hit = jnp.dot(expand, selb, preferred_element_type=F32)
            msk_s[jj] = (hit - 1.0) * (-NEG_INF)
    kj = _iota((_TK, tq), 0)
    ti = _iota((_TK, tq), 1)
    causal = jnp.where(ti >= kj, 0.0, NEG_INF)
    far_consts = [table_ref[N_BUCKETS - 1, g * NSA_REP + r] * LOG2E for r in range(NSA_REP)]

    def logits(j, kind):
        row0 = j * (_TK * planes) + g
        kt = kv_ref[0, pl.ds(row0, _TK, stride=planes), :].astype(BF16)
        vt = kv_ref[0, pl.ds(row0 + N_NSA_KV, _TK, stride=planes), :].T.astype(BF16)
        x = _dot_nt(kt, qb)
        add = None
        if mode == "slc":
            add = msk_s[j]
        elif kind == "far":
            add = jnp.where((i - j) * _TK + ti - kj <= WINDOW, 0.0, NEG_INF)
        if kind == "diag":
            add = causal if add is None else add + causal
        xs = []
        for r in range(NSA_REP):
            xr = x[:, r * tq:(r + 1) * tq]
            if kind == "prev":
                xr = xr + tz_ref[r, :, tq:2 * tq] * LOG2E
            elif kind == "diag":
                xr = xr + tz_ref[r, :, 0:tq] * LOG2E
            if add is not None:
                xr = xr + add
            xs.append(xr)
        return xs, (far_consts if kind == "far" else [0.0] * NSA_REP), vt

    def run(blocks):
        for xs, consts, vt in [logits(j, kind) for j, kind in blocks]:
            _flash_step_t(xs, consts, vt, m_s, l_s, acc_s, tq)

    lo = 0 if mode == "slc" else jnp.maximum(i - WINDOW // _TK, 0)
    n_far = jnp.maximum(i - 1 - lo, 0)

    def far_pair(p, carry):
        run([(lo + 2 * p, "far"), (lo + 2 * p + 1, "far")])
        return carry
    lax.fori_loop(0, jnp.right_shift(n_far, 1), far_pair, 0)

    @pl.when(jnp.bitwise_and(n_far, 1) == 1)
    def _():
        run([(lo + n_far - 1, "far")])

    @pl.when(i >= 1)
    def _():
        run([(i - 1, "prev"), (i, "diag")])

    @pl.when(i == 0)
    def _():
        run([(i, "diag")])
    o_t = acc_s[...] / l_s[...]
    for r in range(NSA_REP):
        o_ref[0, :, r * HEAD_DIM:(r + 1) * HEAD_DIM] = o_t[:, r * tq:(r + 1) * tq].T


def _nsa_flash_prompt(mode, table, q, kv, tz, sel, batch, seq):
    tq = _TK
    assert _TK >= MAX_DISTANCE and WINDOW % _TK == 0 and seq % _TK == 0
    gw = NSA_REP * HEAD_DIM
    q3 = q.reshape(batch, seq, NSA_WIDTH)
    planes = NSA_KV_WIDTH // HEAD_DIM
    kv3 = kv.reshape(batch, seq * planes, HEAD_DIM)
    cols = NSA_REP * tq
    in_specs = [pl.BlockSpec(memory_space=pltpu.SMEM),
                pl.BlockSpec((1, tq, gw), lambda b, g, i: (b, i, g)),
                pl.BlockSpec((1, seq * planes, HEAD_DIM), lambda b, g, i: (b, 0, 0)),
                pl.BlockSpec((NSA_REP, _TK, 2 * tq), lambda b, g, i: (g, 0, 0))]
    args = [table, q3, kv3, tz]
    scratch = [pltpu.VMEM((1, cols), F32), pltpu.VMEM((1, cols), F32), pltpu.VMEM((HEAD_DIM, cols), F32)]
    if mode == "slc":
        in_specs.append(pl.BlockSpec((1, 1, LANES, tq), lambda b, g, i: (b, g, 0, i)))
        args.append(sel)
        scratch.append(pltpu.VMEM((seq // _TK, _TK, tq), F32))
    return pl.pallas_call(
        functools.partial(_nsa_flash_body, mode=mode, tq=tq, seq=seq),
        grid=(batch, N_NSA_KV, seq // tq),
        in_specs=in_specs,
        out_specs=pl.BlockSpec((1, tq, gw), lambda b, g, i: (b, i, g)),
        out_shape=jax.ShapeDtypeStruct((batch, seq, NSA_WIDTH), F32),
        scratch_shapes=scratch,
        compiler_params=_params(("arbitrary",) * 3),
        name="nsa_flash_" + mode,
    )(*args)


def _cumsum_body(sm_ref, c_ref, ct_ref):
    seq = sm_ref.shape[1]
    lane = _iota((LANES, LANES), 1)
    tri = (_iota((LANES, LANES), 0) >= lane).astype(BF16)
    keep = (lane >= LOGF_LANE0) & (lane < LOGF_LANE0 + N_FOX_HEADS)
    carry = jnp.zeros((1, LANES), F32)
    for blk in range(seq // LANES):
        x = jnp.where(keep, sm_ref[0, blk * LANES:(blk + 1) * LANES, :], 0.0)
        cb = _dot01_left(tri, x) + carry
        c_ref[0, blk * LANES:(blk + 1) * LANES, :] = cb
        ct_ref[0, :, blk * LANES:(blk + 1) * LANES] = cb.T
        carry = cb[LANES - 1:LANES, :]


def _cumsum_prompt(sm, batch, seq):
    sm3 = sm.reshape(batch, seq, LANES)
    return pl.pallas_call(
        _cumsum_body,
        grid=(batch,),
        in_specs=[pl.BlockSpec((1, seq, LANES), lambda b: (b, 0, 0))],
        out_specs=[pl.BlockSpec((1, seq, LANES), lambda b: (b, 0, 0)),
                   pl.BlockSpec((1, LANES, seq), lambda b: (b, 0, 0))],
        out_shape=[jax.ShapeDtypeStruct((batch, seq, LANES), F32),
                   jax.ShapeDtypeStruct((batch, LANES, seq), F32)],
        compiler_params=_params(("arbitrary",)),
        name="logf_cumsum",
    )(sm3)


_FOX_T = 512


def _fox_flash_body(q_ref, k_ref, v_ref, c_ref, crow_ref, o_ref, m_s, l_s, acc_s, *, tq):
    h = pl.program_id(1)
    i = pl.program_id(2)
    qb = (q_ref[0] * (ATTN_SCALE * LOG2E)).astype(BF16)
    _init_state(m_s, l_s, acc_s)
    c_q = crow_ref[0, 0, pl.ds(i, 1), :] * LOG2E
    head_lane = _iota((tq, LANES), 1) == LOGF_LANE0 + h
    causal = jnp.where(_iota((tq, tq), 1) >= _iota((tq, tq), 0), 0.0, NEG_INF)

    def logits(j, diag):
        start = pl.multiple_of(j * tq, tq)
        kt = k_ref[0, pl.ds(start, tq), :].astype(BF16)
        vt = v_ref[0, pl.ds(start, tq), :].T.astype(BF16)
        c_k = jnp.sum(jnp.where(head_lane, c_ref[0, pl.ds(start, tq), :], 0.0), axis=-1, keepdims=True)
        x = _dot_nt(kt, qb) - c_k * LOG2E
        if diag:
            x = x + causal
        return x, vt

    def step(x, vt):
        m_old = m_s[...]
        m_new = jnp.maximum(m_old, jnp.max(x, axis=0, keepdims=True) + c_q)
        p = jnp.exp2(x - (m_new - c_q))
        alpha = jnp.exp2(m_old - m_new)
        l_s[...] = alpha * l_s[...] + jnp.sum(p, axis=0, keepdims=True)
        acc_s[...] = alpha * acc_s[...] + jnp.dot(vt, p.astype(BF16), preferred_element_type=F32)
        m_s[...] = m_new

    def run(blocks):
        for x, vt in [logits(j, diag) for j, diag in blocks]:
            step(x, vt)

    def far_pair(p, carry):
        run([(2 * p, False), (2 * p + 1, False)])
        return carry
    lax.fori_loop(0, jnp.right_shift(i, 1), far_pair, 0)

    @pl.when(jnp.bitwise_and(i, 1) == 1)
    def _():
        run([(i - 1, False), (i, True)])

    @pl.when(jnp.bitwise_and(i, 1) == 0)
    def _():
        run([(i, True)])
    o_ref[0] = (acc_s[...] / l_s[...]).T


def _fox_flash_prompt(qf, kvf, c, c_rows, batch, seq):
    tq = _FOX_T
    assert seq % tq == 0
    q3 = qf.reshape(batch, seq, FOX_WIDTH)
    kv3 = kvf.reshape(batch, seq, 2 * FOX_WIDTH)
    return pl.pallas_call(
        functools.partial(_fox_flash_body, tq=tq),
        grid=(batch, N_FOX_HEADS, seq // tq),
        in_specs=[pl.BlockSpec((1, tq, HEAD_DIM), lambda b, h, i: (b, i, h)),
                  pl.BlockSpec((1, seq, HEAD_DIM), lambda b, h, i: (b, 0, h)),
                  pl.BlockSpec((1, seq, HEAD_DIM), lambda b, h, i: (b, 0, N_FOX_HEADS + h)),
                  pl.BlockSpec((1, seq, LANES), lambda b, h, i: (b, 0, 0)),
                  pl.BlockSpec((1, 1, seq // tq, tq), lambda b, h, i: (b, h, 0, 0))],
        out_specs=pl.BlockSpec((1, tq, HEAD_DIM), lambda b, h, i: (b, i, h)),
        out_shape=jax.ShapeDtypeStruct((batch, seq, FOX_WIDTH), F32),
        scratch_shapes=[pltpu.VMEM((1, tq), F32), pltpu.VMEM((1, tq), F32), pltpu.VMEM((HEAD_DIM, tq), F32)],
        compiler_params=_params(("arbitrary",) * 3),
        name="fox_flash_prompt",
    )(q3, kv3, kv3, c, c_rows)


def _rms(x, g):
    return x * lax.rsqrt(jnp.mean(x * x, axis=-1, keepdims=True) + RMS_EPS) * g


def _outproj_body(x_ref, oc_ref, os_ref, ow_ref, of_ref, sm_ref, gn_ref, gf_ref, wo_ref, h_ref):
    gates = sm_ref[...]
    parts = []
    for hh in range(N_NSA_HEADS):
        sl = slice(hh * HEAD_DIM, (hh + 1) * HEAD_DIM)
        parts.append(gates[:, hh:hh + 1] * oc_ref[:, sl]
                     + gates[:, N_NSA_HEADS + hh:N_NSA_HEADS + hh + 1] * os_ref[:, sl]
                     + gates[:, 2 * N_NSA_HEADS + hh:2 * N_NSA_HEADS + hh + 1] * ow_ref[:, sl])
    o_nsa = jnp.concatenate(parts, axis=1)
    mixed = jnp.concatenate([_rms(o_nsa, gn_ref[...]), _rms(of_ref[...], gf_ref[...])], axis=1).astype(BF16)
    h_ref[...] = x_ref[...] + jnp.dot(mixed, wo_ref[...], preferred_element_type=F32)


def _outproj(x2d, o_c, o_s, o_w, o_f, sm, g_nsa, g_fox, wo):
    n, d = x2d.shape
    tm = min(n, 256)
    row = lambda w: pl.BlockSpec((tm, w), lambda i: (i, 0))
    const = lambda shape: pl.BlockSpec(shape, lambda i: (0, 0), pipeline_mode=pl.Buffered(1))
    return pl.pallas_call(
        _outproj_body,
        grid=(n // tm,),
        in_specs=[row(d), row(NSA_WIDTH), row(NSA_WIDTH), row(NSA_WIDTH), row(FOX_WIDTH), row(LANES),
                  const((1, NSA_WIDTH)), const((1, FOX_WIDTH)), const(wo.shape)],
        out_specs=row(d),
        out_shape=jax.ShapeDtypeStruct((n, d), F32),
        compiler_params=_params(("arbitrary",)),
        name="out_proj",
    )(x2d, o_c, o_s, o_w, o_f, sm, g_nsa.reshape(1, -1), g_fox.reshape(1, -1), wo)


def _mlp_body(h_ref, gm_ref, gfin_ref, wu_ref, wd_ref, y_ref, xn_s, acc_s):
    j = pl.program_id(1)

    @pl.when(j == 0)
    def _():
        xn_s[...] = _rms(h_ref[...], gm_ref[...]).astype(BF16)
        acc_s[...] = jnp.zeros(acc_s.shape, F32)
    u = jnp.maximum(jnp.dot(xn_s[...], wu_ref[...], preferred_element_type=F32), 0.0)
    acc_s[...] += jnp.dot((u * u).astype(BF16), wd_ref[...], preferred_element_type=F32)

    @pl.when(j == pl.num_programs(1) - 1)
    def _():
        y_ref[...] = _rms(h_ref[...] + acc_s[...], gfin_ref[...])


def _mlp(h, g_mlp, g_final, wu, wd):
    n, d = h.shape
    dff = wu.shape[1]
    tm = min(n, 512)
    tf = 1024
    return pl.pallas_call(
        _mlp_body,
        grid=(n // tm, dff // tf),
        in_specs=[pl.BlockSpec((tm, d), lambda i, j: (i, 0)),
                  pl.BlockSpec((1, d), lambda i, j: (0, 0)),
                  pl.BlockSpec((1, d), lambda i, j: (0, 0)),
                  pl.BlockSpec((d, tf), lambda i, j: (0, j)),
                  pl.BlockSpec((tf, d), lambda i, j: (j, 0))],
        out_specs=pl.BlockSpec((tm, d), lambda i, j: (i, 0)),
        out_shape=jax.ShapeDtypeStruct((n, d), F32),
        scratch_shapes=[pltpu.VMEM((tm, d), BF16), pltpu.VMEM((tm, d), F32)],
        compiler_params=_params(("arbitrary", "arbitrary")),
        name="mlp_final",
    )(h, g_mlp.reshape(1, d), g_final.reshape(1, d), wu, wd)


def _cmp_attn_s_body(q_ref, p00, p10, p01, p11, bias_ref, oc_ref, imp_ref, *, past, n_blk):
    n_chunks = p00.shape[1]
    t_new = q_ref.shape[1]
    sl = imp_ref.shape[-1]
    k, v = _cmp_kv(p00, p10, p01, p11)
    qb = jnp.concatenate([q_ref[0, :, r * HEAD_DIM:(r + 1) * HEAD_DIM] for r in range(NSA_REP)],
                         axis=0).astype(BF16)
    bias = jnp.concatenate([bias_ref[r] for r in range(NSA_REP)], axis=0)
    rows = NSA_REP * t_new
    t = past + _mod_pow2(_iota((rows, n_chunks), 0), t_new)
    n = _iota((rows, n_chunks), 1)
    mask = (n * CMP_STRIDE + (CMP_BLOCK - 1) <= t) & (n < n_blk)
    s = jnp.where(mask, _dot_nt(qb, k) * ATTN_SCALE + bias, NEG_INF)
    e = jnp.where(mask, jnp.exp(s - jnp.max(s, axis=-1, keepdims=True)), 0.0)
    l = jnp.sum(e, axis=-1, keepdims=True)
    p = e / jnp.where(l > 0.0, l, 1.0)
    o = jnp.dot(p.astype(BF16), v, preferred_element_type=F32)
    psum = jnp.zeros((t_new, n_chunks), F32)
    for r in range(NSA_REP):
        oc_ref[0, :, r * HEAD_DIM:(r + 1) * HEAD_DIM] = o[r * t_new:(r + 1) * t_new]
        psum = psum + p[r * t_new:(r + 1) * t_new]
    nb = _iota((n_chunks, sl), 0)
    sb = _iota((n_chunks, sl), 1)
    cover = ((nb * CMP_STRIDE < sb * SLC_BLOCK + SLC_BLOCK) & (nb * CMP_STRIDE + CMP_BLOCK > sb * SLC_BLOCK))
    imp = _dot01_right(psum, cover.astype(BF16))
    imp_ref[0, 0] = imp


def _topk_s_body(imp_ref, sel_ref, *, past, t_new, n_slc, n_sel):
    rows, sl = imp_ref.shape
    blk = _iota((rows, sl), 1)
    cur = _div_pow2(past + _mod_pow2(_iota((rows, sl), 0), t_new), SLC_BLOCK)
    forced = (blk == 0) | (blk == cur) | (blk == cur - 1)
    score = jnp.where(forced, FORCE_SCORE, jnp.where(blk <= cur, imp_ref[...], -FORCE_SCORE))
    score = jnp.where(blk < n_slc, score, -jnp.inf)
    sel = jnp.zeros((rows, sl), F32)
    for _ in range(n_sel):
        mx = jnp.max(score, axis=-1, keepdims=True)
        first = jnp.min(jnp.where(score == mx, blk, sl), axis=-1, keepdims=True)
        hit = blk == first
        sel = jnp.where(hit, 1.0, sel)
        score = jnp.where(hit, -jnp.inf, score)
    sel_ref[...] = sel


def _cmp_attn_sample(q, pchunks, bias_c, batch, t_new, past):
    n_chunks = pchunks.shape[1]
    n_blk = n_chunks - CMP_SLOTS + 1
    n_slc = -(-(past + t_new) // SLC_BLOCK)
    sl = -(-n_slc // LANES) * LANES
    gw = NSA_REP * HEAD_DIM
    q3 = q.reshape(batch, t_new, NSA_WIDTH)
    body = functools.partial(_cmp_attn_s_body, past=past, n_blk=n_blk)
    o_c, imp = pl.pallas_call(
        body,
        grid=(batch, N_NSA_KV),
        in_specs=[pl.BlockSpec((1, t_new, gw), lambda b, g: (b, 0, g))]
        + _p_specs(n_chunks, lambda b, g: (b, g))
        + [pl.BlockSpec((NSA_REP, t_new, n_chunks), lambda b, g: (g, 0, 0))],
        out_specs=[pl.BlockSpec((1, t_new, gw), lambda b, g: (b, 0, g)),
                   pl.BlockSpec((1, 1, t_new, sl), lambda b, g: (b, g, 0, 0))],
        out_shape=[jax.ShapeDtypeStruct((batch, t_new, NSA_WIDTH), F32),
                   jax.ShapeDtypeStruct((batch, N_NSA_KV, t_new, sl), F32)],
        compiler_params=_params(("arbitrary",) * 2),
        name="cmp_attn_sample",
    )(q3, pchunks, pchunks, pchunks, pchunks, bias_c)
    rows = batch * N_NSA_KV * t_new
    sel = pl.pallas_call(
        functools.partial(_topk_s_body, past=past, t_new=t_new, n_slc=n_slc, n_sel=min(N_SELECT, n_slc)),
        grid=(1,),
        in_specs=[pl.BlockSpec((rows, sl), lambda i: (0, 0))],
        out_specs=pl.BlockSpec((rows, sl), lambda i: (0, 0)),
        out_shape=jax.ShapeDtypeStruct((rows, sl), F32),
        compiler_params=_params(("arbitrary",)),
        name="topk_sample",
    )(imp.reshape(rows, sl))
    return o_c, sel.reshape(batch, N_NSA_KV, t_new, sl)


_GKV = N_NSA_KV * HEAD_DIM


def _dec_tile(qbd, kv, dist, mask, table_ref, full_bias, t_new, m_s, l_s, acc_s):
    s = _dot_nt(qbd, kv[:, :_GKV].astype(BF16)) * ATTN_SCALE
    parts = []
    for hh in range(N_NSA_HEADS):
        rows = slice(hh * t_new, (hh + 1) * t_new)
        if full_bias:
            b = _t5_lookup(_t5_bucket(dist[rows]), table_ref, hh)
        else:
            b = jnp.full((t_new, s.shape[1]), table_ref[N_BUCKETS - 1, hh], F32)
        parts.append(s[rows] + b)
    s = jnp.concatenate(parts, axis=0)
    _online_update(s, mask, kv[:, _GKV:].astype(BF16), m_s, l_s, acc_s)


def _dec_finish(o_ref, t_new, l_s, acc_s):
    o = acc_s[...] / l_s[...]
    for g in range(N_NSA_KV):
        for r in range(NSA_REP):
            hh = g * NSA_REP + r
            o_ref[0, :, hh * HEAD_DIM:(hh + 1) * HEAD_DIM] = o[hh * t_new:(hh + 1) * t_new,
                                                             g * HEAD_DIM:(g + 1) * HEAD_DIM]


def _row_mask(m2, t_new):
    return jnp.concatenate([m2[g * t_new:(g + 1) * t_new] for g in range(N_NSA_KV) for _ in range(NSA_REP)], axis=0)


def _slc_s_body(pt_ref, table_ref, q_ref, *rest, n_pg, past, t_new):
    page_refs = rest[:n_pg]
    selp_ref, knew_ref, vnew_ref, selnew_ref, o_ref, m_s, l_s, acc_s = rest[n_pg:]
    j = pl.program_id(1)
    last = pl.num_programs(1) - 1
    grp_rows = NSA_REP * t_new
    q = (q_ref[0] * ATTN_SCALE).astype(BF16)
    t_pos = past + _iota((t_new, PAGE_SIZE), 0)
    lane = _iota((t_new, PAGE_SIZE), 1)

    @pl.when(j == 0)
    def _():
        _init_state(m_s, l_s, acc_s)

    def update(x, mask, vs):
        x = jnp.where(mask, x, NEG_INF)
        m_old = m_s[...]
        m_new = jnp.maximum(m_old, jnp.max(x, axis=-1, keepdims=True))
        p = jnp.where(mask, jnp.exp(x - m_new), 0.0)
        alpha = jnp.exp(m_old - m_new)
        l_s[...] = alpha * l_s[...] + jnp.sum(p, axis=-1, keepdims=True)
        pb = p.astype(BF16)
        pv = jnp.concatenate([jnp.dot(pb[g * grp_rows:(g + 1) * grp_rows], vs[g], preferred_element_type=F32)
                              for g in range(N_NSA_KV)], axis=0)
        acc_s[...] = alpha * acc_s[...] + pv
        m_s[...] = m_new

    def head_bias(hh, dist):
        return _t5_lookup(_t5_bucket(dist), table_ref, hh)

    dist_last = t_pos - ((j * n_pg + n_pg - 1) * PAGE_SIZE + lane)
    xs, masks, vs = [], [], []
    for g in range(N_NSA_KV):
        plane = lambda c: jnp.concatenate(
            [pr[0, pl.ds(c * N_NSA_KV + g, PAGE_SIZE, stride=2 * N_NSA_KV), :] for pr in page_refs],
            axis=0).astype(BF16)
        x = _dot_nt(q[g * grp_rows:(g + 1) * grp_rows], plane(0))
        vs.append(plane(1))
        for r in range(NSA_REP):
            hh = g * NSA_REP + r
            far = jnp.full((t_new, (n_pg - 1) * PAGE_SIZE), table_ref[N_BUCKETS - 1, hh], F32)
            bias = jnp.concatenate([far, head_bias(hh, dist_last)], axis=1)
            xs.append(x[r * t_new:(r + 1) * t_new] + bias)
        picked = []
        for k in range(n_pg):
            m2 = selp_ref[0, k][g * t_new:(g + 1) * t_new]
            picked.append(jnp.where(lane < SLC_BLOCK, m2[:, 0:1], m2[:, 1:2]))
        masks.extend([jnp.concatenate(picked, axis=1)] * NSA_REP)
    update(jnp.concatenate(xs, axis=0), jnp.concatenate(masks, axis=0) > 0.5, vs)

    @pl.when(j == last)
    def _():
        dist = t_pos - (past + lane)
        xs, masks, vs = [], [], []
        for g in range(N_NSA_KV):
            x = _dot_nt(q[g * grp_rows:(g + 1) * grp_rows], knew_ref[0, g].astype(BF16))
            vs.append(vnew_ref[0, g].astype(BF16))
            for r in range(NSA_REP):
                xs.append(x[r * t_new:(r + 1) * t_new] + head_bias(g * NSA_REP + r, dist))
            sel_g = jnp.broadcast_to(selnew_ref[0][g * t_new:(g + 1) * t_new, 0:1], (t_new, PAGE_SIZE)) > 0.5
            masks.extend([sel_g & (dist >= 0) & (lane < t_new)] * NSA_REP)
        update(jnp.concatenate(xs, axis=0), jnp.concatenate(masks, axis=0), vs)
        o = acc_s[...] / l_s[...]
        for hh in range(N_NSA_HEADS):
            o_ref[0, :, hh * HEAD_DIM:(hh + 1) * HEAD_DIM] = o[hh * t_new:(hh + 1) * t_new]


def _slc_sample(table, page_table, q_rows, cache, selp, knew, vnew, selnew, past, t_new):
    batch, n_pages = page_table.shape
    n_pg = math.gcd(n_pages, 16)
    n_phys = cache.shape[0]
    assert t_new <= PAGE_SIZE and past % PAGE_SIZE == 0 and PAGE_SIZE >= MAX_DISTANCE
    page_rows = PAGE_SIZE * 2 * N_NSA_KV
    x = cache.reshape(n_phys, page_rows, HEAD_DIM)
    rows = N_NSA_HEADS * t_new

    def page_spec(k):
        return pl.BlockSpec((1, page_rows, HEAD_DIM), lambda b, j, pt: (pt[b * n_pages + j * n_pg + k], 0, 0))
    new_spec = pl.BlockSpec((1, N_NSA_KV, PAGE_SIZE, HEAD_DIM), lambda b, j, pt: (b, 0, 0, 0))
    grid_spec = pltpu.PrefetchScalarGridSpec(
        num_scalar_prefetch=1,
        grid=(batch, n_pages // n_pg),
        in_specs=[pl.BlockSpec(memory_space=pltpu.SMEM),
                  pl.BlockSpec((1, rows, HEAD_DIM), lambda b, j, pt: (b, 0, 0))]
        + [page_spec(k) for k in range(n_pg)]
        + [pl.BlockSpec((1, n_pg, N_NSA_KV * t_new, 2), lambda b, j, pt: (b, j, 0, 0)),
           new_spec, new_spec,
           pl.BlockSpec((1, N_NSA_KV * t_new, 2), lambda b, j, pt: (b, 0, 0))],
        out_specs=pl.BlockSpec((1, t_new, NSA_WIDTH), lambda b, j, pt: (b, 0, 0)),
        scratch_shapes=[pltpu.VMEM((rows, 1), F32), pltpu.VMEM((rows, 1), F32), pltpu.VMEM((rows, HEAD_DIM), F32)],
    )
    return pl.pallas_call(
        functools.partial(_slc_s_body, n_pg=n_pg, past=past, t_new=t_new),
        grid_spec=grid_spec,
        out_shape=jax.ShapeDtypeStruct((batch, t_new, NSA_WIDTH), F32),
        compiler_params=_params(("arbitrary", "arbitrary")),
        name="slc_sample",
    )(page_table.reshape(-1), table, q_rows, *([x] * n_pg), selp, knew, vnew, selnew)


def _win_s_body(table_ref, q_ref, kv_ref, o_ref, m_s, l_s, acc_s, *, past, t_new, wb):
    rows = N_NSA_HEADS * t_new
    qbd = q_ref[0]
    t_pos = past + _mod_pow2(_iota((rows, PAGE_SIZE), 0), t_new)
    lane = _iota((rows, PAGE_SIZE), 1)
    _init_state(m_s, l_s, acc_s)
    for k in range(kv_ref.shape[1] // PAGE_SIZE):
        k_pos = past - wb + k * PAGE_SIZE + lane
        dist = t_pos - k_pos
        mask = (dist >= 0) & (dist <= WINDOW) & (k_pos >= 0) & (k * PAGE_SIZE + lane < wb + t_new)
        _dec_tile(qbd, kv_ref[0, k * PAGE_SIZE:(k + 1) * PAGE_SIZE, :], dist, mask, table_ref, True, t_new,
                  m_s, l_s, acc_s)
    _dec_finish(o_ref, t_new, l_s, acc_s)


def _win_sample(table, qbd, win_all_padded, past, t_new, wb):
    batch, n_keys, _ = win_all_padded.shape
    rows = N_NSA_HEADS * t_new
    return pl.pallas_call(
        functools.partial(_win_s_body, past=past, t_new=t_new, wb=wb),
        grid=(batch,),
        in_specs=[pl.BlockSpec(memory_space=pltpu.SMEM),
                  pl.BlockSpec((1, rows, _GKV), lambda b: (b, 0, 0)),
                  pl.BlockSpec((1, n_keys, 2 * _GKV), lambda b: (b, 0, 0))],
        out_specs=pl.BlockSpec((1, t_new, NSA_WIDTH), lambda b: (b, 0, 0)),
        out_shape=jax.ShapeDtypeStruct((batch, t_new, NSA_WIDTH), F32),
        scratch_shapes=[pltpu.VMEM((rows, 1), F32), pltpu.VMEM((rows, 1), F32), pltpu.VMEM((rows, _GKV), F32)],
        compiler_params=_params(("arbitrary",)),
        name="win_sample",
    )(table, qbd, win_all_padded)


_FOX_PAGE_LANES = PAGE_SIZE * N_FOX_HEADS


def _fox_s_body(pt_ref, q_ref, *rest, n_pg, t_new):
    kv_refs = rest[:n_pg]
    lf_refs = rest[n_pg:2 * n_pg]
    tri_ref, knew_ref, vnew_ref, lfn_ref, o_ref, m_s, l_s, acc_s, off_s = rest[2 * n_pg:]
    j = pl.program_id(1)
    rows = N_FOX_HEADS * t_new
    pw = _FOX_PAGE_LANES
    qb = (q_ref[0] * ATTN_SCALE).astype(BF16)
    lane = _iota((rows, PAGE_SIZE), 1)
    t_row = _mod_pow2(_iota((rows, PAGE_SIZE), 0), t_new)
    h_row = _div_pow2(_iota((rows, PAGE_SIZE), 0), t_new)
    lfn = lfn_ref[0]
    c_new = jnp.sum(jnp.where(lane <= t_row, lfn, 0.0), axis=-1, keepdims=True)

    @pl.when(j == 0)
    def _():
        _init_state(m_s, l_s, acc_s)
        off_s[...] = jnp.zeros(off_s.shape, F32)
        u_l = _div_pow2(lane, N_FOX_HEADS)
        bias = jnp.zeros((rows, PAGE_SIZE), F32)
        for u in range(t_new):
            col = jnp.sum(jnp.where((lane > u) & (lane <= t_row), lfn, 0.0), axis=-1, keepdims=True)
            bias = jnp.where(u_l == u, col, bias)
        x = _dot_nt(qb, knew_ref[0].astype(BF16)) + bias
        mask = (u_l <= t_row) & (u_l < t_new) & (_mod_pow2(lane, N_FOX_HEADS) == h_row)
        _online_update(x, mask, vnew_ref[0].astype(BF16), m_s, l_s, acc_s)

    lf = jnp.concatenate([r[0] for r in lf_refs], axis=0)
    later = _dot01_right(lf, tri_ref[...])
    tot = jnp.sum(lf, axis=-1, keepdims=True)
    head_lane = jnp.where(
        _mod_pow2(_iota((rows, pw), 1), N_FOX_HEADS) == _div_pow2(_iota((rows, pw), 0), t_new), 0.0, NEG_INF)
    spread = lambda a: jnp.concatenate(
        [jnp.broadcast_to(a[h:h + 1], (t_new, a.shape[1])) for h in range(N_FOX_HEADS)], axis=0)
    off = off_s[:, 0:1]
    xs, vs = [], []
    for k in range(n_pg):
        kk = kv_refs[k][0, :, 0].reshape(pw, HEAD_DIM).astype(BF16)
        vs.append(kv_refs[k][0, :, 1].reshape(pw, HEAD_DIM).astype(BF16))
        hs = slice(k * N_FOX_HEADS, (k + 1) * N_FOX_HEADS)
        xs.append(_dot_nt(qb, kk) + spread(later[hs]) + (spread(off) + c_new) + head_lane)
        off = off + tot[hs]
    off_s[...] = jnp.broadcast_to(off, off_s.shape)
    x = jnp.concatenate(xs, axis=1)
    m_old = m_s[...]
    m_new = jnp.maximum(m_old, jnp.max(x, axis=-1, keepdims=True))
    p = jnp.exp(x - m_new)
    alpha = jnp.exp(m_old - m_new)
    l_s[...] = alpha * l_s[...] + jnp.sum(p, axis=-1, keepdims=True)
    acc_s[...] = alpha * acc_s[...] + jnp.dot(p.astype(BF16), jnp.concatenate(vs, axis=0),
                                              preferred_element_type=F32)
    m_s[...] = m_new

    @pl.when(j == pl.num_programs(1) - 1)
    def _():
        o = acc_s[...] / l_s[...]
        for h in range(N_FOX_HEADS):
            o_ref[0, :, h * HEAD_DIM:(h + 1) * HEAD_DIM] = o[h * t_new:(h + 1) * t_new]


def _fox_sample(page_table, q_rows, cache, logf_t, tri, knew, vnew, lfn, t_new):
    batch, n_pages = page_table.shape
    n_pg = math.gcd(n_pages, 8)
    rows = N_FOX_HEADS * t_new
    page = lambda k: (lambda b, j, pt: pt[b * n_pages + n_pages - 1 - (j * n_pg + k)])

    def kv_spec(k):
        return pl.BlockSpec((1, PAGE_SIZE, 2, N_FOX_HEADS, HEAD_DIM), lambda b, j, pt: (page(k)(b, j, pt), 0, 0, 0, 0))

    def lf_spec(k):
        return pl.BlockSpec((1, N_FOX_HEADS, PAGE_SIZE), lambda b, j, pt: (page(k)(b, j, pt), 0, 0))
    new_spec = pl.BlockSpec((1, PAGE_SIZE, HEAD_DIM), lambda b, j, pt: (b, 0, 0))
    grid_spec = pltpu.PrefetchScalarGridSpec(
        num_scalar_prefetch=1,
        grid=(batch, n_pages // n_pg),
        in_specs=[pl.BlockSpec((1, rows, HEAD_DIM), lambda b, j, pt: (b, 0, 0))]
        + [kv_spec(k) for k in range(n_pg)] + [lf_spec(k) for k in range(n_pg)]
        + [pl.BlockSpec(tri.shape, lambda b, j, pt: (0, 0)), new_spec, new_spec,
           pl.BlockSpec((1, rows, PAGE_SIZE), lambda b, j, pt: (b, 0, 0))],
        out_specs=pl.BlockSpec((1, t_new, FOX_WIDTH), lambda b, j, pt: (b, 0, 0)),
        scratch_shapes=[pltpu.VMEM((rows, 1), F32), pltpu.VMEM((rows, 1), F32), pltpu.VMEM((rows, HEAD_DIM), F32),
                        pltpu.VMEM((N_FOX_HEADS, LANES), F32)],
    )
    return pl.pallas_call(
        functools.partial(_fox_s_body, n_pg=n_pg, t_new=t_new),
        grid_spec=grid_spec,
        out_shape=jax.ShapeDtypeStruct((batch, t_new, FOX_WIDTH), F32),
        compiler_params=_params(("arbitrary", "arbitrary")),
        name="fox_sample",
    )(page_table.reshape(-1), q_rows, *([cache] * n_pg), *([logf_t] * n_pg), tri, knew, vnew, lfn)


def _pack_weights(w_in, b_f, w_cmp, pe_cmp, w_o, w_up, w_down):
    c_gt = NSA_WIDTH + 3 * NSA_KV_WIDTH
    c_qf = c_gt + GATE_LANES
    c_fl = c_qf + 3 * FOX_WIDTH
    wm = jnp.concatenate([w_in[:, :c_gt], w_in[:, c_qf:c_fl]], axis=1).astype(BF16)
    ws = jnp.concatenate([w_in[:, c_gt:c_qf], w_in[:, c_fl:],
                          jnp.zeros((w_in.shape[0], LANES - GATE_LANES - N_FOX_HEADS), F32)], axis=1).astype(BF16)
    bf128 = jnp.zeros((1, LANES), F32).at[0, LOGF_LANE0:LOGF_LANE0 + N_FOX_HEADS].set(b_f)
    wc = w_cmp.reshape(2, CMP_SLOTS, CMP_STRIDE, HEAD_DIM, HEAD_DIM).transpose(0, 2, 3, 1, 4).reshape(
        2, _CHUNK_K, CMP_SLOTS * HEAD_DIM).astype(BF16)
    pe8 = jnp.zeros((2, SUBLANES, _CHUNK_K), F32).at[:, :CMP_SLOTS].set(pe_cmp.reshape(2, CMP_SLOTS, _CHUNK_K))
    return wm, ws, bf128, wc, pe8, w_o.astype(BF16), w_up.astype(BF16), w_down.astype(BF16)


def _block_diag_q(q, t_new, n_grp, per_grp):
    batch = q.shape[0]
    q5 = q.reshape(batch, t_new, n_grp, per_grp, HEAD_DIM).transpose(0, 2, 3, 1, 4)
    eye = jnp.eye(n_grp, dtype=q.dtype)
    out = q5[:, :, :, :, None, :] * eye[None, :, None, None, :, None]
    return out.reshape(batch, n_grp * per_grp * t_new, n_grp * HEAD_DIM).astype(BF16)


def _prompt_pass(x, table, lw, packed):
    g_attn, g_nsa_out, g_fox_out, g_mlp, g_final = lw
    wm, ws, bf128, wc, pe8, wo, wu, wd = packed
    batch, seq, d = x.shape
    assert seq % 256 == 0 and seq >= WINDOW and seq % PAGE_SIZE == 0
    x2d = x.reshape(batch * seq, d)
    q, ckv, skv, wkv, qf, kvf, sm = _project(x2d, g_attn, wm, ws, bf128)
    own_pages = jnp.arange(batch * (seq // PAGE_SIZE), dtype=I32).reshape(batch, seq // PAGE_SIZE)
    pchunks = _chunkproj_paged(ckv, own_pages, pe8, wc)
    n_chunks = seq // CMP_STRIDE
    dist_c = jnp.arange(seq, dtype=I32)[:, None] - (jnp.arange(n_chunks, dtype=I32)[None, :] * CMP_STRIDE
                                                     + (CMP_BLOCK - 1))
    bias_c = _t5_bias(table, dist_c)
    o_c, sel = _cmp_attn_prompt(q, pchunks, bias_c, batch, seq)
    dist_tz = jnp.arange(2 * _TK, dtype=I32)[None, :] - jnp.arange(_TK, dtype=I32)[:, None]
    tz = _t5_bias(table, dist_tz)
    o_s = _nsa_flash_prompt("slc", table, q, skv, tz, sel, batch, seq)
    o_w = _nsa_flash_prompt("win", table, q, wkv, tz, None, batch, seq)
    c, c_t = _cumsum_prompt(sm, batch, seq)
    c_rows = c_t[:, LOGF_LANE0:LOGF_LANE0 + N_FOX_HEADS, :].reshape(batch, N_FOX_HEADS, seq // _FOX_T, _FOX_T)
    o_f = _fox_flash_prompt(qf, kvf, c, c_rows, batch, seq)
    h = _outproj(x2d, o_c.reshape(-1, NSA_WIDTH), o_s.reshape(-1, NSA_WIDTH), o_w.reshape(-1, NSA_WIDTH),
                 o_f.reshape(-1, FOX_WIDTH), sm, g_nsa_out, g_fox_out, wo)
    y = _mlp(h, g_mlp, g_final, wu, wd).reshape(batch, seq, d)
    kv_shape = (1, batch, seq, 2, N_NSA_KV, HEAD_DIM)
    wkv5 = wkv.reshape(kv_shape)
    logf = sm.reshape(batch, seq, LANES)[:, :, LOGF_LANE0:LOGF_LANE0 + N_FOX_HEADS]
    return (y, ckv.reshape(kv_shape), skv.reshape(kv_shape),
            kvf.reshape(1, batch, seq, 2, N_FOX_HEADS, HEAD_DIM), logf[None],
            wkv5[:, :, seq - min(WINDOW, seq):])


def _sample_pass(x, cache_cmp, cache_slc, cache_fox, cache_logf, win_buf, page_table, table, lw, packed):
    g_attn, g_nsa_out, g_fox_out, g_mlp, g_final = lw
    wm, ws, bf128, wc, pe8, wo, wu, wd = packed
    batch, t_new, d = x.shape
    n_pages = page_table.shape[1]
    past = n_pages * PAGE_SIZE
    assert t_new < CMP_STRIDE and t_new % SUBLANES == 0 and t_new * N_FOX_HEADS <= PAGE_SIZE
    x2d = x.reshape(batch * t_new, d)
    q, ckv, skv, wkv, qf, kvf, sm = _project(x2d, g_attn, wm, ws, bf128)
    pchunks = _chunkproj_paged(cache_cmp, page_table, pe8, wc)
    n_chunks = pchunks.shape[1]
    pos = past + jnp.arange(t_new, dtype=I32)
    dist_c = pos[:, None] - (jnp.arange(n_chunks, dtype=I32)[None, :] * CMP_STRIDE + (CMP_BLOCK - 1))
    bias_c = _t5_bias(table, dist_c)
    o_c, sel = _cmp_attn_sample(q, pchunks, bias_c, batch, t_new, past)
    q5 = q.reshape(batch, t_new, N_NSA_KV, NSA_REP, HEAD_DIM)
    q_rows = q5.transpose(0, 2, 3, 1, 4).reshape(batch, N_NSA_HEADS * t_new, HEAD_DIM)
    n_past_blk = past // SLC_BLOCK
    bpp = PAGE_SIZE // SLC_BLOCK
    selp = sel[..., :n_past_blk].reshape(batch, N_NSA_KV, t_new, n_pages, bpp).transpose(0, 3, 1, 2, 4).reshape(
        batch, n_pages, N_NSA_KV * t_new, bpp)
    selnew = jnp.broadcast_to(sel[..., n_past_blk:n_past_blk + 1].reshape(batch, N_NSA_KV * t_new, 1),
                              (batch, N_NSA_KV * t_new, bpp))
    skv5 = skv.reshape(batch, t_new, 2, N_NSA_KV, HEAD_DIM).transpose(2, 0, 3, 1, 4)
    skv5 = jnp.pad(skv5, ((0, 0), (0, 0), (0, 0), (0, PAGE_SIZE - t_new), (0, 0)))
    o_s = _slc_sample(table, page_table, q_rows, cache_slc, selp, skv5[0], skv5[1], selnew, past, t_new)
    qbd = _block_diag_q(q.reshape(batch, t_new, NSA_WIDTH), t_new, N_NSA_KV, NSA_REP)
    wb = win_buf.shape[1]
    win_all = jnp.concatenate([win_buf.reshape(batch, wb, NSA_KV_WIDTH), wkv.reshape(batch, t_new, NSA_KV_WIDTH)],
                              axis=1)
    n_keys = -(-(wb + t_new) // PAGE_SIZE) * PAGE_SIZE
    o_w = _win_sample(table, qbd, jnp.pad(win_all, ((0, 0), (0, n_keys - wb - t_new), (0, 0))), past, t_new, wb)
    logf_new = sm.reshape(batch, t_new, LANES)[:, :, LOGF_LANE0:LOGF_LANE0 + N_FOX_HEADS]
    logf_t = cache_logf.transpose(0, 2, 1)
    qf_rows = qf.reshape(batch, t_new, N_FOX_HEADS, HEAD_DIM).transpose(0, 2, 1, 3).reshape(
        batch, N_FOX_HEADS * t_new, HEAD_DIM)
    kvf5 = kvf.reshape(batch, t_new, 2, N_FOX_HEADS * HEAD_DIM).transpose(2, 0, 1, 3).reshape(
        2, batch, t_new * N_FOX_HEADS, HEAD_DIM)
    kvf5 = jnp.pad(kvf5, ((0, 0), (0, 0), (0, PAGE_SIZE - t_new * N_FOX_HEADS), (0, 0)))
    lfn = jnp.broadcast_to(logf_new.transpose(0, 2, 1)[:, :, None, :], (batch, N_FOX_HEADS, t_new, t_new))
    lfn = jnp.pad(lfn.reshape(batch, N_FOX_HEADS * t_new, t_new), ((0, 0), (0, 0), (0, PAGE_SIZE - t_new)))
    tri = (jnp.arange(PAGE_SIZE)[:, None] > jnp.arange(_FOX_PAGE_LANES)[None, :] // N_FOX_HEADS).astype(BF16)
    o_f = _fox_sample(page_table, qf_rows, cache_fox, logf_t, tri, kvf5[0], kvf5[1], lfn, t_new)
    h = _outproj(x2d, o_c.reshape(-1, NSA_WIDTH), o_s.reshape(-1, NSA_WIDTH), o_w.reshape(-1, NSA_WIDTH),
                 o_f.reshape(-1, FOX_WIDTH), sm, g_nsa_out, g_fox_out, wo)
    y = _mlp(h, g_mlp, g_final, wu, wd).reshape(batch, t_new, d)
    kv_shape = (1, batch, t_new, 2, N_NSA_KV, HEAD_DIM)
    return (y, ckv.reshape(kv_shape), skv.reshape(kv_shape),
            kvf.reshape(1, batch, t_new, 2, N_FOX_HEADS, HEAD_DIM), logf_new[None],
            win_all[:, t_new:].reshape(1, batch, wb, 2, N_NSA_KV, HEAD_DIM))


def kernel(x_prompt, x_sample, cache_cmp_kv, cache_slc_kv, cache_fox_kv, cache_fox_logf, state_win_kv, page_table,
           t5_table, g_attn, w_in, b_f, w_cmp, pe_cmp, g_nsa_out, g_fox_out, w_o, g_mlp, w_up, w_down, g_final):
    assert g_attn.shape[0] == 1, "single-layer step"
    assert x_prompt.shape[-1] == NSA_WIDTH + FOX_WIDTH
    packed = _pack_weights(w_in[0], b_f[0], w_cmp[0], pe_cmp[0], w_o[0], w_up[0], w_down[0])
    lw = (g_attn[0], g_nsa_out[0], g_fox_out[0], g_mlp[0], g_final)
    yp, cmp_p, slc_p, fox_p, logf_p, win_p = _prompt_pass(x_prompt, t5_table, lw, packed)
    ys, cmp_s, slc_s, fox_s, logf_s, win_s = _sample_pass(
        x_sample, cache_cmp_kv[0], cache_slc_kv[0], cache_fox_kv[0], cache_fox_logf[0], state_win_kv[0],
        page_table, t5_table, lw, packed)
    return (yp, ys, cmp_p, slc_p, fox_p, logf_p, win_p, cmp_s, slc_s, fox_s, logf_s, win_s)
```

```python
import functools
import math

import jax
import jax.numpy as jnp
from jax import lax
from jax.experimental import pallas as pl
from jax.experimental.pallas import tpu as pltpu

F32 = jnp.float32
BF16 = jnp.bfloat16
I32 = jnp.int32

HEAD_DIM = 128
N_NSA_HEADS = 8
N_NSA_KV = 2
NSA_REP = 4
N_FOX_HEADS = 8
NSA_WIDTH = N_NSA_HEADS * HEAD_DIM
FOX_WIDTH = N_FOX_HEADS * HEAD_DIM
NSA_KV_WIDTH = 2 * N_NSA_KV * HEAD_DIM
CMP_BLOCK = 32
CMP_STRIDE = 16
CMP_SLOTS = CMP_BLOCK // CMP_STRIDE
SLC_BLOCK = 64
N_SELECT = 16
WINDOW = 512
N_BUCKETS = 32
MAX_DISTANCE = 128
PAGE_SIZE = 128
RMS_EPS = 1e-6
NEG_INF = -1e30
FORCE_SCORE = 1e9
ATTN_SCALE = HEAD_DIM ** -0.5
LOG2E = 1.0 / math.log(2.0)

LANES = 128
SUBLANES = 8
VMEM_LIMIT = 56 * 1024 * 1024
GATE_LANES = 3 * N_NSA_HEADS
LOGF_LANE0 = GATE_LANES


def _params(sem):
    return pltpu.CompilerParams(dimension_semantics=sem, vmem_limit_bytes=VMEM_LIMIT)


def _iota(shape, dim):
    return lax.broadcasted_iota(I32, shape, dim)


def _log2(n):
    assert n > 0 and n & (n - 1) == 0
    return n.bit_length() - 1


def _div_pow2(x, n):
    return jnp.right_shift(x, _log2(n))


def _mod_pow2(x, n):
    return jnp.bitwise_and(x, n - 1)


def _dot_nt(a, b):
    return lax.dot_general(a, b, (((1,), (1,)), ((), ())), preferred_element_type=F32)


def _split3(x):
    hi = x.astype(BF16)
    r1 = x - hi.astype(F32)
    mid = r1.astype(BF16)
    lo = (r1 - mid.astype(F32)).astype(BF16)
    return hi, mid, lo


def _dot01_right(x, m01):
    hi, mid, lo = _split3(x)
    d = lambda a: jnp.dot(a, m01, preferred_element_type=F32)
    return d(hi) + d(mid) + d(lo)


def _dot01_left(m01, x):
    hi, mid, lo = _split3(x)
    d = lambda a: jnp.dot(m01, a, preferred_element_type=F32)
    return d(hi) + d(mid) + d(lo)


def _dot01_nt(m01, x):
    hi, mid, lo = _split3(x)
    return _dot_nt(m01, hi) + _dot_nt(m01, mid) + _dot_nt(m01, lo)


def _t5_bucket(dist):
    n = jnp.maximum(dist, 0)
    exact = N_BUCKETS // 2
    nf = jnp.maximum(n, 1).astype(F32)
    far = exact + (jnp.log(nf / exact) / math.log(MAX_DISTANCE / exact) * (N_BUCKETS - exact)).astype(I32)
    return jnp.where(n < exact, n, jnp.minimum(far, N_BUCKETS - 1))


def _t5_lookup(bucket, table_ref, head):
    acc = jnp.zeros(bucket.shape, F32)
    for k in range(N_BUCKETS):
        acc = jnp.where(bucket == k, table_ref[k, head], acc)
    return acc


def _online_update(s, mask, v, m_s, l_s, acc_s):
    if mask is not None:
        s = jnp.where(mask, s, NEG_INF)
    m_old = m_s[...]
    m_new = jnp.maximum(m_old, jnp.max(s, axis=-1, keepdims=True))
    p = jnp.exp(s - m_new)
    if mask is not None:
        p = jnp.where(mask, p, 0.0)
    alpha = jnp.exp(m_old - m_new)
    l_s[...] = alpha * l_s[...] + jnp.sum(p, axis=-1, keepdims=True)
    acc_s[...] = alpha * acc_s[...] + jnp.dot(p.astype(BF16), v, preferred_element_type=F32)
    m_s[...] = m_new


def _init_state(m_s, l_s, acc_s):
    m_s[...] = jnp.full(m_s.shape, NEG_INF, F32)
    l_s[...] = jnp.zeros(l_s.shape, F32)
    acc_s[...] = jnp.zeros(acc_s.shape, F32)


def _t5_bias_body(table_ref, dist_ref, out_ref):
    bucket = _t5_bucket(dist_ref[...])
    for h in range(N_NSA_HEADS):
        out_ref[h] = _t5_lookup(bucket, table_ref, h)


def _t5_bias(table, dist):
    rows, cols = dist.shape
    tr = min(rows, 256)
    assert rows % tr == 0
    return pl.pallas_call(
        _t5_bias_body,
        grid=(rows // tr,),
        in_specs=[pl.BlockSpec(memory_space=pltpu.SMEM),
                  pl.BlockSpec((tr, cols), lambda i: (i, 0))],
        out_specs=pl.BlockSpec((N_NSA_HEADS, tr, cols), lambda i: (0, i, 0)),
        out_shape=jax.ShapeDtypeStruct((N_NSA_HEADS, rows, cols), F32),
        compiler_params=_params(("arbitrary",)),
        name="t5_bias",
    )(table, dist)


_PROJ_WIDTHS = (NSA_WIDTH, NSA_KV_WIDTH, NSA_KV_WIDTH, NSA_KV_WIDTH, FOX_WIDTH, 2 * FOX_WIDTH)
_PROJ_COL_CHUNK = 512


def _proj_body(x_ref, g_ref, wm_ref, ws_ref, bf_ref, q_ref, ckv_ref, skv_ref, wkv_ref, qf_ref, kvf_ref, sm_ref):
    x = x_ref[...]
    xn = x * lax.rsqrt(jnp.mean(x * x, axis=-1, keepdims=True) + RMS_EPS) * g_ref[...]
    xb = xn.astype(BF16)
    tm = x.shape[0]
    planes = NSA_KV_WIDTH // HEAD_DIM
    off = 0
    for ref, width in zip((q_ref, ckv_ref, skv_ref, wkv_ref, qf_ref, kvf_ref), _PROJ_WIDTHS):
        for c in range(0, width, _PROJ_COL_CHUNK):
            y = jnp.dot(xb, wm_ref[:, off + c:off + c + _PROJ_COL_CHUNK], preferred_element_type=F32)
            if ref.shape[-1] == width:
                ref[:, c:c + _PROJ_COL_CHUNK] = y
            else:
                for cg in range(planes):
                    ref[pl.ds(cg, tm, stride=planes), :] = y[:, cg * HEAD_DIM:(cg + 1) * HEAD_DIM]
        off += width
    s = jnp.dot(xb, ws_ref[...], preferred_element_type=F32)
    z = s + bf_ref[...]
    lane = _iota(s.shape, 1)
    logf = jnp.minimum(z, 0.0) - jnp.log1p(jnp.exp(-jnp.abs(z)))
    sig = 1.0 / (1.0 + jnp.exp(-s))
    sm_ref[...] = jnp.where(lane < GATE_LANES, sig, logf)


def _project(x2d, g_attn, wm, ws, bf128):
    n, d = x2d.shape
    tm = min(n, 256)
    assert n % tm == 0
    const = lambda shape: pl.BlockSpec(shape, lambda i: (0, 0), pipeline_mode=pl.Buffered(1))
    assert _PROJ_COL_CHUNK == NSA_KV_WIDTH
    planes = NSA_KV_WIDTH // HEAD_DIM
    shapes = [(n, NSA_WIDTH)] + [(n * planes, HEAD_DIM)] * 3 + [(n, FOX_WIDTH), (n, 2 * FOX_WIDTH), (n, LANES)]
    return pl.pallas_call(
        _proj_body,
        grid=(n // tm,),
        in_specs=[pl.BlockSpec((tm, d), lambda i: (i, 0)), const((1, d)), const(wm.shape), const(ws.shape),
                  const((1, LANES))],
        out_specs=[pl.BlockSpec((r // (n // tm), w), lambda i: (i, 0)) for r, w in shapes],
        out_shape=[jax.ShapeDtypeStruct(sh, F32) for sh in shapes],
        compiler_params=_params(("arbitrary",)),
        name="in_proj",
    )(x2d, g_attn.reshape(1, d), wm, ws, bf128)


_CHUNK_K = CMP_STRIDE * HEAD_DIM


def _chunkproj_compute(get_rows, pe_ref, w_ref, out_ref):
    for c in range(2):
        w = w_ref[c]
        pos = jnp.dot(pe_ref[c].astype(BF16), w, preferred_element_type=F32)
        for g in range(N_NSA_KV):
            y = jnp.dot(get_rows(c, g).astype(BF16), w, preferred_element_type=F32)
            for u in range(CMP_SLOTS):
                col = ((u * 2 + c) * N_NSA_KV + g) * HEAD_DIM
                out_ref[0, :, col:col + HEAD_DIM] = (
                    y[:, u * HEAD_DIM:(u + 1) * HEAD_DIM] + pos[u:u + 1, u * HEAD_DIM:(u + 1) * HEAD_DIM])


def _chunkproj_s_body(pt_ref, *refs, n_pg):
    page_refs = refs[:n_pg]
    pe_ref, w_ref, out_ref = refs[n_pg:]
    cpp = PAGE_SIZE // CMP_STRIDE
    planes = 2 * N_NSA_KV

    def get_rows(c, g):
        per_r = [jnp.concatenate([pr[0, pl.ds(r * planes + c * N_NSA_KV + g, cpp, stride=CMP_STRIDE * planes), :]
                                  for pr in page_refs], axis=0) for r in range(CMP_STRIDE)]
        return jnp.concatenate(per_r, axis=1)
    _chunkproj_compute(get_rows, pe_ref, w_ref, out_ref)


def _chunkproj_paged(rows, page_table, pe8, wc):
    batch, n_pages = page_table.shape
    cpp = PAGE_SIZE // CMP_STRIDE
    n_pg = math.gcd(n_pages, 32)
    page_rows = PAGE_SIZE * 2 * N_NSA_KV
    x = rows.reshape(-1, page_rows, HEAD_DIM)
    out_w = CMP_SLOTS * NSA_KV_WIDTH

    def page_spec(k):
        return pl.BlockSpec((1, page_rows, HEAD_DIM),
                            lambda b, j, pt: (pt[b * n_pages + j * n_pg + k], 0, 0))
    grid_spec = pltpu.PrefetchScalarGridSpec(
        num_scalar_prefetch=1,
        grid=(batch, n_pages // n_pg),
        in_specs=[page_spec(k) for k in range(n_pg)] + [
            pl.BlockSpec(pe8.shape, lambda b, j, pt: (0, 0, 0)),
            pl.BlockSpec(wc.shape, lambda b, j, pt: (0, 0, 0))],
        out_specs=pl.BlockSpec((1, n_pg * cpp, out_w), lambda b, j, pt: (b, j, 0)),
    )
    return pl.pallas_call(
        functools.partial(_chunkproj_s_body, n_pg=n_pg),
        grid_spec=grid_spec,
        out_shape=jax.ShapeDtypeStruct((batch, n_pages * cpp, out_w), F32),
        compiler_params=_params(("arbitrary", "arbitrary")),
        name="chunkproj_paged",
    )(page_table.reshape(-1), *([x] * n_pg), pe8, wc)


def _cmp_kv(p00, p10, p01, p11):
    n = p00.shape[1]
    k = p00[0] + pltpu.roll(p10[0], n - 1, 0)
    v = p01[0] + pltpu.roll(p11[0], n - 1, 0)
    return k.astype(BF16), v.astype(BF16)


def _p_specs(n_chunks, index):
    def spec(u, c):
        return pl.BlockSpec((1, n_chunks, HEAD_DIM),
                            lambda *a: (index(*a)[0], 0, (u * 2 + c) * N_NSA_KV + index(*a)[1]))
    return [spec(0, 0), spec(1, 0), spec(0, 1), spec(1, 1)]


def _cmp_attn_p_body(q_ref, p00, p10, p01, p11, bias_ref, oc_ref, sel_ref, *, tq, n_blk, n_slc, n_sel):
    i = pl.program_id(2)
    n_chunks = p00.shape[1]
    k, v = _cmp_kv(p00, p10, p01, p11)
    t = i * tq + _iota((tq, n_chunks), 0)
    n = _iota((tq, n_chunks), 1)
    mask = (n * CMP_STRIDE + (CMP_BLOCK - 1) <= t) & (n < n_blk)
    psum = jnp.zeros((tq, n_chunks), F32)
    for r in range(NSA_REP):
        qr = q_ref[0, :, r * HEAD_DIM:(r + 1) * HEAD_DIM].astype(BF16)
        s = _dot_nt(qr, k) * ATTN_SCALE + bias_ref[r]
        s = jnp.where(mask, s, NEG_INF)
        e = jnp.where(mask, jnp.exp(s - jnp.max(s, axis=-1, keepdims=True)), 0.0)
        l = jnp.sum(e, axis=-1, keepdims=True)
        p = e / jnp.where(l > 0.0, l, 1.0)
        oc_ref[0, :, r * HEAD_DIM:(r + 1) * HEAD_DIM] = jnp.dot(p.astype(BF16), v, preferred_element_type=F32)
        psum = psum + p
    sb = _iota((LANES, n_chunks), 0)
    nb = _iota((LANES, n_chunks), 1)
    cover = ((nb * CMP_STRIDE < sb * SLC_BLOCK + SLC_BLOCK) & (nb * CMP_STRIDE + CMP_BLOCK > sb * SLC_BLOCK))
    imp_t = _dot01_nt(cover.astype(BF16), psum)
    blk = _iota((LANES, tq), 0)
    cur = _div_pow2(i * tq + _iota((LANES, tq), 1), SLC_BLOCK)
    forced = (blk == 0) | (blk == cur) | (blk == cur - 1)
    score = jnp.where(forced, FORCE_SCORE, jnp.where(blk <= cur, imp_t, -FORCE_SCORE))
    rank = jnp.zeros((LANES, tq), F32)
    for s2 in range(n_slc):
        row = score[s2:s2 + 1, :]
        ahead = (row > score) | ((row == score) & (s2 < blk))
        rank = rank + ahead.astype(F32)
    sel_t = ((rank < n_sel) & (blk < n_slc)).astype(F32)
    sel_ref[0, 0] = sel_t


def _cmp_attn_prompt(q, pchunks, bias_c, batch, seq):
    tq = 256
    n_chunks = pchunks.shape[1]
    n_blk = n_chunks - CMP_SLOTS + 1
    n_slc = -(-seq // SLC_BLOCK)
    assert n_chunks == LANES and n_slc <= LANES and seq % tq == 0
    q3 = q.reshape(batch, seq, NSA_WIDTH)
    gw = NSA_REP * HEAD_DIM
    body = functools.partial(_cmp_attn_p_body, tq=tq, n_blk=n_blk, n_slc=n_slc, n_sel=min(N_SELECT, n_slc))
    return pl.pallas_call(
        body,
        grid=(batch, N_NSA_KV, seq // tq),
        in_specs=[pl.BlockSpec((1, tq, gw), lambda b, g, i: (b, i, g))]
        + _p_specs(n_chunks, lambda b, g, i: (b, g))
        + [pl.BlockSpec((NSA_REP, tq, n_chunks), lambda b, g, i: (g, i, 0))],
        out_specs=[pl.BlockSpec((1, tq, gw), lambda b, g, i: (b, i, g)),
                   pl.BlockSpec((1, 1, LANES, tq), lambda b, g, i: (b, g, 0, i))],
        out_shape=[jax.ShapeDtypeStruct((batch, seq, NSA_WIDTH), F32),
                   jax.ShapeDtypeStruct((batch, N_NSA_KV, LANES, seq), F32)],
        compiler_params=_params(("arbitrary",) * 3),
        name="cmp_attn_prompt",
    )(q3, pchunks, pchunks, pchunks, pchunks, bias_c)


_TK = 256


def _flash_step_t(xs, consts, vt, m_s, l_s, acc_s, tq):
    ps, alphas = [], []
    for r, x in enumerate(xs):
        cols = slice(r * tq, (r + 1) * tq)
        m_old = m_s[:, cols]
        m_new = jnp.maximum(m_old, jnp.max(x, axis=0, keepdims=True) + consts[r])
        p = jnp.exp2(x - (m_new - consts[r]))
        alpha = jnp.exp2(m_old - m_new)
        l_s[:, cols] = alpha * l_s[:, cols] + jnp.sum(p, axis=0, keepdims=True)
        m_s[:, cols] = m_new
        ps.append(p.astype(BF16))
        alphas.append(alpha)
    pv = jnp.dot(vt, jnp.concatenate(ps, axis=1), preferred_element_type=F32)
    acc_s[...] = jnp.concatenate(alphas, axis=1) * acc_s[...] + pv


def _nsa_flash_body(table_ref, q_ref, kv_ref, tz_ref, *rest, mode, tq, seq):
    if mode == "slc":
        sel_ref, o_ref, m_s, l_s, acc_s, msk_s = rest
    else:
        o_ref, m_s, l_s, acc_s = rest
    g = pl.program_id(1)
    i = pl.program_id(2)
    qb = jnp.concatenate([q_ref[0, :, r * HEAD_DIM:(r + 1) * HEAD_DIM] * (ATTN_SCALE * LOG2E)
                          for r in range(NSA_REP)], axis=0).astype(BF16)
    _init_state(m_s, l_s, acc_s)
    planes = NSA_KV_WIDTH // HEAD_DIM
    if mode == "slc":
        selb = sel_ref[0, 0].astype(BF16)
        kb = _iota((_TK, LANES), 0)
        sb = _iota((_TK, LANES), 1)
        for jj in range(seq // _TK):
            expand = (sb == _div_pow2(jj * _TK + kb, SLC_BLOCK)).astype(BF16)
            hit = jnp.dot(expand, selb, preferred_element_type=F32)
            msk_s[jj] = (hit - 1.0) * (-NEG_INF)
    kj = _iota((_TK, tq), 0)
    ti = _iota((_TK, tq), 1)
    causal = jnp.where(ti >= kj, 0.0, NEG_INF)
    far_consts = [table_ref[N_BUCKETS - 1, g * NSA_REP + r] * LOG2E for r in range(NSA_REP)]

    def logits(j, kind):
        row0 = j * (_TK * planes) + g
        kt = kv_ref[0, pl.ds(row0, _TK, stride=planes), :].astype(BF16)
        vt = kv_ref[0, pl.ds(row0 + N_NSA_KV, _TK, stride=planes), :].T.astype(BF16)
        x = _dot_nt(kt, qb)
        add = None
        if mode == "slc":
            add = msk_s[j]
        elif kind == "far":
            add = jnp.where((i - j) * _TK + ti - kj <= WINDOW, 0.0, NEG_INF)
        if kind == "diag":
            add = causal if add is None else add + causal
        xs = []
        for r in range(NSA_REP):
            xr = x[:, r * tq:(r + 1) * tq]
            if kind == "prev":
                xr = xr + tz_ref[r, :, tq:2 * tq] * LOG2E
            elif kind == "diag":
                xr = xr + tz_ref[r, :, 0:tq] * LOG2E
            if add is not None:
                xr = xr + add
            xs.append(xr)
        return xs, (far_consts if kind == "far" else [0.0] * NSA_REP), vt

    def run(blocks):
        for xs, consts, vt in [logits(j, kind) for j, kind in blocks]:
            _flash_step_t(xs, consts, vt, m_s, l_s, acc_s, tq)

    lo = 0 if mode == "slc" else jnp.maximum(i - WINDOW // _TK, 0)
    n_far = jnp.maximum(i - 1 - lo, 0)

    def far_pair(p, carry):
        run([(lo + 2 * p, "far"), (lo + 2 * p + 1, "far")])
        return carry
    lax.fori_loop(0, jnp.right_shift(n_far, 1), far_pair, 0)

    @pl.when(jnp.bitwise_and(n_far, 1) == 1)
    def _():
        run([(lo + n_far - 1, "far")])

    @pl.when(i >= 1)
    def _():
        run([(i - 1, "prev"), (i, "diag")])

    @pl.when(i == 0)
    def _():
        run([(i, "diag")])
    o_t = acc_s[...] / l_s[...]
    for r in range(NSA_REP):
        o_ref[0, :, r * HEAD_DIM:(r + 1) * HEAD_DIM] = o_t[:, r * tq:(r + 1) * tq].T


def _nsa_flash_prompt(mode, table, q, kv, tz, sel, batch, seq):
    tq = _TK
    assert _TK >= MAX_DISTANCE and WINDOW % _TK == 0 and seq % _TK == 0
    gw = NSA_REP * HEAD_DIM
    q3 = q.reshape(batch, seq, NSA_WIDTH)
    planes = NSA_KV_WIDTH // HEAD_DIM
    kv3 = kv.reshape(batch, seq * planes, HEAD_DIM)
    cols = NSA_REP * tq
    in_specs = [pl.BlockSpec(memory_space=pltpu.SMEM),
                pl.BlockSpec((1, tq, gw), lambda b, g, i: (b, i, g)),
                pl.BlockSpec((1, seq * planes, HEAD_DIM), lambda b, g, i: (b, 0, 0)),
                pl.BlockSpec((NSA_REP, _TK, 2 * tq), lambda b, g, i: (g, 0, 0))]
    args = [table, q3, kv3, tz]
    scratch = [pltpu.VMEM((1, cols), F32), pltpu.VMEM((1, cols), F32), pltpu.VMEM((HEAD_DIM, cols), F32)]
    if mode == "slc":
        in_specs.append(pl.BlockSpec((1, 1, LANES, tq), lambda b, g, i: (b, g, 0, i)))
        args.append(sel)
        scratch.append(pltpu.VMEM((seq // _TK, _TK, tq), F32))
    return pl.pallas_call(
        functools.partial(_nsa_flash_body, mode=mode, tq=tq, seq=seq),
        grid=(batch, N_NSA_KV, seq // tq),
        in_specs=in_specs,
        out_specs=pl.BlockSpec((1, tq, gw), lambda b, g, i: (b, i, g)),
        out_shape=jax.ShapeDtypeStruct((batch, seq, NSA_WIDTH), F32),
        scratch_shapes=scratch,
        compiler_params=_params(("arbitrary",) * 3),
        name="nsa_flash_" + mode,
    )(*args)


def _cumsum_body(sm_ref, c_ref, ct_ref):
    seq = sm_ref.shape[1]
    lane = _iota((LANES, LANES), 1)
    tri = (_iota((LANES, LANES), 0) >= lane).astype(BF16)
    keep = (lane >= LOGF_LANE0) & (lane < LOGF_LANE0 + N_FOX_HEADS)
    carry = jnp.zeros((1, LANES), F32)
    for blk in range(seq // LANES):
        x = jnp.where(keep, sm_ref[0, blk * LANES:(blk + 1) * LANES, :], 0.0)
        cb = _dot01_left(tri, x) + carry
        c_ref[0, blk * LANES:(blk + 1) * LANES, :] = cb
        ct_ref[0, :, blk * LANES:(blk + 1) * LANES] = cb.T
        carry = cb[LANES - 1:LANES, :]


def _cumsum_prompt(sm, batch, seq):
    sm3 = sm.reshape(batch, seq, LANES)
    return pl.pallas_call(
        _cumsum_body,
        grid=(batch,),
        in_specs=[pl.BlockSpec((1, seq, LANES), lambda b: (b, 0, 0))],
        out_specs=[pl.BlockSpec((1, seq, LANES), lambda b: (b, 0, 0)),
                   pl.BlockSpec((1, LANES, seq), lambda b: (b, 0, 0))],
        out_shape=[jax.ShapeDtypeStruct((batch, seq, LANES), F32),
                   jax.ShapeDtypeStruct((batch, LANES, seq), F32)],
        compiler_params=_params(("arbitrary",)),
        name="logf_cumsum",
    )(sm3)


_FOX_T = 512


def _fox_flash_body(q_ref, k_ref, v_ref, c_ref, crow_ref, o_ref, m_s, l_s, acc_s, *, tq):
    h = pl.program_id(1)
    i = pl.program_id(2)
    qb = (q_ref[0] * (ATTN_SCALE * LOG2E)).astype(BF16)
    _init_state(m_s, l_s, acc_s)
    c_q = crow_ref[0, 0, pl.ds(i, 1), :] * LOG2E
    head_lane = _iota((tq, LANES), 1) == LOGF_LANE0 + h
    causal = jnp.where(_iota((tq, tq), 1) >= _iota((tq, tq), 0), 0.0, NEG_INF)

    def logits(j, diag):
        start = pl.multiple_of(j * tq, tq)
        kt = k_ref[0, pl.ds(start, tq), :].astype(BF16)
        vt = v_ref[0, pl.ds(start, tq), :].T.astype(BF16)
        c_k = jnp.sum(jnp.where(head_lane, c_ref[0, pl.ds(start, tq), :], 0.0), axis=-1, keepdims=True)
        x = _dot_nt(kt, qb) - c_k * LOG2E
        if diag:
            x = x + causal
        return x, vt

    def step(x, vt):
        m_old = m_s[...]
        m_new = jnp.maximum(m_old, jnp.max(x, axis=0, keepdims=True) + c_q)
        p = jnp.exp2(x - (m_new - c_q))
        alpha = jnp.exp2(m_old - m_new)
        l_s[...] = alpha * l_s[...] + jnp.sum(p, axis=0, keepdims=True)
        acc_s[...] = alpha * acc_s[...] + jnp.dot(vt, p.astype(BF16), preferred_element_type=F32)
        m_s[...] = m_new

    def run(blocks):
        for x, vt in [logits(j, diag) for j, diag in blocks]:
            step(x, vt)

    def far_pair(p, carry):
        run([(2 * p, False), (2 * p + 1, False)])
        return carry
    lax.fori_loop(0, jnp.right_shift(i, 1), far_pair, 0)

    @pl.when(jnp.bitwise_and(i, 1) == 1)
    def _():
        run([(i - 1, False), (i, True)])

    @pl.when(jnp.bitwise_and(i, 1) == 0)
    def _():
        run([(i, True)])
    o_ref[0] = (acc_s[...] / l_s[...]).T


def _fox_flash_prompt(qf, kvf, c, c_rows, batch, seq):
    tq = _FOX_T
    assert seq % tq == 0
    q3 = qf.reshape(batch, seq, FOX_WIDTH)
    kv3 = kvf.reshape(batch, seq, 2 * FOX_WIDTH)
    return pl.pallas_call(
        functools.partial(_fox_flash_body, tq=tq),
        grid=(batch, N_FOX_HEADS, seq // tq),
        in_specs=[pl.BlockSpec((1, tq, HEAD_DIM), lambda b, h, i: (b, i, h)),
                  pl.BlockSpec((1, seq, HEAD_DIM), lambda b, h, i: (b, 0, h)),
                  pl.BlockSpec((1, seq, HEAD_DIM), lambda b, h, i: (b, 0, N_FOX_HEADS + h)),
                  pl.BlockSpec((1, seq, LANES), lambda b, h, i: (b, 0, 0)),
                  pl.BlockSpec((1, 1, seq // tq, tq), lambda b, h, i: (b, h, 0, 0))],
        out_specs=pl.BlockSpec((1, tq, HEAD_DIM), lambda b, h, i: (b, i, h)),
        out_shape=jax.ShapeDtypeStruct((batch, seq, FOX_WIDTH), F32),
        scratch_shapes=[pltpu.VMEM((1, tq), F32), pltpu.VMEM((1, tq), F32), pltpu.VMEM((HEAD_DIM, tq), F32)],
        compiler_params=_params(("arbitrary",) * 3),
        name="fox_flash_prompt",
    )(q3, kv3, kv3, c, c_rows)


def _rms(x, g):
    return x * lax.rsqrt(jnp.mean(x * x, axis=-1, keepdims=True) + RMS_EPS) * g


_OUTPROJ_SUB = 256


def _outproj_body(x_ref, oc_ref, os_ref, ow_ref, of_ref, sm_ref, gn_ref, gf_ref, wo_ref, h_ref):
    tm = x_ref.shape[0]
    for r0 in range(0, tm, _OUTPROJ_SUB):
        rows = slice(r0, min(r0 + _OUTPROJ_SUB, tm))
        gates = sm_ref[rows, :]
        parts = []
        for hh in range(N_NSA_HEADS):
            sl = slice(hh * HEAD_DIM, (hh + 1) * HEAD_DIM)
            parts.append(gates[:, hh:hh + 1] * oc_ref[rows, sl]
                         + gates[:, N_NSA_HEADS + hh:N_NSA_HEADS + hh + 1] * os_ref[rows, sl]
                         + gates[:, 2 * N_NSA_HEADS + hh:2 * N_NSA_HEADS + hh + 1] * ow_ref[rows, sl])
        o_nsa = jnp.concatenate(parts, axis=1)
        mixed = jnp.concatenate([_rms(o_nsa, gn_ref[...]), _rms(of_ref[rows, :], gf_ref[...])],
                                axis=1).astype(BF16)
        h_ref[rows, :] = x_ref[rows, :] + jnp.dot(mixed, wo_ref[...], preferred_element_type=F32)


def _outproj(x2d, o_c, o_s, o_w, o_f, sm, g_nsa, g_fox, wo):
    n, d = x2d.shape
    tm = min(n, 2 * _OUTPROJ_SUB)
    assert n % tm == 0
    row = lambda w: pl.BlockSpec((tm, w), lambda i: (i, 0))
    const = lambda shape: pl.BlockSpec(shape, lambda i: (0, 0), pipeline_mode=pl.Buffered(1))
    return pl.pallas_call(
        _outproj_body,
        grid=(n // tm,),
        in_specs=[row(d), row(NSA_WIDTH), row(NSA_WIDTH), row(NSA_WIDTH), row(FOX_WIDTH), row(LANES),
                  const((1, NSA_WIDTH)), const((1, FOX_WIDTH)), const(wo.shape)],
        out_specs=row(d),
        out_shape=jax.ShapeDtypeStruct((n, d), F32),
        compiler_params=_params(("arbitrary",)),
        name="out_proj",
    )(x2d, o_c, o_s, o_w, o_f, sm, g_nsa.reshape(1, -1), g_fox.reshape(1, -1), wo)


def _mlp_body(h_ref, gm_ref, gfin_ref, wu_ref, wd_ref, y_ref, xn_s, acc_s):
    j = pl.program_id(1)

    @pl.when(j == 0)
    def _():
        xn_s[...] = _rms(h_ref[...], gm_ref[...]).astype(BF16)
        acc_s[...] = jnp.zeros(acc_s.shape, F32)
    u = jnp.maximum(jnp.dot(xn_s[...], wu_ref[...], preferred_element_type=F32), 0.0)
    acc_s[...] += jnp.dot((u * u).astype(BF16), wd_ref[...], preferred_element_type=F32)

    @pl.when(j == pl.num_programs(1) - 1)
    def _():
        y_ref[...] = _rms(h_ref[...] + acc_s[...], gfin_ref[...])


def _mlp(h, g_mlp, g_final, wu, wd):
    n, d = h.shape
    dff = wu.shape[1]
    tm = min(n, 512)
    tf = 1024
    return pl.pallas_call(
        _mlp_body,
        grid=(n // tm, dff // tf),
        in_specs=[pl.BlockSpec((tm, d), lambda i, j: (i, 0)),
                  pl.BlockSpec((1, d), lambda i, j: (0, 0)),
                  pl.BlockSpec((1, d), lambda i, j: (0, 0)),
                  pl.BlockSpec((d, tf), lambda i, j: (0, j)),
                  pl.BlockSpec((tf, d), lambda i, j: (j, 0))],
        out_specs=pl.BlockSpec((tm, d), lambda i, j: (i, 0)),
        out_shape=jax.ShapeDtypeStruct((n, d), F32),
        scratch_shapes=[pltpu.VMEM((tm, d), BF16), pltpu.VMEM((tm, d), F32)],
        compiler_params=_params(("arbitrary", "arbitrary")),
        name="mlp_final",
    )(h, g_mlp.reshape(1, d), g_final.reshape(1, d), wu, wd)


def _cmp_attn_s_body(q_ref, p00, p10, p01, p11, bias_ref, oc_ref, imp_ref, *, past, n_blk):
    n_chunks = p00.shape[1]
    t_new = q_ref.shape[1]
    sl = imp_ref.shape[-1]
    k, v = _cmp_kv(p00, p10, p01, p11)
    qb = jnp.concatenate([q_ref[0, :, r * HEAD_DIM:(r + 1) * HEAD_DIM] for r in range(NSA_REP)],
                         axis=0).astype(BF16)
    bias = jnp.concatenate([bias_ref[r] for r in range(NSA_REP)], axis=0)
    rows = NSA_REP * t_new
    t = past + _mod_pow2(_iota((rows, n_chunks), 0), t_new)
    n = _iota((rows, n_chunks), 1)
    mask = (n * CMP_STRIDE + (CMP_BLOCK - 1) <= t) & (n < n_blk)
    s = jnp.where(mask, _dot_nt(qb, k) * ATTN_SCALE + bias, NEG_INF)
    e = jnp.where(mask, jnp.exp(s - jnp.max(s, axis=-1, keepdims=True)), 0.0)
    l = jnp.sum(e, axis=-1, keepdims=True)
    p = e / jnp.where(l > 0.0, l, 1.0)
    o = jnp.dot(p.astype(BF16), v, preferred_element_type=F32)
    psum = jnp.zeros((t_new, n_chunks), F32)
    for r in range(NSA_REP):
        oc_ref[0, :, r * HEAD_DIM:(r + 1) * HEAD_DIM] = o[r * t_new:(r + 1) * t_new]
        psum = psum + p[r * t_new:(r + 1) * t_new]
    nb = _iota((n_chunks, sl), 0)
    sb = _iota((n_chunks, sl), 1)
    cover = ((nb * CMP_STRIDE < sb * SLC_BLOCK + SLC_BLOCK) & (nb * CMP_STRIDE + CMP_BLOCK > sb * SLC_BLOCK))
    imp = _dot01_right(psum, cover.astype(BF16))
    imp_ref[0, 0] = imp


def _topk_s_body(imp_ref, sel_ref, *, past, t_new, n_slc, n_sel):
    rows, sl = imp_ref.shape
    blk = _iota((rows, sl), 1)
    cur = _div_pow2(past + _mod_pow2(_iota((rows, sl), 0), t_new), SLC_BLOCK)
    forced = (blk == 0) | (blk == cur) | (blk == cur - 1)
    score = jnp.where(forced, FORCE_SCORE, jnp.where(blk <= cur, imp_ref[...], -FORCE_SCORE))
    score = jnp.where(blk < n_slc, score, -jnp.inf)
    sel = jnp.zeros((rows, sl), F32)
    for _ in range(n_sel):
        mx = jnp.max(score, axis=-1, keepdims=True)
        first = jnp.min(jnp.where(score == mx, blk, sl), axis=-1, keepdims=True)
        hit = blk == first
        sel = jnp.where(hit, 1.0, sel)
        score = jnp.where(hit, -jnp.inf, score)
    sel_ref[...] = sel


def _cmp_attn_sample(q, pchunks, bias_c, batch, t_new, past):
    n_chunks = pchunks.shape[1]
    n_blk = n_chunks - CMP_SLOTS + 1
    n_slc = -(-(past + t_new) // SLC_BLOCK)
    sl = -(-n_slc // LANES) * LANES
    gw = NSA_REP * HEAD_DIM
    q3 = q.reshape(batch, t_new, NSA_WIDTH)
    body = functools.partial(_cmp_attn_s_body, past=past, n_blk=n_blk)
    o_c, imp = pl.pallas_call(
        body,
        grid=(batch, N_NSA_KV),
        in_specs=[pl.BlockSpec((1, t_new, gw), lambda b, g: (b, 0, g))]
        + _p_specs(n_chunks, lambda b, g: (b, g))
        + [pl.BlockSpec((NSA_REP, t_new, n_chunks), lambda b, g: (g, 0, 0))],
        out_specs=[pl.BlockSpec((1, t_new, gw), lambda b, g: (b, 0, g)),
                   pl.BlockSpec((1, 1, t_new, sl), lambda b, g: (b, g, 0, 0))],
        out_shape=[jax.ShapeDtypeStruct((batch, t_new, NSA_WIDTH), F32),
                   jax.ShapeDtypeStruct((batch, N_NSA_KV, t_new, sl), F32)],
        compiler_params=_params(("arbitrary",) * 2),
        name="cmp_attn_sample",
    )(q3, pchunks, pchunks, pchunks, pchunks, bias_c)
    rows = batch * N_NSA_KV * t_new
    sel = pl.pallas_call(
        functools.partial(_topk_s_body, past=past, t_new=t_new, n_slc=n_slc, n_sel=min(N_SELECT, n_slc)),
        grid=(1,),
        in_specs=[pl.BlockSpec((rows, sl), lambda i: (0, 0))],
        out_specs=pl.BlockSpec((rows, sl), lambda i: (0, 0)),
        out_shape=jax.ShapeDtypeStruct((rows, sl), F32),
        compiler_params=_params(("arbitrary",)),
        name="topk_sample",
    )(imp.reshape(rows, sl))
    return o_c, sel.reshape(batch, N_NSA_KV, t_new, sl)


_GKV = N_NSA_KV * HEAD_DIM


def _dec_tile(qbd, kv, dist, mask, table_ref, full_bias, t_new, m_s, l_s, acc_s):
    s = _dot_nt(qbd, kv[:, :_GKV].astype(BF16)) * ATTN_SCALE
    parts = []
    for hh in range(N_NSA_HEADS):
        rows = slice(hh * t_new, (hh + 1) * t_new)
        if full_bias:
            b = _t5_lookup(_t5_bucket(dist[rows]), table_ref, hh)
        else:
            b = jnp.full((t_new, s.shape[1]), table_ref[N_BUCKETS - 1, hh], F32)
        parts.append(s[rows] + b)
    s = jnp.concatenate(parts, axis=0)
    _online_update(s, mask, kv[:, _GKV:].astype(BF16), m_s, l_s, acc_s)


def _dec_finish(o_ref, t_new, l_s, acc_s):
    o = acc_s[...] / l_s[...]
    for g in range(N_NSA_KV):
        for r in range(NSA_REP):
            hh = g * NSA_REP + r
            o_ref[0, :, hh * HEAD_DIM:(hh + 1) * HEAD_DIM] = o[hh * t_new:(hh + 1) * t_new,
                                                             g * HEAD_DIM:(g + 1) * HEAD_DIM]


def _row_mask(m2, t_new):
    return jnp.concatenate([m2[g * t_new:(g + 1) * t_new] for g in range(N_NSA_KV) for _ in range(NSA_REP)], axis=0)


def _slc_s_body(pt_ref, table_ref, q_ref, *rest, n_pg, past, t_new):
    page_refs = rest[:n_pg]
    selp_ref, knew_ref, vnew_ref, selnew_ref, o_ref, m_s, l_s, acc_s = rest[n_pg:]
    j = pl.program_id(1)
    last = pl.num_programs(1) - 1
    grp_rows = NSA_REP * t_new
    q = (q_ref[0] * ATTN_SCALE).astype(BF16)
    t_pos = past + _iota((t_new, PAGE_SIZE), 0)
    lane = _iota((t_new, PAGE_SIZE), 1)

    @pl.when(j == 0)
    def _():
        _init_state(m_s, l_s, acc_s)

    def update(x, mask, vs):
        x = jnp.where(mask, x, NEG_INF)
        m_old = m_s[...]
        m_new = jnp.maximum(m_old, jnp.max(x, axis=-1, keepdims=True))
        p = jnp.where(mask, jnp.exp(x - m_new), 0.0)
        alpha = jnp.exp(m_old - m_new)
        l_s[...] = alpha * l_s[...] + jnp.sum(p, axis=-1, keepdims=True)
        pb = p.astype(BF16)
        pv = jnp.concatenate([jnp.dot(pb[g * grp_rows:(g + 1) * grp_rows], vs[g], preferred_element_type=F32)
                              for g in range(N_NSA_KV)], axis=0)
        acc_s[...] = alpha * acc_s[...] + pv
        m_s[...] = m_new

    def head_bias(hh, dist):
        return _t5_lookup(_t5_bucket(dist), table_ref, hh)

    dist_last = t_pos - ((j * n_pg + n_pg - 1) * PAGE_SIZE + lane)
    xs, masks, vs = [], [], []
    for g in range(N_NSA_KV):
        plane = lambda c: jnp.concatenate(
            [pr[0, pl.ds(c * N_NSA_KV + g, PAGE_SIZE, stride=2 * N_NSA_KV), :] for pr in page_refs],
            axis=0).astype(BF16)
        x = _dot_nt(q[g * grp_rows:(g + 1) * grp_rows], plane(0))
        vs.append(plane(1))
        for r in range(NSA_REP):
            hh = g * NSA_REP + r
            far = jnp.full((t_new, (n_pg - 1) * PAGE_SIZE), table_ref[N_BUCKETS - 1, hh], F32)
            bias = jnp.concatenate([far, head_bias(hh, dist_last)], axis=1)
            xs.append(x[r * t_new:(r + 1) * t_new] + bias)
        picked = []
        for k in range(n_pg):
            m2 = selp_ref[0, k][g * t_new:(g + 1) * t_new]
            picked.append(jnp.where(lane < SLC_BLOCK, m2[:, 0:1], m2[:, 1:2]))
        masks.extend([jnp.concatenate(picked, axis=1)] * NSA_REP)
    update(jnp.concatenate(xs, axis=0), jnp.concatenate(masks, axis=0) > 0.5, vs)

    @pl.when(j == last)
    def _():
        dist = t_pos - (past + lane)
        xs, masks, vs = [], [], []
        for g in range(N_NSA_KV):
            x = _dot_nt(q[g * grp_rows:(g + 1) * grp_rows], knew_ref[0, g].astype(BF16))
            vs.append(vnew_ref[0, g].astype(BF16))
            for r in range(NSA_REP):
                xs.append(x[r * t_new:(r + 1) * t_new] + head_bias(g * NSA_REP + r, dist))
            sel_g = jnp.broadcast_to(selnew_ref[0][g * t_new:(g + 1) * t_new, 0:1], (t_new, PAGE_SIZE)) > 0.5
            masks.extend([sel_g & (dist >= 0) & (lane < t_new)] * NSA_REP)
        update(jnp.concatenate(xs, axis=0), jnp.concatenate(masks, axis=0), vs)
        o = acc_s[...] / l_s[...]
        for hh in range(N_NSA_HEADS):
            o_ref[0, :, hh * HEAD_DIM:(hh + 1) * HEAD_DIM] = o[hh * t_new:(hh + 1) * t_new]


def _slc_sample(table, page_table, q_rows, cache, selp, knew, vnew, selnew, past, t_new):
    batch, n_pages = page_table.shape
    n_pg = math.gcd(n_pages, 16)
    n_phys = cache.shape[0]
    assert t_new <= PAGE_SIZE and past % PAGE_SIZE == 0 and PAGE_SIZE >= MAX_DISTANCE
    page_rows = PAGE_SIZE * 2 * N_NSA_KV
    x = cache.reshape(n_phys, page_rows, HEAD_DIM)
    rows = N_NSA_HEADS * t_new

    def page_spec(k):
        return pl.BlockSpec((1, page_rows, HEAD_DIM), lambda b, j, pt: (pt[b * n_pages + j * n_pg + k], 0, 0))
    new_spec = pl.BlockSpec((1, N_NSA_KV, PAGE_SIZE, HEAD_DIM), lambda b, j, pt: (b, 0, 0, 0))
    grid_spec = pltpu.PrefetchScalarGridSpec(
        num_scalar_prefetch=1,
        grid=(batch, n_pages // n_pg),
        in_specs=[pl.BlockSpec(memory_space=pltpu.SMEM),
                  pl.BlockSpec((1, rows, HEAD_DIM), lambda b, j, pt: (b, 0, 0))]
        + [page_spec(k) for k in range(n_pg)]
        + [pl.BlockSpec((1, n_pg, N_NSA_KV * t_new, 2), lambda b, j, pt: (b, j, 0, 0)),
           new_spec, new_spec,
           pl.BlockSpec((1, N_NSA_KV * t_new, 2), lambda b, j, pt: (b, 0, 0))],
        out_specs=pl.BlockSpec((1, t_new, NSA_WIDTH), lambda b, j, pt: (b, 0, 0)),
        scratch_shapes=[pltpu.VMEM((rows, 1), F32), pltpu.VMEM((rows, 1), F32), pltpu.VMEM((rows, HEAD_DIM), F32)],
    )
    return pl.pallas_call(
        functools.partial(_slc_s_body, n_pg=n_pg, past=past, t_new=t_new),
        grid_spec=grid_spec,
        out_shape=jax.ShapeDtypeStruct((batch, t_new, NSA_WIDTH), F32),
        compiler_params=_params(("arbitrary", "arbitrary")),
        name="slc_sample",
    )(page_table.reshape(-1), table, q_rows, *([x] * n_pg), selp, knew, vnew, selnew)


def _win_s_body(table_ref, q_ref, kv_ref, o_ref, m_s, l_s, acc_s, *, past, t_new, wb):
    rows = N_NSA_HEADS * t_new
    qbd = q_ref[0]
    t_pos = past + _mod_pow2(_iota((rows, PAGE_SIZE), 0), t_new)
    lane = _iota((rows, PAGE_SIZE), 1)
    _init_state(m_s, l_s, acc_s)
    for k in range(kv_ref.shape[1] // PAGE_SIZE):
        k_pos = past - wb + k * PAGE_SIZE + lane
        dist = t_pos - k_pos
        mask = (dist >= 0) & (dist <= WINDOW) & (k_pos >= 0) & (k * PAGE_SIZE + lane < wb + t_new)
        _dec_tile(qbd, kv_ref[0, k * PAGE_SIZE:(k + 1) * PAGE_SIZE, :], dist, mask, table_ref, True, t_new,
                  m_s, l_s, acc_s)
    _dec_finish(o_ref, t_new, l_s, acc_s)


def _win_sample(table, qbd, win_all_padded, past, t_new, wb):
    batch, n_keys, _ = win_all_padded.shape
    rows = N_NSA_HEADS * t_new
    return pl.pallas_call(
        functools.partial(_win_s_body, past=past, t_new=t_new, wb=wb),
        grid=(batch,),
        in_specs=[pl.BlockSpec(memory_space=pltpu.SMEM),
                  pl.BlockSpec((1, rows, _GKV), lambda b: (b, 0, 0)),
                  pl.BlockSpec((1, n_keys, 2 * _GKV), lambda b: (b, 0, 0))],
        out_specs=pl.BlockSpec((1, t_new, NSA_WIDTH), lambda b: (b, 0, 0)),
        out_shape=jax.ShapeDtypeStruct((batch, t_new, NSA_WIDTH), F32),
        scratch_shapes=[pltpu.VMEM((rows, 1), F32), pltpu.VMEM((rows, 1), F32), pltpu.VMEM((rows, _GKV), F32)],
        compiler_params=_params(("arbitrary",)),
        name="win_sample",
    )(table, qbd, win_all_padded)


_FOX_PAGE_LANES = PAGE_SIZE * N_FOX_HEADS


def _fox_s_body(pt_ref, q_ref, *rest, n_pg, t_new):
    kv_refs = rest[:n_pg]
    lf_refs = rest[n_pg:2 * n_pg]
    tri_ref, knew_ref, vnew_ref, lfn_ref, o_ref, m_s, l_s, acc_s, off_s = rest[2 * n_pg:]
    j = pl.program_id(1)
    rows = N_FOX_HEADS * t_new
    pw = _FOX_PAGE_LANES
    qb = (q_ref[0] * ATTN_SCALE).astype(BF16)
    lane = _iota((rows, PAGE_SIZE), 1)
    t_row = _mod_pow2(_iota((rows, PAGE_SIZE), 0), t_new)
    h_row = _div_pow2(_iota((rows, PAGE_SIZE), 0), t_new)
    lfn = lfn_ref[0]
    c_new = jnp.sum(jnp.where(lane <= t_row, lfn, 0.0), axis=-1, keepdims=True)

    @pl.when(j == 0)
    def _():
        _init_state(m_s, l_s, acc_s)
        off_s[...] = jnp.zeros(off_s.shape, F32)
        u_l = _div_pow2(lane, N_FOX_HEADS)
        bias = jnp.zeros((rows, PAGE_SIZE), F32)
        for u in range(t_new):
            col = jnp.sum(jnp.where((lane > u) & (lane <= t_row), lfn, 0.0), axis=-1, keepdims=True)
            bias = jnp.where(u_l == u, col, bias)
        x = _dot_nt(qb, knew_ref[0].astype(BF16)) + bias
        mask = (u_l <= t_row) & (u_l < t_new) & (_mod_pow2(lane, N_FOX_HEADS) == h_row)
        _online_update(x, mask, vnew_ref[0].astype(BF16), m_s, l_s, acc_s)

    lf = jnp.concatenate([r[0] for r in lf_refs], axis=0)
    later = _dot01_right(lf, tri_ref[...])
    tot = jnp.sum(lf, axis=-1, keepdims=True)
    head_lane = jnp.where(
        _mod_pow2(_iota((rows, pw), 1), N_FOX_HEADS) == _div_pow2(_iota((rows, pw), 0), t_new), 0.0, NEG_INF)
    spread = lambda a: jnp.concatenate(
        [jnp.broadcast_to(a[h:h + 1], (t_new, a.shape[1])) for h in range(N_FOX_HEADS)], axis=0)
    off = off_s[:, 0:1]
    xs, vs = [], []
    for k in range(n_pg):
        kk = kv_refs[k][0, :, 0].reshape(pw, HEAD_DIM).astype(BF16)
        vs.append(kv_refs[k][0, :, 1].reshape(pw, HEAD_DIM).astype(BF16))
        hs = slice(k * N_FOX_HEADS, (k + 1) * N_FOX_HEADS)
        xs.append(_dot_nt(qb, kk) + spread(later[hs]) + (spread(off) + c_new) + head_lane)
        off = off + tot[hs]
    off_s[...] = jnp.broadcast_to(off, off_s.shape)
    x = jnp.concatenate(xs, axis=1)
    m_old = m_s[...]
    m_new = jnp.maximum(m_old, jnp.max(x, axis=-1, keepdims=True))
    p = jnp.exp(x - m_new)
    alpha = jnp.exp(m_old - m_new)
    l_s[...] = alpha * l_s[...] + jnp.sum(p, axis=-1, keepdims=True)
    acc_s[...] = alpha * acc_s[...] + jnp.dot(p.astype(BF16), jnp.concatenate(vs, axis=0),
                                              preferred_element_type=F32)
    m_s[...] = m_new

    @pl.when(j == pl.num_programs(1) - 1)
    def _():
        o = acc_s[...] / l_s[...]
        for h in range(N_FOX_HEADS):
            o_ref[0, :, h * HEAD_DIM:(h + 1) * HEAD_DIM] = o[h * t_new:(h + 1) * t_new]


def _fox_sample(page_table, q_rows, cache, logf_t, tri, knew, vnew, lfn, t_new):
    batch, n_pages = page_table.shape
    n_pg = math.gcd(n_pages, 16)
    rows = N_FOX_HEADS * t_new
    page = lambda k: (lambda b, j, pt: pt[b * n_pages + n_pages - 1 - (j * n_pg + k)])

    def kv_spec(k):
        return pl.BlockSpec((1, PAGE_SIZE, 2, N_FOX_HEADS, HEAD_DIM), lambda b, j, pt: (page(k)(b, j, pt), 0, 0, 0, 0))

    def lf_spec(k):
        return pl.BlockSpec((1, N_FOX_HEADS, PAGE_SIZE), lambda b, j, pt: (page(k)(b, j, pt), 0, 0))
    new_spec = pl.BlockSpec((1, PAGE_SIZE, HEAD_DIM), lambda b, j, pt: (b, 0, 0))
    grid_spec = pltpu.PrefetchScalarGridSpec(
        num_scalar_prefetch=1,
        grid=(batch, n_pages // n_pg),
        in_specs=[pl.BlockSpec((1, rows, HEAD_DIM), lambda b, j, pt: (b, 0, 0))]
        + [kv_spec(k) for k in range(n_pg)] + [lf_spec(k) for k in range(n_pg)]
        + [pl.BlockSpec(tri.shape, lambda b, j, pt: (0, 0)), new_spec, new_spec,
           pl.BlockSpec((1, rows, PAGE_SIZE), lambda b, j, pt: (b, 0, 0))],
        out_specs=pl.BlockSpec((1, t_new, FOX_WIDTH), lambda b, j, pt: (b, 0, 0)),
        scratch_shapes=[pltpu.VMEM((rows, 1), F32), pltpu.VMEM((rows, 1), F32), pltpu.VMEM((rows, HEAD_DIM), F32),
                        pltpu.VMEM((N_FOX_HEADS, LANES), F32)],
    )
    return pl.pallas_call(
        functools.partial(_fox_s_body, n_pg=n_pg, t_new=t_new),
        grid_spec=grid_spec,
        out_shape=jax.ShapeDtypeStruct((batch, t_new, FOX_WIDTH), F32),
        compiler_params=_params(("arbitrary", "arbitrary")),
        name="fox_sample",
    )(page_table.reshape(-1), q_rows, *([cache] * n_pg), *([logf_t] * n_pg), tri, knew, vnew, lfn)


def _pack_weights(w_in, b_f, w_cmp, pe_cmp, w_o, w_up, w_down):
    c_gt = NSA_WIDTH + 3 * NSA_KV_WIDTH
    c_qf = c_gt + GATE_LANES
    c_fl = c_qf + 3 * FOX_WIDTH
    wm = jnp.concatenate([w_in[:, :c_gt], w_in[:, c_qf:c_fl]], axis=1).astype(BF16)
    ws = jnp.concatenate([w_in[:, c_gt:c_qf], w_in[:, c_fl:],
                          jnp.zeros((w_in.shape[0], LANES - GATE_LANES - N_FOX_HEADS), F32)], axis=1).astype(BF16)
    bf128 = jnp.zeros((1, LANES), F32).at[0, LOGF_LANE0:LOGF_LANE0 + N_FOX_HEADS].set(b_f)
    wc = w_cmp.reshape(2, CMP_SLOTS, CMP_STRIDE, HEAD_DIM, HEAD_DIM).transpose(0, 2, 3, 1, 4).reshape(
        2, _CHUNK_K, CMP_SLOTS * HEAD_DIM).astype(BF16)
    pe8 = jnp.zeros((2, SUBLANES, _CHUNK_K), F32).at[:, :CMP_SLOTS].set(pe_cmp.reshape(2, CMP_SLOTS, _CHUNK_K))
    return wm, ws, bf128, wc, pe8, w_o.astype(BF16), w_up.astype(BF16), w_down.astype(BF16)


def _block_diag_q(q, t_new, n_grp, per_grp):
    batch = q.shape[0]
    q5 = q.reshape(batch, t_new, n_grp, per_grp, HEAD_DIM).transpose(0, 2, 3, 1, 4)
    eye = jnp.eye(n_grp, dtype=q.dtype)
    out = q5[:, :, :, :, None, :] * eye[None, :, None, None, :, None]
    return out.reshape(batch, n_grp * per_grp * t_new, n_grp * HEAD_DIM).astype(BF16)


def _prompt_pass(x, table, lw, packed):
    g_attn, g_nsa_out, g_fox_out, g_mlp, g_final = lw
    wm, ws, bf128, wc, pe8, wo, wu, wd = packed
    batch, seq, d = x.shape
    assert seq % 256 == 0 and seq >= WINDOW and seq % PAGE_SIZE == 0
    x2d = x.reshape(batch * seq, d)
    q, ckv, skv, wkv, qf, kvf, sm = _project(x2d, g_attn, wm, ws, bf128)
    own_pages = jnp.arange(batch * (seq // PAGE_SIZE), dtype=I32).reshape(batch, seq // PAGE_SIZE)
    pchunks = _chunkproj_paged(ckv, own_pages, pe8, wc)
    n_chunks = seq // CMP_STRIDE
    dist_c = jnp.arange(seq, dtype=I32)[:, None] - (jnp.arange(n_chunks, dtype=I32)[None, :] * CMP_STRIDE
                                                     + (CMP_BLOCK - 1))
    bias_c = _t5_bias(table, dist_c)
    o_c, sel = _cmp_attn_prompt(q, pchunks, bias_c, batch, seq)
    dist_tz = jnp.arange(2 * _TK, dtype=I32)[None, :] - jnp.arange(_TK, dtype=I32)[:, None]
    tz = _t5_bias(table, dist_tz)
    o_s = _nsa_flash_prompt("slc", table, q, skv, tz, sel, batch, seq)
    o_w = _nsa_flash_prompt("win", table, q, wkv, tz, None, batch, seq)
    c, c_t = _cumsum_prompt(sm, batch, seq)
    c_rows = c_t[:, LOGF_LANE0:LOGF_LANE0 + N_FOX_HEADS, :].reshape(batch, N_FOX_HEADS, seq // _FOX_T, _FOX_T)
    o_f = _fox_flash_prompt(qf, kvf, c, c_rows, batch, seq)
    h = _outproj(x2d, o_c.reshape(-1, NSA_WIDTH), o_s.reshape(-1, NSA_WIDTH), o_w.reshape(-1, NSA_WIDTH),
                 o_f.reshape(-1, FOX_WIDTH), sm, g_nsa_out, g_fox_out, wo)
    y = _mlp(h, g_mlp, g_final, wu, wd).reshape(batch, seq, d)
    kv_shape = (1, batch, seq, 2, N_NSA_KV, HEAD_DIM)
    wkv5 = wkv.reshape(kv_shape)
    logf = sm.reshape(batch, seq, LANES)[:, :, LOGF_LANE0:LOGF_LANE0 + N_FOX_HEADS]
    return (y, ckv.reshape(kv_shape), skv.reshape(kv_shape),
            kvf.reshape(1, batch, seq, 2, N_FOX_HEADS, HEAD_DIM), logf[None],
            wkv5[:, :, seq - min(WINDOW, seq):])


def _sample_pass(x, cache_cmp, cache_slc, cache_fox, cache_logf, win_buf, page_table, table, lw, packed):
    g_attn, g_nsa_out, g_fox_out, g_mlp, g_final = lw
    wm, ws, bf128, wc, pe8, wo, wu, wd = packed
    batch, t_new, d = x.shape
    n_pages = page_table.shape[1]
    past = n_pages * PAGE_SIZE
    assert t_new < CMP_STRIDE and t_new % SUBLANES == 0 and t_new * N_FOX_HEADS <= PAGE_SIZE
    x2d = x.reshape(batch * t_new, d)
    q, ckv, skv, wkv, qf, kvf, sm = _project(x2d, g_attn, wm, ws, bf128)
    pchunks = _chunkproj_paged(cache_cmp, page_table, pe8, wc)
    n_chunks = pchunks.shape[1]
    pos = past + jnp.arange(t_new, dtype=I32)
    dist_c = pos[:, None] - (jnp.arange(n_chunks, dtype=I32)[None, :] * CMP_STRIDE + (CMP_BLOCK - 1))
    bias_c = _t5_bias(table, dist_c)
    o_c, sel = _cmp_attn_sample(q, pchunks, bias_c, batch, t_new, past)
    q5 = q.reshape(batch, t_new, N_NSA_KV, NSA_REP, HEAD_DIM)
    q_rows = q5.transpose(0, 2, 3, 1, 4).reshape(batch, N_NSA_HEADS * t_new, HEAD_DIM)
    n_past_blk = past // SLC_BLOCK
    bpp = PAGE_SIZE // SLC_BLOCK
    selp = sel[..., :n_past_blk].reshape(batch, N_NSA_KV, t_new, n_pages, bpp).transpose(0, 3, 1, 2, 4).reshape(
        batch, n_pages, N_NSA_KV * t_new, bpp)
    selnew = jnp.broadcast_to(sel[..., n_past_blk:n_past_blk + 1].reshape(batch, N_NSA_KV * t_new, 1),
                              (batch, N_NSA_KV * t_new, bpp))
    skv5 = skv.reshape(batch, t_new, 2, N_NSA_KV, HEAD_DIM).transpose(2, 0, 3, 1, 4)
    skv5 = jnp.pad(skv5, ((0, 0), (0, 0), (0, 0), (0, PAGE_SIZE - t_new), (0, 0)))
    o_s = _slc_sample(table, page_table, q_rows, cache_slc, selp, skv5[0], skv5[1], selnew, past, t_new)
    qbd = _block_diag_q(q.reshape(batch, t_new, NSA_WIDTH), t_new, N_NSA_KV, NSA_REP)
    wb = win_buf.shape[1]
    win_all = jnp.concatenate([win_buf.reshape(batch, wb, NSA_KV_WIDTH), wkv.reshape(batch, t_new, NSA_KV_WIDTH)],
                              axis=1)
    n_keys = -(-(wb + t_new) // PAGE_SIZE) * PAGE_SIZE
    o_w = _win_sample(table, qbd, jnp.pad(win_all, ((0, 0), (0, n_keys - wb - t_new), (0, 0))), past, t_new, wb)
    logf_new = sm.reshape(batch, t_new, LANES)[:, :, LOGF_LANE0:LOGF_LANE0 + N_FOX_HEADS]
    logf_t = cache_logf.transpose(0, 2, 1)
    qf_rows = qf.reshape(batch, t_new, N_FOX_HEADS, HEAD_DIM).transpose(0, 2, 1, 3).reshape(
        batch, N_FOX_HEADS * t_new, HEAD_DIM)
    kvf5 = kvf.reshape(batch, t_new, 2, N_FOX_HEADS * HEAD_DIM).transpose(2, 0, 1, 3).reshape(
        2, batch, t_new * N_FOX_HEADS, HEAD_DIM)
    kvf5 = jnp.pad(kvf5, ((0, 0), (0, 0), (0, PAGE_SIZE - t_new * N_FOX_HEADS), (0, 0)))
    lfn = jnp.broadcast_to(logf_new.transpose(0, 2, 1)[:, :, None, :], (batch, N_FOX_HEADS, t_new, t_new))
    lfn = jnp.pad(lfn.reshape(batch, N_FOX_HEADS * t_new, t_new), ((0, 0), (0, 0), (0, PAGE_SIZE - t_new)))
    tri = (jnp.arange(PAGE_SIZE)[:, None] > jnp.arange(_FOX_PAGE_LANES)[None, :] // N_FOX_HEADS).astype(BF16)
    o_f = _fox_sample(page_table, qf_rows, cache_fox, logf_t, tri, kvf5[0], kvf5[1], lfn, t_new)
    h = _outproj(x2d, o_c.reshape(-1, NSA_WIDTH), o_s.reshape(-1, NSA_WIDTH), o_w.reshape(-1, NSA_WIDTH),
                 o_f.reshape(-1, FOX_WIDTH), sm, g_nsa_out, g_fox_out, wo)
    y = _mlp(h, g_mlp, g_final, wu, wd).reshape(batch, t_new, d)
    kv_shape = (1, batch, t_new, 2, N_NSA_KV, HEAD_DIM)
    return (y, ckv.reshape(kv_shape), skv.reshape(kv_shape),
            kvf.reshape(1, batch, t_new, 2, N_FOX_HEADS, HEAD_DIM), logf_new[None],
            win_all[:, t_new:].reshape(1, batch, wb, 2, N_NSA_KV, HEAD_DIM))


def kernel(x_prompt, x_sample, cache_cmp_kv, cache_slc_kv, cache_fox_kv, cache_fox_logf, state_win_kv, page_table,
           t5_table, g_attn, w_in, b_f, w_cmp, pe_cmp, g_nsa_out, g_fox_out, w_o, g_mlp, w_up, w_down, g_final):
    assert g_attn.shape[0] == 1, "single-layer step"
    assert x_prompt.shape[-1] == NSA_WIDTH + FOX_WIDTH
    packed = _pack_weights(w_in[0], b_f[0], w_cmp[0], pe_cmp[0], w_o[0], w_up[0], w_down[0])
    lw = (g_attn[0], g_nsa_out[0], g_fox_out[0], g_mlp[0], g_final)
    yp, cmp_p, slc_p, fox_p, logf_p, win_p = _prompt_pass(x_prompt, t5_table, lw, packed)
    ys, cmp_s, slc_s, fox_s, logf_s, win_s = _sample_pass(
        x_sample, cache_cmp_kv[0], cache_slc_kv[0], cache_fox_kv[0], cache_fox_logf[0], state_win_kv[0],
        page_table, t5_table, lw, packed)
    return (yp, ys, cmp_p, slc_p, fox_p, logf_p, win_p, cmp_s, slc_s, fox_s, logf_s, win_s)
```

```python
import functools
import math

import jax
import jax.numpy as jnp
from jax import lax
from jax.experimental import pallas as pl
from jax.experimental.pallas import tpu as pltpu

F32 = jnp.float32
BF16 = jnp.bfloat16
I32 = jnp.int32

HEAD_DIM = 128
N_NSA_HEADS = 8
N_NSA_KV = 2
NSA_REP = 4
N_FOX_HEADS = 8
NSA_WIDTH = N_NSA_HEADS * HEAD_DIM
FOX_WIDTH = N_FOX_HEADS * HEAD_DIM
NSA_KV_WIDTH = 2 * N_NSA_KV * HEAD_DIM
CMP_BLOCK = 32
CMP_STRIDE = 16
CMP_SLOTS = CMP_BLOCK // CMP_STRIDE
SLC_BLOCK = 64
N_SELECT = 16
WINDOW = 512
N_BUCKETS = 32
MAX_DISTANCE = 128
PAGE_SIZE = 128
RMS_EPS = 1e-6
NEG_INF = -1e30
FORCE_SCORE = 1e9
ATTN_SCALE = HEAD_DIM ** -0.5
LOG2E = 1.0 / math.log(2.0)

LANES = 128
SUBLANES = 8
VMEM_LIMIT = 56 * 1024 * 1024
GATE_LANES = 3 * N_NSA_HEADS
LOGF_LANE0 = GATE_LANES


def _params(sem):
    return pltpu.CompilerParams(dimension_semantics=sem, vmem_limit_bytes=VMEM_LIMIT)


def _iota(shape, dim):
    return lax.broadcasted_iota(I32, shape, dim)


def _log2(n):
    assert n > 0 and n & (n - 1) == 0
    return n.bit_length() - 1


def _div_pow2(x, n):
    return jnp.right_shift(x, _log2(n))


def _mod_pow2(x, n):
    return jnp.bitwise_and(x, n - 1)


def _dot_nt(a, b):
    return lax.dot_general(a, b, (((1,), (1,)), ((), ())), preferred_element_type=F32)


def _split3(x):
    hi = x.astype(BF16)
    r1 = x - hi.astype(F32)
    mid = r1.astype(BF16)
    lo = (r1 - mid.astype(F32)).astype(BF16)
    return hi, mid, lo


def _dot01_right(x, m01):
    hi, mid, lo = _split3(x)
    d = lambda a: jnp.dot(a, m01, preferred_element_type=F32)
    return d(hi) + d(mid) + d(lo)


def _dot01_left(m01, x):
    hi, mid, lo = _split3(x)
    d = lambda a: jnp.dot(m01, a, preferred_element_type=F32)
    return d(hi) + d(mid) + d(lo)


def _dot01_nt(m01, x):
    hi, mid, lo = _split3(x)
    return _dot_nt(m01, hi) + _dot_nt(m01, mid) + _dot_nt(m01, lo)


def _t5_bucket(dist):
    n = jnp.maximum(dist, 0)
    exact = N_BUCKETS // 2
    nf = jnp.maximum(n, 1).astype(F32)
    far = exact + (jnp.log(nf / exact) / math.log(MAX_DISTANCE / exact) * (N_BUCKETS - exact)).astype(I32)
    return jnp.where(n < exact, n, jnp.minimum(far, N_BUCKETS - 1))


def _t5_lookup(bucket, table_ref, head):
    acc = jnp.zeros(bucket.shape, F32)
    for k in range(N_BUCKETS):
        acc = jnp.where(bucket == k, table_ref[k, head], acc)
    return acc


def _online_update(s, mask, v, m_s, l_s, acc_s):
    if mask is not None:
        s = jnp.where(mask, s, NEG_INF)
    m_old = m_s[...]
    m_new = jnp.maximum(m_old, jnp.max(s, axis=-1, keepdims=True))
    p = jnp.exp(s - m_new)
    if mask is not None:
        p = jnp.where(mask, p, 0.0)
    alpha = jnp.exp(m_old - m_new)
    l_s[...] = alpha * l_s[...] + jnp.sum(p, axis=-1, keepdims=True)
    acc_s[...] = alpha * acc_s[...] + jnp.dot(p.astype(BF16), v, preferred_element_type=F32)
    m_s[...] = m_new


def _init_state(m_s, l_s, acc_s):
    m_s[...] = jnp.full(m_s.shape, NEG_INF, F32)
    l_s[...] = jnp.zeros(l_s.shape, F32)
    acc_s[...] = jnp.zeros(acc_s.shape, F32)


def _t5_bias_body(table_ref, dist_ref, out_ref):
    bucket = _t5_bucket(dist_ref[...])
    for h in range(N_NSA_HEADS):
        out_ref[h] = _t5_lookup(bucket, table_ref, h)


def _t5_bias(table, dist):
    rows, cols = dist.shape
    tr = min(rows, 256)
    assert rows % tr == 0
    return pl.pallas_call(
        _t5_bias_body,
        grid=(rows // tr,),
        in_specs=[pl.BlockSpec(memory_space=pltpu.SMEM),
                  pl.BlockSpec((tr, cols), lambda i: (i, 0))],
        out_specs=pl.BlockSpec((N_NSA_HEADS, tr, cols), lambda i: (0, i, 0)),
        out_shape=jax.ShapeDtypeStruct((N_NSA_HEADS, rows, cols), F32),
        compiler_params=_params(("arbitrary",)),
        name="t5_bias",
    )(table, dist)


_PROJ_WIDTHS = (NSA_WIDTH, NSA_KV_WIDTH, NSA_KV_WIDTH, NSA_KV_WIDTH, FOX_WIDTH, 2 * FOX_WIDTH)
_PROJ_COL_CHUNK = 512


def _proj_body(x_ref, g_ref, wm_ref, ws_ref, bf_ref, q_ref, ckv_ref, skv_ref, wkv_ref, qf_ref, kvf_ref, sm_ref):
    x = x_ref[...]
    xn = x * lax.rsqrt(jnp.mean(x * x, axis=-1, keepdims=True) + RMS_EPS) * g_ref[...]
    xb = xn.astype(BF16)
    tm = x.shape[0]
    planes = NSA_KV_WIDTH // HEAD_DIM
    off = 0
    for ref, width in zip((q_ref, ckv_ref, skv_ref, wkv_ref, qf_ref, kvf_ref), _PROJ_WIDTHS):
        for c in range(0, width, _PROJ_COL_CHUNK):
            y = jnp.dot(xb, wm_ref[:, off + c:off + c + _PROJ_COL_CHUNK], preferred_element_type=F32)
            if ref.shape[-1] == width:
                ref[:, c:c + _PROJ_COL_CHUNK] = y
            else:
                for cg in range(planes):
                    ref[pl.ds(cg, tm, stride=planes), :] = y[:, cg * HEAD_DIM:(cg + 1) * HEAD_DIM]
        off += width
    s = jnp.dot(xb, ws_ref[...], preferred_element_type=F32)
    z = s + bf_ref[...]
    lane = _iota(s.shape, 1)
    logf = jnp.minimum(z, 0.0) - jnp.log1p(jnp.exp(-jnp.abs(z)))
    sig = 1.0 / (1.0 + jnp.exp(-s))
    sm_ref[...] = jnp.where(lane < GATE_LANES, sig, logf)


def _project(x2d, g_attn, wm, ws, bf128):
    n, d = x2d.shape
    tm = min(n, 256)
    assert n % tm == 0
    const = lambda shape: pl.BlockSpec(shape, lambda i: (0, 0), pipeline_mode=pl.Buffered(1))
    assert _PROJ_COL_CHUNK == NSA_KV_WIDTH
    planes = NSA_KV_WIDTH // HEAD_DIM
    shapes = [(n, NSA_WIDTH)] + [(n * planes, HEAD_DIM)] * 3 + [(n, FOX_WIDTH), (n, 2 * FOX_WIDTH), (n, LANES)]
    return pl.pallas_call(
        _proj_body,
        grid=(n // tm,),
        in_specs=[pl.BlockSpec((tm, d), lambda i: (i, 0)), const((1, d)), const(wm.shape), const(ws.shape),
                  const((1, LANES))],
        out_specs=[pl.BlockSpec((r // (n // tm), w), lambda i: (i, 0)) for r, w in shapes],
        out_shape=[jax.ShapeDtypeStruct(sh, F32) for sh in shapes],
        compiler_params=_params(("arbitrary",)),
        name="in_proj",
    )(x2d, g_attn.reshape(1, d), wm, ws, bf128)


_CHUNK_K = CMP_STRIDE * HEAD_DIM


def _chunkproj_compute(get_rows, pe_ref, w_ref, out_ref):
    for c in range(2):
        w = w_ref[c]
        pos = jnp.dot(pe_ref[c].astype(BF16), w, preferred_element_type=F32)
        for g in range(N_NSA_KV):
            y = jnp.dot(get_rows(c, g).astype(BF16), w, preferred_element_type=F32)
            for u in range(CMP_SLOTS):
                col = ((u * 2 + c) * N_NSA_KV + g) * HEAD_DIM
                out_ref[0, :, col:col + HEAD_DIM] = (
                    y[:, u * HEAD_DIM:(u + 1) * HEAD_DIM] + pos[u:u + 1, u * HEAD_DIM:(u + 1) * HEAD_DIM])


def _chunkproj_s_body(pt_ref, *refs, n_pg):
    page_refs = refs[:n_pg]
    pe_ref, w_ref, out_ref = refs[n_pg:]
    cpp = PAGE_SIZE // CMP_STRIDE
    planes = 2 * N_NSA_KV

    def get_rows(c, g):
        per_r = [jnp.concatenate([pr[0, pl.ds(r * planes + c * N_NSA_KV + g, cpp, stride=CMP_STRIDE * planes), :]
                                  for pr in page_refs], axis=0) for r in range(CMP_STRIDE)]
        return jnp.concatenate(per_r, axis=1)
    _chunkproj_compute(get_rows, pe_ref, w_ref, out_ref)


def _chunkproj_paged(rows, page_table, pe8, wc):
    batch, n_pages = page_table.shape
    cpp = PAGE_SIZE // CMP_STRIDE
    n_pg = math.gcd(n_pages, 32)
    page_rows = PAGE_SIZE * 2 * N_NSA_KV
    x = rows.reshape(-1, page_rows, HEAD_DIM)
    out_w = CMP_SLOTS * NSA_KV_WIDTH

    def page_spec(k):
        return pl.BlockSpec((1, page_rows, HEAD_DIM),
                            lambda b, j, pt: (pt[b * n_pages + j * n_pg + k], 0, 0))
    grid_spec = pltpu.PrefetchScalarGridSpec(
        num_scalar_prefetch=1,
        grid=(batch, n_pages // n_pg),
        in_specs=[page_spec(k) for k in range(n_pg)] + [
            pl.BlockSpec(pe8.shape, lambda b, j, pt: (0, 0, 0)),
            pl.BlockSpec(wc.shape, lambda b, j, pt: (0, 0, 0))],
        out_specs=pl.BlockSpec((1, n_pg * cpp, out_w), lambda b, j, pt: (b, j, 0)),
    )
    return pl.pallas_call(
        functools.partial(_chunkproj_s_body, n_pg=n_pg),
        grid_spec=grid_spec,
        out_shape=jax.ShapeDtypeStruct((batch, n_pages * cpp, out_w), F32),
        compiler_params=_params(("arbitrary", "arbitrary")),
        name="chunkproj_paged",
    )(page_table.reshape(-1), *([x] * n_pg), pe8, wc)


def _cmp_kv(p00, p10, p01, p11):
    n = p00.shape[1]
    k = p00[0] + pltpu.roll(p10[0], n - 1, 0)
    v = p01[0] + pltpu.roll(p11[0], n - 1, 0)
    return k.astype(BF16), v.astype(BF16)


def _p_specs(n_chunks, index):
    def spec(u, c):
        return pl.BlockSpec((1, n_chunks, HEAD_DIM),
                            lambda *a: (index(*a)[0], 0, (u * 2 + c) * N_NSA_KV + index(*a)[1]))
    return [spec(0, 0), spec(1, 0), spec(0, 1), spec(1, 1)]


def _cmp_attn_p_body(q_ref, p00, p10, p01, p11, bias_ref, oc_ref, sel_ref, *, tq, n_blk, n_slc, n_sel):
    i = pl.program_id(2)
    n_chunks = p00.shape[1]
    k, v = _cmp_kv(p00, p10, p01, p11)
    t = i * tq + _iota((tq, n_chunks), 0)
    n = _iota((tq, n_chunks), 1)
    mask = (n * CMP_STRIDE + (CMP_BLOCK - 1) <= t) & (n < n_blk)
    psum = jnp.zeros((tq, n_chunks), F32)
    for r in range(NSA_REP):
        qr = q_ref[0, :, r * HEAD_DIM:(r + 1) * HEAD_DIM].astype(BF16)
        s = _dot_nt(qr, k) * ATTN_SCALE + bias_ref[r]
        s = jnp.where(mask, s, NEG_INF)
        e = jnp.where(mask, jnp.exp(s - jnp.max(s, axis=-1, keepdims=True)), 0.0)
        l = jnp.sum(e, axis=-1, keepdims=True)
        p = e / jnp.where(l > 0.0, l, 1.0)
        oc_ref[0, :, r * HEAD_DIM:(r + 1) * HEAD_DIM] = jnp.dot(p.astype(BF16), v, preferred_element_type=F32)
        psum = psum + p
    sb = _iota((LANES, n_chunks), 0)
    nb = _iota((LANES, n_chunks), 1)
    cover = ((nb * CMP_STRIDE < sb * SLC_BLOCK + SLC_BLOCK) & (nb * CMP_STRIDE + CMP_BLOCK > sb * SLC_BLOCK))
    imp_t = _dot01_nt(cover.astype(BF16), psum)
    blk = _iota((LANES, tq), 0)
    cur = _div_pow2(i * tq + _iota((LANES, tq), 1), SLC_BLOCK)
    forced = (blk == 0) | (blk == cur) | (blk == cur - 1)
    score = jnp.where(forced, FORCE_SCORE, jnp.where(blk <= cur, imp_t, -FORCE_SCORE))
    rank = jnp.zeros((LANES, tq), F32)
    for s2 in range(n_slc):
        row = score[s2:s2 + 1, :]
        ahead = (row > score) | ((row == score) & (s2 < blk))
        rank = rank + ahead.astype(F32)
    sel_t = ((rank < n_sel) & (blk < n_slc)).astype(F32)
    sel_ref[0, 0] = sel_t


def _cmp_attn_prompt(q, pchunks, bias_c, batch, seq):
    tq = 256
    n_chunks = pchunks.shape[1]
    n_blk = n_chunks - CMP_SLOTS + 1
    n_slc = -(-seq // SLC_BLOCK)
    assert n_chunks == LANES and n_slc <= LANES and seq % tq == 0
    q3 = q.reshape(batch, seq, NSA_WIDTH)
    gw = NSA_REP * HEAD_DIM
    body = functools.partial(_cmp_attn_p_body, tq=tq, n_blk=n_blk, n_slc=n_slc, n_sel=min(N_SELECT, n_slc))
    return pl.pallas_call(
        body,
        grid=(batch, N_NSA_KV, seq // tq),
        in_specs=[pl.BlockSpec((1, tq, gw), lambda b, g, i: (b, i, g))]
        + _p_specs(n_chunks, lambda b, g, i: (b, g))
        + [pl.BlockSpec((NSA_REP, tq, n_chunks), lambda b, g, i: (g, i, 0))],
        out_specs=[pl.BlockSpec((1, tq, gw), lambda b, g, i: (b, i, g)),
                   pl.BlockSpec((1, 1, LANES, tq), lambda b, g, i: (b, g, 0, i))],
        out_shape=[jax.ShapeDtypeStruct((batch, seq, NSA_WIDTH), F32),
                   jax.ShapeDtypeStruct((batch, N_NSA_KV, LANES, seq), F32)],
        compiler_params=_params(("arbitrary",) * 3),
        name="cmp_attn_prompt",
    )(q3, pchunks, pchunks, pchunks, pchunks, bias_c)


_TK = 256


def _flash_step_t(xs, consts, vt, m_s, l_s, acc_s, tq):
    ps, alphas = [], []
    for r, x in enumerate(xs):
        cols = slice(r * tq, (r + 1) * tq)
        m_old = m_s[:, cols]
        m_new = jnp.maximum(m_old, jnp.max(x, axis=0, keepdims=True) + consts[r])
        p = jnp.exp2(x - (m_new - consts[r]))
        alpha = jnp.exp2(m_old - m_new)
        l_s[:, cols] = alpha * l_s[:, cols] + jnp.sum(p, axis=0, keepdims=True)
        m_s[:, cols] = m_new
        ps.append(p.astype(BF16))
        alphas.append(alpha)
    pv = jnp.dot(vt, jnp.concatenate(ps, axis=1), preferred_element_type=F32)
    acc_s[...] = jnp.concatenate(alphas, axis=1) * acc_s[...] + pv


def _nsa_flash_body(table_ref, q_ref, kv_ref, tz_ref, *rest, mode, tq, seq):
    if mode == "slc":
        sel_ref, o_ref, m_s, l_s, acc_s, msk_s = rest
    else:
        o_ref, m_s, l_s, acc_s = rest
    g = pl.program_id(1)
    i = pl.program_id(2)
    qb = jnp.concatenate([q_ref[0, :, r * HEAD_DIM:(r + 1) * HEAD_DIM] * (ATTN_SCALE * LOG2E)
                          for r in range(NSA_REP)], axis=0).astype(BF16)
    _init_state(m_s, l_s, acc_s)
    planes = NSA_KV_WIDTH // HEAD_DIM
    if mode == "slc":
        selb = sel_ref[0, 0].astype(BF16)
        kb = _iota((_TK, LANES), 0)
        sb = _iota((_TK, LANES), 1)
        for jj in range(seq // _TK):
            expand = (sb == _div_pow2(jj * _TK + kb, SLC_BLOCK)).astype(BF16)
            hit = jnp.dot(expand, selb, preferred_element_type=F32)
            msk_s[jj] = (hit - 1.0) * (-NEG_INF)
    kj = _iota((_TK, tq), 0)
    ti = _iota((_TK, tq), 1)
    causal = jnp.where(ti >= kj, 0.0, NEG_INF)
    far_consts = [table_ref[N_BUCKETS - 1, g * NSA_REP + r] * LOG2E for r in range(NSA_REP)]

    def logits(j, kind):
        row0 = j * (_TK * planes) + g
        kt = kv_ref[0, pl.ds(row0, _TK, stride=planes), :].astype(BF16)
        vt = kv_ref[0, pl.ds(row0 + N_NSA_KV, _TK, stride=planes), :].T.astype(BF16)
        x = _dot_nt(kt, qb)
        add = None
        if mode == "slc":
            add = msk_s[j]
        elif kind == "far":
            add = jnp.where((i - j) * _TK + ti - kj <= WINDOW, 0.0, NEG_INF)
        if kind == "diag":
            add = causal if add is None else add + causal
        xs = []
        for r in range(NSA_REP):
            xr = x[:, r * tq:(r + 1) * tq]
            if kind == "prev":
                xr = xr + tz_ref[r, :, tq:2 * tq] * LOG2E
            elif kind == "diag":
                xr = xr + tz_ref[r, :, 0:tq] * LOG2E
            if add is not None:
                xr = xr + add
            xs.append(xr)
        return xs, (far_consts if kind == "far" else [0.0] * NSA_REP), vt

    def run(blocks):
        for xs, consts, vt in [logits(j, kind) for j, kind in blocks]:
            _flash_step_t(xs, consts, vt, m_s, l_s, acc_s, tq)

    lo = 0 if mode == "slc" else jnp.maximum(i - WINDOW // _TK, 0)
    n_far = jnp.maximum(i - 1 - lo, 0)

    def far_pair(p, carry):
        run([(lo + 2 * p, "far"), (lo + 2 * p + 1, "far")])
        return carry
    lax.fori_loop(0, jnp.right_shift(n_far, 1), far_pair, 0)

    @pl.when(jnp.bitwise_and(n_far, 1) == 1)
    def _():
        run([(lo + n_far - 1, "far")])

    @pl.when(i >= 1)
    def _():
        run([(i - 1, "prev"), (i, "diag")])

    @pl.when(i == 0)
    def _():
        run([(i, "diag")])
    o_t = acc_s[...] / l_s[...]
    for r in range(NSA_REP):
        o_ref[0, :, r * HEAD_DIM:(r + 1) * HEAD_DIM] = o_t[:, r * tq:(r + 1) * tq].T


def _nsa_flash_prompt(mode, table, q, kv, tz, sel, batch, seq):
    tq = _TK
    assert _TK >= MAX_DISTANCE and WINDOW % _TK == 0 and seq % _TK == 0
    gw = NSA_REP * HEAD_DIM
    q3 = q.reshape(batch, seq, NSA_WIDTH)
    planes = NSA_KV_WIDTH // HEAD_DIM
    kv3 = kv.reshape(batch, seq * planes, HEAD_DIM)
    cols = NSA_REP * tq
    in_specs = [pl.BlockSpec(memory_space=pltpu.SMEM),
                pl.BlockSpec((1, tq, gw), lambda b, g, i: (b, i, g)),
                pl.BlockSpec((1, seq * planes, HEAD_DIM), lambda b, g, i: (b, 0, 0)),
                pl.BlockSpec((NSA_REP, _TK, 2 * tq), lambda b, g, i: (g, 0, 0))]
    args = [table, q3, kv3, tz]
    scratch = [pltpu.VMEM((1, cols), F32), pltpu.VMEM((1, cols), F32), pltpu.VMEM((HEAD_DIM, cols), F32)]
    if mode == "slc":
        in_specs.append(pl.BlockSpec((1, 1, LANES, tq), lambda b, g, i: (b, g, 0, i)))
        args.append(sel)
        scratch.append(pltpu.VMEM((seq // _TK, _TK, tq), F32))
    return pl.pallas_call(
        functools.partial(_nsa_flash_body, mode=mode, tq=tq, seq=seq),
        grid=(batch, N_NSA_KV, seq // tq),
        in_specs=in_specs,
        out_specs=pl.BlockSpec((1, tq, gw), lambda b, g, i: (b, i, g)),
        out_shape=jax.ShapeDtypeStruct((batch, seq, NSA_WIDTH), F32),
        scratch_shapes=scratch,
        compiler_params=_params(("arbitrary",) * 3),
        name="nsa_flash_" + mode,
    )(*args)


def _cumsum_body(sm_ref, c_ref, ct_ref):
    seq = sm_ref.shape[1]
    lane = _iota((LANES, LANES), 1)
    tri = (_iota((LANES, LANES), 0) >= lane).astype(BF16)
    keep = (lane >= LOGF_LANE0) & (lane < LOGF_LANE0 + N_FOX_HEADS)
    carry = jnp.zeros((1, LANES), F32)
    for blk in range(seq // LANES):
        x = jnp.where(keep, sm_ref[0, blk * LANES:(blk + 1) * LANES, :], 0.0)
        cb = _dot01_left(tri, x) + carry
        c_ref[0, blk * LANES:(blk + 1) * LANES, :] = cb
        ct_ref[0, :, blk * LANES:(blk + 1) * LANES] = cb.T
        carry = cb[LANES - 1:LANES, :]


def _cumsum_prompt(sm, batch, seq):
    sm3 = sm.reshape(batch, seq, LANES)
    return pl.pallas_call(
        _cumsum_body,
        grid=(batch,),
        in_specs=[pl.BlockSpec((1, seq, LANES), lambda b: (b, 0, 0))],
        out_specs=[pl.BlockSpec((1, seq, LANES), lambda b: (b, 0, 0)),
                   pl.BlockSpec((1, LANES, seq), lambda b: (b, 0, 0))],
        out_shape=[jax.ShapeDtypeStruct((batch, seq, LANES), F32),
                   jax.ShapeDtypeStruct((batch, LANES, seq), F32)],
        compiler_params=_params(("arbitrary",)),
        name="logf_cumsum",
    )(sm3)


_FOX_T = 512


def _fox_flash_body(q_ref, k_ref, v_ref, c_ref, crow_ref, o_ref, m_s, l_s, acc_s, *, tq):
    h = pl.program_id(1)
    i = pl.program_id(2)
    qb = (q_ref[0] * (ATTN_SCALE * LOG2E)).astype(BF16)
    _init_state(m_s, l_s, acc_s)
    c_q = crow_ref[0, 0, pl.ds(i, 1), :] * LOG2E
    head_lane = _iota((tq, LANES), 1) == LOGF_LANE0 + h
    causal = jnp.where(_iota((tq, tq), 1) >= _iota((tq, tq), 0), 0.0, NEG_INF)

    def logits(j, diag):
        start = pl.multiple_of(j * tq, tq)
        kt = k_ref[0, pl.ds(start, tq), :].astype(BF16)
        vt = v_ref[0, pl.ds(start, tq), :].T.astype(BF16)
        c_k = jnp.sum(jnp.where(head_lane, c_ref[0, pl.ds(start, tq), :], 0.0), axis=-1, keepdims=True)
        x = _dot_nt(kt, qb) - c_k * LOG2E
        if diag:
            x = x + causal
        return x, vt

    def step(x, vt):
        m_old = m_s[...]
        m_new = jnp.maximum(m_old, jnp.max(x, axis=0, keepdims=True) + c_q)
        p = jnp.exp2(x - (m_new - c_q))
        alpha = jnp.exp2(m_old - m_new)
        l_s[...] = alpha * l_s[...] + jnp.sum(p, axis=0, keepdims=True)
        acc_s[...] = alpha * acc_s[...] + jnp.dot(vt, p.astype(BF16), preferred_element_type=F32)
        m_s[...] = m_new

    def run(blocks):
        for x, vt in [logits(j, diag) for j, diag in blocks]:
            step(x, vt)

    def far_pair(p, carry):
        run([(2 * p, False), (2 * p + 1, False)])
        return carry
    lax.fori_loop(0, jnp.right_shift(i, 1), far_pair, 0)

    @pl.when(jnp.bitwise_and(i, 1) == 1)
    def _():
        run([(i - 1, False), (i, True)])

    @pl.when(jnp.bitwise_and(i, 1) == 0)
    def _():
        run([(i, True)])
    o_ref[0] = (acc_s[...] / l_s[...]).T


def _fox_flash_prompt(qf, kvf, c, c_rows, batch, seq):
    tq = _FOX_T
    assert seq % tq == 0
    q3 = qf.reshape(batch, seq, FOX_WIDTH)
    kv3 = kvf.reshape(batch, seq, 2 * FOX_WIDTH)
    return pl.pallas_call(
        functools.partial(_fox_flash_body, tq=tq),
        grid=(batch, N_FOX_HEADS, seq // tq),
        in_specs=[pl.BlockSpec((1, tq, HEAD_DIM), lambda b, h, i: (b, i, h)),
                  pl.BlockSpec((1, seq, HEAD_DIM), lambda b, h, i: (b, 0, h)),
                  pl.BlockSpec((1, seq, HEAD_DIM), lambda b, h, i: (b, 0, N_FOX_HEADS + h)),
                  pl.BlockSpec((1, seq, LANES), lambda b, h, i: (b, 0, 0)),
                  pl.BlockSpec((1, 1, seq // tq, tq), lambda b, h, i: (b, h, 0, 0))],
        out_specs=pl.BlockSpec((1, tq, HEAD_DIM), lambda b, h, i: (b, i, h)),
        out_shape=jax.ShapeDtypeStruct((batch, seq, FOX_WIDTH), F32),
        scratch_shapes=[pltpu.VMEM((1, tq), F32), pltpu.VMEM((1, tq), F32), pltpu.VMEM((HEAD_DIM, tq), F32)],
        compiler_params=_params(("arbitrary",) * 3),
        name="fox_flash_prompt",
    )(q3, kv3, kv3, c, c_rows)


def _rms(x, g):
    return x * lax.rsqrt(jnp.mean(x * x, axis=-1, keepdims=True) + RMS_EPS) * g


_OUTPROJ_SUB = 256


def _outproj_body(x_ref, oc_ref, os_ref, ow_ref, of_ref, sm_ref, gn_ref, gf_ref, wo_ref, h_ref):
    tm = x_ref.shape[0]
    for r0 in range(0, tm, _OUTPROJ_SUB):
        rows = slice(r0, min(r0 + _OUTPROJ_SUB, tm))
        gates = sm_ref[rows, :]
        parts = []
        for hh in range(N_NSA_HEADS):
            sl = slice(hh * HEAD_DIM, (hh + 1) * HEAD_DIM)
            parts.append(gates[:, hh:hh + 1] * oc_ref[rows, sl]
                         + gates[:, N_NSA_HEADS + hh:N_NSA_HEADS + hh + 1] * os_ref[rows, sl]
                         + gates[:, 2 * N_NSA_HEADS + hh:2 * N_NSA_HEADS + hh + 1] * ow_ref[rows, sl])
        o_nsa = jnp.concatenate(parts, axis=1)
        mixed = jnp.concatenate([_rms(o_nsa, gn_ref[...]), _rms(of_ref[rows, :], gf_ref[...])],
                                axis=1).astype(BF16)
        h_ref[rows, :] = x_ref[rows, :] + jnp.dot(mixed, wo_ref[...], preferred_element_type=F32)


def _outproj(x2d, o_c, o_s, o_w, o_f, sm, g_nsa, g_fox, wo):
    n, d = x2d.shape
    tm = min(n, 2 * _OUTPROJ_SUB)
    assert n % tm == 0
    row = lambda w: pl.BlockSpec((tm, w), lambda i: (i, 0))
    const = lambda shape: pl.BlockSpec(shape, lambda i: (0, 0), pipeline_mode=pl.Buffered(1))
    return pl.pallas_call(
        _outproj_body,
        grid=(n // tm,),
        in_specs=[row(d), row(NSA_WIDTH), row(NSA_WIDTH), row(NSA_WIDTH), row(FOX_WIDTH), row(LANES),
                  const((1, NSA_WIDTH)), const((1, FOX_WIDTH)), const(wo.shape)],
        out_specs=row(d),
        out_shape=jax.ShapeDtypeStruct((n, d), F32),
        compiler_params=_params(("arbitrary",)),
        name="out_proj",
    )(x2d, o_c, o_s, o_w, o_f, sm, g_nsa.reshape(1, -1), g_fox.reshape(1, -1), wo)


def _mlp_body(h_ref, gm_ref, gfin_ref, wu_ref, wd_ref, y_ref, xn_s, acc_s):
    j = pl.program_id(1)

    @pl.when(j == 0)
    def _():
        xn_s[...] = _rms(h_ref[...], gm_ref[...]).astype(BF16)
        acc_s[...] = jnp.zeros(acc_s.shape, F32)
    u = jnp.maximum(jnp.dot(xn_s[...], wu_ref[...], preferred_element_type=F32), 0.0)
    acc_s[...] += jnp.dot((u * u).astype(BF16), wd_ref[...], preferred_element_type=F32)

    @pl.when(j == pl.num_programs(1) - 1)
    def _():
        y_ref[...] = _rms(h_ref[...] + acc_s[...], gfin_ref[...])


def _mlp(h, g_mlp, g_final, wu, wd):
    n, d = h.shape
    dff = wu.shape[1]
    tm = min(n, 512)
    tf = 1024
    return pl.pallas_call(
        _mlp_body,
        grid=(n // tm, dff // tf),
        in_specs=[pl.BlockSpec((tm, d), lambda i, j: (i, 0)),
                  pl.BlockSpec((1, d), lambda i, j: (0, 0)),
                  pl.BlockSpec((1, d), lambda i, j: (0, 0)),
                  pl.BlockSpec((d, tf), lambda i, j: (0, j)),
                  pl.BlockSpec((tf, d), lambda i, j: (j, 0))],
        out_specs=pl.BlockSpec((tm, d), lambda i, j: (i, 0)),
        out_shape=jax.ShapeDtypeStruct((n, d), F32),
        scratch_shapes=[pltpu.VMEM((tm, d), BF16), pltpu.VMEM((tm, d), F32)],
        compiler_params=_params(("arbitrary", "arbitrary")),
        name="mlp_final",
    )(h, g_mlp.reshape(1, d), g_final.reshape(1, d), wu, wd)


def _cmp_attn_s_body(q_ref, p00, p10, p01, p11, bias_ref, oc_ref, imp_ref, *, past, n_blk):
    n_chunks = p00.shape[1]
    t_new = q_ref.shape[1]
    sl = imp_ref.shape[-1]
    k, v = _cmp_kv(p00, p10, p01, p11)
    qb = jnp.concatenate([q_ref[0, :, r * HEAD_DIM:(r + 1) * HEAD_DIM] for r in range(NSA_REP)],
                         axis=0).astype(BF16)
    bias = jnp.concatenate([bias_ref[r] for r in range(NSA_REP)], axis=0)
    rows = NSA_REP * t_new
    t = past + _mod_pow2(_iota((rows, n_chunks), 0), t_new)
    n = _iota((rows, n_chunks), 1)
    mask = (n * CMP_STRIDE + (CMP_BLOCK - 1) <= t) & (n < n_blk)
    s = jnp.where(mask, _dot_nt(qb, k) * ATTN_SCALE + bias, NEG_INF)
    e = jnp.where(mask, jnp.exp(s - jnp.max(s, axis=-1, keepdims=True)), 0.0)
    l = jnp.sum(e, axis=-1, keepdims=True)
    p = e / jnp.where(l > 0.0, l, 1.0)
    o = jnp.dot(p.astype(BF16), v, preferred_element_type=F32)
    psum = jnp.zeros((t_new, n_chunks), F32)
    for r in range(NSA_REP):
        oc_ref[0, :, r * HEAD_DIM:(r + 1) * HEAD_DIM] = o[r * t_new:(r + 1) * t_new]
        psum = psum + p[r * t_new:(r + 1) * t_new]
    nb = _iota((n_chunks, sl), 0)
    sb = _iota((n_chunks, sl), 1)
    cover = ((nb * CMP_STRIDE < sb * SLC_BLOCK + SLC_BLOCK) & (nb * CMP_STRIDE + CMP_BLOCK > sb * SLC_BLOCK))
    imp = _dot01_right(psum, cover.astype(BF16))
    imp_ref[0, 0] = imp


def _topk_s_body(imp_ref, sel_ref, *, past, t_new, n_slc, n_sel):
    rows, sl = imp_ref.shape
    blk = _iota((rows, sl), 1)
    cur = _div_pow2(past + _mod_pow2(_iota((rows, sl), 0), t_new), SLC_BLOCK)
    forced = (blk == 0) | (blk == cur) | (blk == cur - 1)
    score = jnp.where(forced, FORCE_SCORE, jnp.where(blk <= cur, imp_ref[...], -FORCE_SCORE))
    score = jnp.where(blk < n_slc, score, -jnp.inf)
    sel = jnp.zeros((rows, sl), F32)
    for _ in range(n_sel):
        mx = jnp.max(score, axis=-1, keepdims=True)
        first = jnp.min(jnp.where(score == mx, blk, sl), axis=-1, keepdims=True)
        hit = blk == first
        sel = jnp.where(hit, 1.0, sel)
        score = jnp.where(hit, -jnp.inf, score)
    sel_ref[...] = sel


def _cmp_attn_sample(q, pchunks, bias_c, batch, t_new, past):
    n_chunks = pchunks.shape[1]
    n_blk = n_chunks - CMP_SLOTS + 1
    n_slc = -(-(past + t_new) // SLC_BLOCK)
    sl = -(-n_slc // LANES) * LANES
    gw = NSA_REP * HEAD_DIM
    q3 = q.reshape(batch, t_new, NSA_WIDTH)
    body = functools.partial(_cmp_attn_s_body, past=past, n_blk=n_blk)
    o_c, imp = pl.pallas_call(
        body,
        grid=(batch, N_NSA_KV),
        in_specs=[pl.BlockSpec((1, t_new, gw), lambda b, g: (b, 0, g))]
        + _p_specs(n_chunks, lambda b, g: (b, g))
        + [pl.BlockSpec((NSA_REP, t_new, n_chunks), lambda b, g: (g, 0, 0))],
        out_specs=[pl.BlockSpec((1, t_new, gw), lambda b, g: (b, 0, g)),
                   pl.BlockSpec((1, 1, t_new, sl), lambda b, g: (b, g, 0, 0))],
        out_shape=[jax.ShapeDtypeStruct((batch, t_new, NSA_WIDTH), F32),
                   jax.ShapeDtypeStruct((batch, N_NSA_KV, t_new, sl), F32)],
        compiler_params=_params(("arbitrary",) * 2),
        name="cmp_attn_sample",
    )(q3, pchunks, pchunks, pchunks, pchunks, bias_c)
    rows = batch * N_NSA_KV * t_new
    sel = pl.pallas_call(
        functools.partial(_topk_s_body, past=past, t_new=t_new, n_slc=n_slc, n_sel=min(N_SELECT, n_slc)),
        grid=(1,),
        in_specs=[pl.BlockSpec((rows, sl), lambda i: (0, 0))],
        out_specs=pl.BlockSpec((rows, sl), lambda i: (0, 0)),
        out_shape=jax.ShapeDtypeStruct((rows, sl), F32),
        compiler_params=_params(("arbitrary",)),
        name="topk_sample",
    )(imp.reshape(rows, sl))
    return o_c, sel.reshape(batch, N_NSA_KV, t_new, sl)


_GKV = N_NSA_KV * HEAD_DIM


def _dec_tile(qbd, kv, dist, mask, table_ref, full_bias, t_new, m_s, l_s, acc_s):
    s = _dot_nt(qbd, kv[:, :_GKV].astype(BF16)) * ATTN_SCALE
    parts = []
    for hh in range(N_NSA_HEADS):
        rows = slice(hh * t_new, (hh + 1) * t_new)
        if full_bias:
            b = _t5_lookup(_t5_bucket(dist[rows]), table_ref, hh)
        else:
            b = jnp.full((t_new, s.shape[1]), table_ref[N_BUCKETS - 1, hh], F32)
        parts.append(s[rows] + b)
    s = jnp.concatenate(parts, axis=0)
    _online_update(s, mask, kv[:, _GKV:].astype(BF16), m_s, l_s, acc_s)


def _dec_finish(o_ref, t_new, l_s, acc_s):
    o = acc_s[...] / l_s[...]
    for g in range(N_NSA_KV):
        for r in range(NSA_REP):
            hh = g * NSA_REP + r
            o_ref[0, :, hh * HEAD_DIM:(hh + 1) * HEAD_DIM] = o[hh * t_new:(hh + 1) * t_new,
                                                             g * HEAD_DIM:(g + 1) * HEAD_DIM]


def _row_mask(m2, t_new):
    return jnp.concatenate([m2[g * t_new:(g + 1) * t_new] for g in range(N_NSA_KV) for _ in range(NSA_REP)], axis=0)


def _slc_s_body(pt_ref, table_ref, q_ref, *rest, n_pg, past, t_new):
    page_refs = rest[:n_pg]
    selp_ref, knew_ref, vnew_ref, selnew_ref, o_ref, m_s, l_s, acc_s = rest[n_pg:]
    j = pl.program_id(1)
    last = pl.num_programs(1) - 1
    grp_rows = NSA_REP * t_new
    q = (q_ref[0] * ATTN_SCALE).astype(BF16)
    t_pos = past + _iota((t_new, PAGE_SIZE), 0)
    lane = _iota((t_new, PAGE_SIZE), 1)

    @pl.when(j == 0)
    def _():
        _init_state(m_s, l_s, acc_s)

    def update(x, mask, vs):
        x = jnp.where(mask, x, NEG_INF)
        m_old = m_s[...]
        m_new = jnp.maximum(m_old, jnp.max(x, axis=-1, keepdims=True))
        p = jnp.where(mask, jnp.exp(x - m_new), 0.0)
        alpha = jnp.exp(m_old - m_new)
        l_s[...] = alpha * l_s[...] + jnp.sum(p, axis=-1, keepdims=True)
        pb = p.astype(BF16)
        pv = jnp.concatenate([jnp.dot(pb[g * grp_rows:(g + 1) * grp_rows], vs[g], preferred_element_type=F32)
                              for g in range(N_NSA_KV)], axis=0)
        acc_s[...] = alpha * acc_s[...] + pv
        m_s[...] = m_new

    def head_bias(hh, dist):
        return _t5_lookup(_t5_bucket(dist), table_ref, hh)

    dist_last = t_pos - ((j * n_pg + n_pg - 1) * PAGE_SIZE + lane)
    xs, masks, vs = [], [], []
    for g in range(N_NSA_KV):
        plane = lambda c: jnp.concatenate(
            [pr[0, pl.ds(c * N_NSA_KV + g, PAGE_SIZE, stride=2 * N_NSA_KV), :] for pr in page_refs],
            axis=0).astype(BF16)
        x = _dot_nt(q[g * grp_rows:(g + 1) * grp_rows], plane(0))
        vs.append(plane(1))
        for r in range(NSA_REP):
            hh = g * NSA_REP + r
            far = jnp.full((t_new, (n_pg - 1) * PAGE_SIZE), table_ref[N_BUCKETS - 1, hh], F32)
            bias = jnp.concatenate([far, head_bias(hh, dist_last)], axis=1)
            xs.append(x[r * t_new:(r + 1) * t_new] + bias)
        picked = []
        for k in range(n_pg):
            m2 = selp_ref[0, k][g * t_new:(g + 1) * t_new]
            picked.append(jnp.where(lane < SLC_BLOCK, m2[:, 0:1], m2[:, 1:2]))
        masks.extend([jnp.concatenate(picked, axis=1)] * NSA_REP)
    update(jnp.concatenate(xs, axis=0), jnp.concatenate(masks, axis=0) > 0.5, vs)

    @pl.when(j == last)
    def _():
        dist = t_pos - (past + lane)
        xs, masks, vs = [], [], []
        for g in range(N_NSA_KV):
            x = _dot_nt(q[g * grp_rows:(g + 1) * grp_rows], knew_ref[0, g].astype(BF16))
            vs.append(vnew_ref[0, g].astype(BF16))
            for r in range(NSA_REP):
                xs.append(x[r * t_new:(r + 1) * t_new] + head_bias(g * NSA_REP + r, dist))
            sel_g = jnp.broadcast_to(selnew_ref[0][g * t_new:(g + 1) * t_new, 0:1], (t_new, PAGE_SIZE)) > 0.5
            masks.extend([sel_g & (dist >= 0) & (lane < t_new)] * NSA_REP)
        update(jnp.concatenate(xs, axis=0), jnp.concatenate(masks, axis=0), vs)
        o = acc_s[...] / l_s[...]
        for hh in range(N_NSA_HEADS):
            o_ref[0, :, hh * HEAD_DIM:(hh + 1) * HEAD_DIM] = o[hh * t_new:(hh + 1) * t_new]


def _slc_sample(table, page_table, q_rows, cache, selp, knew, vnew, selnew, past, t_new):
    batch, n_pages = page_table.shape
    n_pg = math.gcd(n_pages, 32)
    n_phys = cache.shape[0]
    assert t_new <= PAGE_SIZE and past % PAGE_SIZE == 0 and PAGE_SIZE >= MAX_DISTANCE
    page_rows = PAGE_SIZE * 2 * N_NSA_KV
    x = cache.reshape(n_phys, page_rows, HEAD_DIM)
    rows = N_NSA_HEADS * t_new

    def page_spec(k):
        return pl.BlockSpec((1, page_rows, HEAD_DIM), lambda b, j, pt: (pt[b * n_pages + j * n_pg + k], 0, 0))
    new_spec = pl.BlockSpec((1, N_NSA_KV, PAGE_SIZE, HEAD_DIM), lambda b, j, pt: (b, 0, 0, 0))
    grid_spec = pltpu.PrefetchScalarGridSpec(
        num_scalar_prefetch=1,
        grid=(batch, n_pages // n_pg),
        in_specs=[pl.BlockSpec(memory_space=pltpu.SMEM),
                  pl.BlockSpec((1, rows, HEAD_DIM), lambda b, j, pt: (b, 0, 0))]
        + [page_spec(k) for k in range(n_pg)]
        + [pl.BlockSpec((1, n_pg, N_NSA_KV * t_new, 2), lambda b, j, pt: (b, j, 0, 0)),
           new_spec, new_spec,
           pl.BlockSpec((1, N_NSA_KV * t_new, 2), lambda b, j, pt: (b, 0, 0))],
        out_specs=pl.BlockSpec((1, t_new, NSA_WIDTH), lambda b, j, pt: (b, 0, 0)),
        scratch_shapes=[pltpu.VMEM((rows, 1), F32), pltpu.VMEM((rows, 1), F32), pltpu.VMEM((rows, HEAD_DIM), F32)],
    )
    return pl.pallas_call(
        functools.partial(_slc_s_body, n_pg=n_pg, past=past, t_new=t_new),
        grid_spec=grid_spec,
        out_shape=jax.ShapeDtypeStruct((batch, t_new, NSA_WIDTH), F32),
        compiler_params=_params(("arbitrary", "arbitrary")),
        name="slc_sample",
    )(page_table.reshape(-1), table, q_rows, *([x] * n_pg), selp, knew, vnew, selnew)


def _win_s_body(table_ref, q_ref, kv_ref, o_ref, m_s, l_s, acc_s, *, past, t_new, wb):
    rows = N_NSA_HEADS * t_new
    qbd = q_ref[0]
    t_pos = past + _mod_pow2(_iota((rows, PAGE_SIZE), 0), t_new)
    lane = _iota((rows, PAGE_SIZE), 1)
    _init_state(m_s, l_s, acc_s)
    for k in range(kv_ref.shape[1] // PAGE_SIZE):
        k_pos = past - wb + k * PAGE_SIZE + lane
        dist = t_pos - k_pos
        mask = (dist >= 0) & (dist <= WINDOW) & (k_pos >= 0) & (k * PAGE_SIZE + lane < wb + t_new)
        _dec_tile(qbd, kv_ref[0, k * PAGE_SIZE:(k + 1) * PAGE_SIZE, :], dist, mask, table_ref, True, t_new,
                  m_s, l_s, acc_s)
    _dec_finish(o_ref, t_new, l_s, acc_s)


def _win_sample(table, qbd, win_all_padded, past, t_new, wb):
    batch, n_keys, _ = win_all_padded.shape
    rows = N_NSA_HEADS * t_new
    return pl.pallas_call(
        functools.partial(_win_s_body, past=past, t_new=t_new, wb=wb),
        grid=(batch,),
        in_specs=[pl.BlockSpec(memory_space=pltpu.SMEM),
                  pl.BlockSpec((1, rows, _GKV), lambda b: (b, 0, 0)),
                  pl.BlockSpec((1, n_keys, 2 * _GKV), lambda b: (b, 0, 0))],
        out_specs=pl.BlockSpec((1, t_new, NSA_WIDTH), lambda b: (b, 0, 0)),
        out_shape=jax.ShapeDtypeStruct((batch, t_new, NSA_WIDTH), F32),
        scratch_shapes=[pltpu.VMEM((rows, 1), F32), pltpu.VMEM((rows, 1), F32), pltpu.VMEM((rows, _GKV), F32)],
        compiler_params=_params(("arbitrary",)),
        name="win_sample",
    )(table, qbd, win_all_padded)


_FOX_PAGE_LANES = PAGE_SIZE * N_FOX_HEADS


def _fox_s_body(pt_ref, q_ref, *rest, n_pg, t_new):
    kv_refs = rest[:n_pg]
    lf_refs = rest[n_pg:2 * n_pg]
    tri_ref, knew_ref, vnew_ref, lfn_ref, o_ref, m_s, l_s, acc_s, off_s = rest[2 * n_pg:]
    j = pl.program_id(1)
    rows = N_FOX_HEADS * t_new
    pw = _FOX_PAGE_LANES
    qb = (q_ref[0] * ATTN_SCALE).astype(BF16)
    lane = _iota((rows, PAGE_SIZE), 1)
    t_row = _mod_pow2(_iota((rows, PAGE_SIZE), 0), t_new)
    h_row = _div_pow2(_iota((rows, PAGE_SIZE), 0), t_new)
    lfn = lfn_ref[0]
    c_new = jnp.sum(jnp.where(lane <= t_row, lfn, 0.0), axis=-1, keepdims=True)

    @pl.when(j == 0)
    def _():
        _init_state(m_s, l_s, acc_s)
        off_s[...] = jnp.zeros(off_s.shape, F32)
        u_l = _div_pow2(lane, N_FOX_HEADS)
        bias = jnp.zeros((rows, PAGE_SIZE), F32)
        for u in range(t_new):
            col = jnp.sum(jnp.where((lane > u) & (lane <= t_row), lfn, 0.0), axis=-1, keepdims=True)
            bias = jnp.where(u_l == u, col, bias)
        x = _dot_nt(qb, knew_ref[0].astype(BF16)) + bias
        mask = (u_l <= t_row) & (u_l < t_new) & (_mod_pow2(lane, N_FOX_HEADS) == h_row)
        _online_update(x, mask, vnew_ref[0].astype(BF16), m_s, l_s, acc_s)

    lf = jnp.concatenate([r[0] for r in lf_refs], axis=0)
    later = _dot01_right(lf, tri_ref[...])
    tot = jnp.sum(lf, axis=-1, keepdims=True)
    head_lane = jnp.where(
        _mod_pow2(_iota((rows, pw), 1), N_FOX_HEADS) == _div_pow2(_iota((rows, pw), 0), t_new), 0.0, NEG_INF)
    spread = lambda a: jnp.concatenate(
        [jnp.broadcast_to(a[h:h + 1], (t_new, a.shape[1])) for h in range(N_FOX_HEADS)], axis=0)
    off = off_s[:, 0:1]
    xs, vs = [], []
    for k in range(n_pg):
        kk = kv_refs[k][0, :, 0].reshape(pw, HEAD_DIM).astype(BF16)
        vs.append(kv_refs[k][0, :, 1].reshape(pw, HEAD_DIM).astype(BF16))
        hs = slice(k * N_FOX_HEADS, (k + 1) * N_FOX_HEADS)
        xs.append(_dot_nt(qb, kk) + spread(later[hs]) + (spread(off) + c_new) + head_lane)
        off = off + tot[hs]
    off_s[...] = jnp.broadcast_to(off, off_s.shape)
    x = jnp.concatenate(xs, axis=1)
    m_old = m_s[...]
    m_new = jnp.maximum(m_old, jnp.max(x, axis=-1, keepdims=True))
    p = jnp.exp(x - m_new)
    alpha = jnp.exp(m_old - m_new)
    l_s[...] = alpha * l_s[...] + jnp.sum(p, axis=-1, keepdims=True)
    acc_s[...] = alpha * acc_s[...] + jnp.dot(p.astype(BF16), jnp.concatenate(vs, axis=0),
                                              preferred_element_type=F32)
    m_s[...] = m_new

    @pl.when(j == pl.num_programs(1) - 1)
    def _():
        o = acc_s[...] / l_s[...]
        for h in range(N_FOX_HEADS):
            o_ref[0, :, h * HEAD_DIM:(h + 1) * HEAD_DIM] = o[h * t_new:(h + 1) * t_new]


def _fox_sample(page_table, q_rows, cache, logf_t, tri, knew, vnew, lfn, t_new):
    batch, n_pages = page_table.shape
    n_pg = math.gcd(n_pages, 16)
    rows = N_FOX_HEADS * t_new
    page = lambda k: (lambda b, j, pt: pt[b * n_pages + n_pages - 1 - (j * n_pg + k)])

    def kv_spec(k):
        return pl.BlockSpec((1, PAGE_SIZE, 2, N_FOX_HEADS, HEAD_DIM), lambda b, j, pt: (page(k)(b, j, pt), 0, 0, 0, 0))

    def lf_spec(k):
        return pl.BlockSpec((1, N_FOX_HEADS, PAGE_SIZE), lambda b, j, pt: (page(k)(b, j, pt), 0, 0))
    new_spec = pl.BlockSpec((1, PAGE_SIZE, HEAD_DIM), lambda b, j, pt: (b, 0, 0))
    grid_spec = pltpu.PrefetchScalarGridSpec(
        num_scalar_prefetch=1,
        grid=(batch, n_pages // n_pg),
        in_specs=[pl.BlockSpec((1, rows, HEAD_DIM), lambda b, j, pt: (b, 0, 0))]
        + [kv_spec(k) for k in range(n_pg)] + [lf_spec(k) for k in range(n_pg)]
        + [pl.BlockSpec(tri.shape, lambda b, j, pt: (0, 0)), new_spec, new_spec,
           pl.BlockSpec((1, rows, PAGE_SIZE), lambda b, j, pt: (b, 0, 0))],
        out_specs=pl.BlockSpec((1, t_new, FOX_WIDTH), lambda b, j, pt: (b, 0, 0)),
        scratch_shapes=[pltpu.VMEM((rows, 1), F32), pltpu.VMEM((rows, 1), F32), pltpu.VMEM((rows, HEAD_DIM), F32),
                        pltpu.VMEM((N_FOX_HEADS, LANES), F32)],
    )
    return pl.pallas_call(
        functools.partial(_fox_s_body, n_pg=n_pg, t_new=t_new),
        grid_spec=grid_spec,
        out_shape=jax.ShapeDtypeStruct((batch, t_new, FOX_WIDTH), F32),
        compiler_params=_params(("arbitrary", "arbitrary")),
        name="fox_sample",
    )(page_table.reshape(-1), q_rows, *([cache] * n_pg), *([logf_t] * n_pg), tri, knew, vnew, lfn)


def _pack_weights(w_in, b_f, w_cmp, pe_cmp, w_o, w_up, w_down):
    c_gt = NSA_WIDTH + 3 * NSA_KV_WIDTH
    c_qf = c_gt + GATE_LANES
    c_fl = c_qf + 3 * FOX_WIDTH
    wm = jnp.concatenate([w_in[:, :c_gt], w_in[:, c_qf:c_fl]], axis=1).astype(BF16)
    ws = jnp.concatenate([w_in[:, c_gt:c_qf], w_in[:, c_fl:],
                          jnp.zeros((w_in.shape[0], LANES - GATE_LANES - N_FOX_HEADS), F32)], axis=1).astype(BF16)
    bf128 = jnp.zeros((1, LANES), F32).at[0, LOGF_LANE0:LOGF_LANE0 + N_FOX_HEADS].set(b_f)
    wc = w_cmp.reshape(2, CMP_SLOTS, CMP_STRIDE, HEAD_DIM, HEAD_DIM).transpose(0, 2, 3, 1, 4).reshape(
        2, _CHUNK_K, CMP_SLOTS * HEAD_DIM).astype(BF16)
    pe8 = jnp.zeros((2, SUBLANES, _CHUNK_K), F32).at[:, :CMP_SLOTS].set(pe_cmp.reshape(2, CMP_SLOTS, _CHUNK_K))
    return wm, ws, bf128, wc, pe8, w_o.astype(BF16), w_up.astype(BF16), w_down.astype(BF16)


def _block_diag_q(q, t_new, n_grp, per_grp):
    batch = q.shape[0]
    q5 = q.reshape(batch, t_new, n_grp, per_grp, HEAD_DIM).transpose(0, 2, 3, 1, 4)
    eye = jnp.eye(n_grp, dtype=q.dtype)
    out = q5[:, :, :, :, None, :] * eye[None, :, None, None, :, None]
    return out.reshape(batch, n_grp * per_grp * t_new, n_grp * HEAD_DIM).astype(BF16)


def _prompt_pass(x, table, lw, packed):
    g_attn, g_nsa_out, g_fox_out, g_mlp, g_final = lw
    wm, ws, bf128, wc, pe8, wo, wu, wd = packed
    batch, seq, d = x.shape
    assert seq % 256 == 0 and seq >= WINDOW and seq % PAGE_SIZE == 0
    x2d = x.reshape(batch * seq, d)
    q, ckv, skv, wkv, qf, kvf, sm = _project(x2d, g_attn, wm, ws, bf128)
    own_pages = jnp.arange(batch * (seq // PAGE_SIZE), dtype=I32).reshape(batch, seq // PAGE_SIZE)
    pchunks = _chunkproj_paged(ckv, own_pages, pe8, wc)
    n_chunks = seq // CMP_STRIDE
    dist_c = jnp.arange(seq, dtype=I32)[:, None] - (jnp.arange(n_chunks, dtype=I32)[None, :] * CMP_STRIDE
                                                     + (CMP_BLOCK - 1))
    bias_c = _t5_bias(table, dist_c)
    o_c, sel = _cmp_attn_prompt(q, pchunks, bias_c, batch, seq)
    dist_tz = jnp.arange(2 * _TK, dtype=I32)[None, :] - jnp.arange(_TK, dtype=I32)[:, None]
    tz = _t5_bias(table, dist_tz)
    o_s = _nsa_flash_prompt("slc", table, q, skv, tz, sel, batch, seq)
    o_w = _nsa_flash_prompt("win", table, q, wkv, tz, None, batch, seq)
    c, c_t = _cumsum_prompt(sm, batch, seq)
    c_rows = c_t[:, LOGF_LANE0:LOGF_LANE0 + N_FOX_HEADS, :].reshape(batch, N_FOX_HEADS, seq // _FOX_T, _FOX_T)
    o_f = _fox_flash_prompt(qf, kvf, c, c_rows, batch, seq)
    h = _outproj(x2d, o_c.reshape(-1, NSA_WIDTH), o_s.reshape(-1, NSA_WIDTH), o_w.reshape(-1, NSA_WIDTH),
                 o_f.reshape(-1, FOX_WIDTH), sm, g_nsa_out, g_fox_out, wo)
    y = _mlp(h, g_mlp, g_final, wu, wd).reshape(batch, seq, d)
    kv_shape = (1, batch, seq, 2, N_NSA_KV, HEAD_DIM)
    wkv5 = wkv.reshape(kv_shape)
    logf = sm.reshape(batch, seq, LANES)[:, :, LOGF_LANE0:LOGF_LANE0 + N_FOX_HEADS]
    return (y, ckv.reshape(kv_shape), skv.reshape(kv_shape),
            kvf.reshape(1, batch, seq, 2, N_FOX_HEADS, HEAD_DIM), logf[None],
            wkv5[:, :, seq - min(WINDOW, seq):])


def _sample_pass(x, cache_cmp, cache_slc, cache_fox, cache_logf, win_buf, page_table, table, lw, packed):
    g_attn, g_nsa_out, g_fox_out, g_mlp, g_final = lw
    wm, ws, bf128, wc, pe8, wo, wu, wd = packed
    batch, t_new, d = x.shape
    n_pages = page_table.shape[1]
    past = n_pages * PAGE_SIZE
    assert t_new < CMP_STRIDE and t_new % SUBLANES == 0 and t_new * N_FOX_HEADS <= PAGE_SIZE
    x2d = x.reshape(batch * t_new, d)
    q, ckv, skv, wkv, qf, kvf, sm = _project(x2d, g_attn, wm, ws, bf128)
    pchunks = _chunkproj_paged(cache_cmp, page_table, pe8, wc)
    n_chunks = pchunks.shape[1]
    pos = past + jnp.arange(t_new, dtype=I32)
    dist_c = pos[:, None] - (jnp.arange(n_chunks, dtype=I32)[None, :] * CMP_STRIDE + (CMP_BLOCK - 1))
    bias_c = _t5_bias(table, dist_c)
    o_c, sel = _cmp_attn_sample(q, pchunks, bias_c, batch, t_new, past)
    q5 = q.reshape(batch, t_new, N_NSA_KV, NSA_REP, HEAD_DIM)
    q_rows = q5.transpose(0, 2, 3, 1, 4).reshape(batch, N_NSA_HEADS * t_new, HEAD_DIM)
    n_past_blk = past // SLC_BLOCK
    bpp = PAGE_SIZE // SLC_BLOCK
    selp = sel[..., :n_past_blk].reshape(batch, N_NSA_KV, t_new, n_pages, bpp).transpose(0, 3, 1, 2, 4).reshape(
        batch, n_pages, N_NSA_KV * t_new, bpp)
    selnew = jnp.broadcast_to(sel[..., n_past_blk:n_past_blk + 1].reshape(batch, N_NSA_KV * t_new, 1),
                              (batch, N_NSA_KV * t_new, bpp))
    skv5 = skv.reshape(batch, t_new, 2, N_NSA_KV, HEAD_DIM).transpose(2, 0, 3, 1, 4)
    skv5 = jnp.pad(skv5, ((0, 0), (0, 0), (0, 0), (0, PAGE_SIZE - t_new), (0, 0)))
    o_s = _slc_sample(table, page_table, q_rows, cache_slc, selp, skv5[0], skv5[1], selnew, past, t_new)
    qbd = _block_diag_q(q.reshape(batch, t_new, NSA_WIDTH), t_new, N_NSA_KV, NSA_REP)
    wb = win_buf.shape[1]
    win_all = jnp.concatenate([win_buf.reshape(batch, wb, NSA_KV_WIDTH), wkv.reshape(batch, t_new, NSA_KV_WIDTH)],
                              axis=1)
    n_keys = -(-(wb + t_new) // PAGE_SIZE) * PAGE_SIZE
    o_w = _win_sample(table, qbd, jnp.pad(win_all, ((0, 0), (0, n_keys - wb - t_new), (0, 0))), past, t_new, wb)
    logf_new = sm.reshape(batch, t_new, LANES)[:, :, LOGF_LANE0:LOGF_LANE0 + N_FOX_HEADS]
    logf_t = cache_logf.transpose(0, 2, 1)
    qf_rows = qf.reshape(batch, t_new, N_FOX_HEADS, HEAD_DIM).transpose(0, 2, 1, 3).reshape(
        batch, N_FOX_HEADS * t_new, HEAD_DIM)
    kvf5 = kvf.reshape(batch, t_new, 2, N_FOX_HEADS * HEAD_DIM).transpose(2, 0, 1, 3).reshape(
        2, batch, t_new * N_FOX_HEADS, HEAD_DIM)
    kvf5 = jnp.pad(kvf5, ((0, 0), (0, 0), (0, PAGE_SIZE - t_new * N_FOX_HEADS), (0, 0)))
    lfn = jnp.broadcast_to(logf_new.transpose(0, 2, 1)[:, :, None, :], (batch, N_FOX_HEADS, t_new, t_new))
    lfn = jnp.pad(lfn.reshape(batch, N_FOX_HEADS * t_new, t_new), ((0, 0), (0, 0), (0, PAGE_SIZE - t_new)))
    tri = (jnp.arange(PAGE_SIZE)[:, None] > jnp.arange(_FOX_PAGE_LANES)[None, :] // N_FOX_HEADS).astype(BF16)
    o_f = _fox_sample(page_table, qf_rows, cache_fox, logf_t, tri, kvf5[0], kvf5[1], lfn, t_new)
    h = _outproj(x2d, o_c.reshape(-1, NSA_WIDTH), o_s.reshape(-1, NSA_WIDTH), o_w.reshape(-1, NSA_WIDTH),
                 o_f.reshape(-1, FOX_WIDTH), sm, g_nsa_out, g_fox_out, wo)
    y = _mlp(h, g_mlp, g_final, wu, wd).reshape(batch, t_new, d)
    kv_shape = (1, batch, t_new, 2, N_NSA_KV, HEAD_DIM)
    return (y, ckv.reshape(kv_shape), skv.reshape(kv_shape),
            kvf.reshape(1, batch, t_new, 2, N_FOX_HEADS, HEAD_DIM), logf_new[None],
            win_all[:, t_new:].reshape(1, batch, wb, 2, N_NSA_KV, HEAD_DIM))


def kernel(x_prompt, x_sample, cache_cmp_kv, cache_slc_kv, cache_fox_kv, cache_fox_logf, state_win_kv, page_table,
           t5_table, g_attn, w_in, b_f, w_cmp, pe_cmp, g_nsa_out, g_fox_out, w_o, g_mlp, w_up, w_down, g_final):
    assert g_attn.shape[0] == 1, "single-layer step"
    assert x_prompt.shape[-1] == NSA_WIDTH + FOX_WIDTH
    packed = _pack_weights(w_in[0], b_f[0], w_cmp[0], pe_cmp[0], w_o[0], w_up[0], w_down[0])
    lw = (g_attn[0], g_nsa_out[0], g_fox_out[0], g_mlp[0], g_final)
    yp, cmp_p, slc_p, fox_p, logf_p, win_p = _prompt_pass(x_prompt, t5_table, lw, packed)
    ys, cmp_s, slc_s, fox_s, logf_s, win_s = _sample_pass(
        x_sample, cache_cmp_kv[0], cache_slc_kv[0], cache_fox_kv[0], cache_fox_logf[0], state_win_kv[0],
        page_table, t5_table, lw, packed)
    return (yp, ys, cmp_p, slc_p, fox_p, logf_p, win_p, cmp_s, slc_s, fox_s, logf_s, win_s)
```
